```python
import math
import jax, jax.numpy as jnp
from jax import lax
import numpy as np

D_MODEL = 1024
BATCH = 16
SEQ = 4096
DEPTH = 4
DEC_BATCH = 32
DEC_SEQ = 2048
PAST_LEN = 128

N_MIXERS = 2
N_ATTN_LAYERS = (DEPTH + N_MIXERS - 1) // N_MIXERS
N_CONV_LAYERS = DEPTH // N_MIXERS
DILATED_GROUPS = ((128, 1), (512, 4), (2048, 16))
N_GROUPS = len(DILATED_GROUPS)
HEADS_PER_GROUP = 8
HEAD_DIM = 128
N_HEADS_TOTAL = N_GROUPS * HEADS_PER_GROUP
ATTN_OUT_WIDTH = HEADS_PER_GROUP * HEAD_DIM
QKV_WIDTH = 3 * N_HEADS_TOTAL * HEAD_DIM
REL_BUCKETS = 32
REL_MAX_DIST = 1024
CONV_WIDTH = 31
CONV_PAD = (CONV_WIDTH - 1) // 2
D_FF = int(math.ceil(8 * D_MODEL / 3 / 256) * 256)
NEG_INF = -1e30
EPS = 1e-6

kernel_name = "hybrid_dilated_attn_conformer_encoder"


def _rmsnorm(x, g):
    xf = x.astype(jnp.float32)
    y = xf * lax.rsqrt(jnp.mean(xf * xf, axis=-1, keepdims=True) + EPS)
    return (y * g.astype(jnp.float32)).astype(x.dtype)


def _layernorm(x, g, b):
    xf = x.astype(jnp.float32)
    mu = jnp.mean(xf, axis=-1, keepdims=True)
    xc = xf - mu
    y = xc * lax.rsqrt(jnp.mean(xc * xc, axis=-1, keepdims=True) + EPS)
    return (y * g.astype(jnp.float32) + b.astype(jnp.float32)).astype(x.dtype)


def _t5_bucket(rel):
    half = REL_BUCKETS // 2
    max_exact = half // 2
    ret = jnp.where(rel > 0, half, 0)
    n = jnp.abs(rel)
    nf = jnp.maximum(n, 1).astype(jnp.float32)
    large = max_exact + (jnp.log(nf / max_exact) / math.log(REL_MAX_DIST / max_exact)
                         * (half - max_exact)).astype(jnp.int32)
    large = jnp.minimum(large, half - 1)
    return ret + jnp.where(n < max_exact, n, large)


def _dilated_group_attention(q, k, v, dil, half, rel_table_g):
    B, S, H, Dh = q.shape
    L = S // dil
    nb = -(-L // half)
    Lp = nb * half

    def to_res(t):
        return t.reshape(B, L, dil, H, Dh).transpose(0, 2, 1, 3, 4)

    qr = jnp.pad(to_res(q), ((0, 0), (0, 0), (0, Lp - L), (0, 0), (0, 0))).reshape(B, dil, nb, half, H, Dh)
    kv_pad = ((0, 0), (0, 0), (half, Lp - L + half), (0, 0), (0, 0))
    kr = jnp.pad(to_res(k), kv_pad).reshape(B, dil, nb + 2, half, H, Dh)
    vr = jnp.pad(to_res(v), kv_pad).reshape(B, dil, nb + 2, half, H, Dh)

    scores = jnp.concatenate(
        [jnp.einsum('brnqhd,brnkhd->brhnqk', qr, kr[:, :, j:j + nb]) for j in range(3)],
        axis=-1).astype(jnp.float32)

    qq = jnp.arange(half)[:, None]
    kk = jnp.arange(3 * half)[None, :]
    rel = kk - half - qq
    key_m = jnp.arange(nb)[:, None, None] * half + qq[None] + rel[None]
    valid = (jnp.abs(rel)[None] <= half) & (key_m >= 0) & (key_m < L)
    bias = rel_table_g.astype(jnp.float32)[_t5_bucket(rel * dil)]
    bias = bias.transpose(2, 0, 1)

    logits = scores + bias[None, None, :, None]
    logits = jnp.where(valid[None, None, None], logits, NEG_INF)
    lse = jax.nn.logsumexp(logits, axis=-1)
    p = jnp.exp(logits - lse[..., None]).astype(v.dtype)
    out = sum(jnp.einsum('brhnqk,brnkhd->brnqhd', p[..., j * half:(j + 1) * half], vr[:, :, j:j + nb])
              for j in range(3))
    out = out.reshape(B, dil, Lp, H, Dh)[:, :, :L].transpose(0, 2, 1, 3, 4).reshape(B, S, H, Dh)
    lse = lse.transpose(0, 1, 3, 4, 2).reshape(B, dil, Lp, H)[:, :, :L]
    lse = lse.transpose(0, 2, 1, 3).reshape(B, S, H)
    return out, lse


def _attention_mixer(h, w_qkv, q_gain, k_gain, w_o, rel_table):
    B, S, _ = h.shape
    qkv = (h @ w_qkv).reshape(B, S, 3, N_GROUPS, HEADS_PER_GROUP, HEAD_DIM)
    q = _rmsnorm(qkv[:, :, 0], q_gain) * (HEAD_DIM ** -0.5)
    k = _rmsnorm(qkv[:, :, 1], k_gain)
    v = qkv[:, :, 2]
    outs, lses = [], []
    for g, (win, dil) in enumerate(DILATED_GROUPS):
        o, l = _dilated_group_attention(
            q[:, :, g], k[:, :, g], v[:, :, g], dil, win // (2 * dil),
            rel_table[:, g * HEADS_PER_GROUP:(g + 1) * HEADS_PER_GROUP])
        outs.append(o)
        lses.append(l)
    wts = jax.nn.softmax(jnp.stack(lses), axis=0)
    o = jnp.einsum('gbsh,gbshd->bshd', wts, jnp.stack(outs).astype(jnp.float32)).astype(h.dtype)
    return o.reshape(B, S, ATTN_OUT_WIDTH) @ w_o


def _conv_mixer(h, w_pw1, b_pw1, w_dw, b_dw, ln_g, ln_b, w_pw2, b_pw2):
    a, gt = jnp.split(h @ w_pw1 + b_pw1, 2, axis=-1)
    u = a * jax.nn.sigmoid(gt)
    u = lax.conv_general_dilated(u, w_dw[:, None, :], (1,), [(CONV_PAD, CONV_PAD)],
                                 dimension_numbers=('NWC', 'WIO', 'NWC'),
                                 feature_group_count=D_MODEL) + b_dw
    u = jax.nn.silu(_layernorm(u, ln_g, ln_b))
    return u @ w_pw2 + b_pw2


def _swiglu(h, w_in, w_out):
    gate, up = jnp.split(h @ w_in, 2, axis=-1)
    return (jax.nn.silu(gate) * up) @ w_out


def _trunk(x, c, rel_bias_table, norm1_g, norm2_g, ada_w, ada_b,
           attn_w_qkv, attn_q_gain, attn_k_gain, attn_w_o,
           conv_w_pw1, conv_b_pw1, conv_w_dw, conv_b_dw, conv_ln_g, conv_ln_b,
           conv_w_pw2, conv_b_pw2, ffn_w_in, ffn_w_out):
    c_act = jax.nn.silu(c)
    for i in range(DEPTH):
        ada = c_act @ ada_w[i] + ada_b[i]
        sh1, sc1, g1, sh2, sc2, g2 = [t[:, None, :] for t in jnp.split(ada, 6, axis=-1)]
        h = _rmsnorm(x, norm1_g[i]) * (1 + sc1) + sh1
        j = i // N_MIXERS
        if i % N_MIXERS == 0:
            y = _attention_mixer(h, attn_w_qkv[j], attn_q_gain[j], attn_k_gain[j],
                                 attn_w_o[j], rel_bias_table)
        else:
            y = _conv_mixer(h, conv_w_pw1[j], conv_b_pw1[j], conv_w_dw[j], conv_b_dw[j],
                            conv_ln_g[j], conv_ln_b[j], conv_w_pw2[j], conv_b_pw2[j])
        x = x + g1 * y
        h = _rmsnorm(x, norm2_g[i]) * (1 + sc2) + sh2
        x = x + g2 * _swiglu(h, ffn_w_in[i], ffn_w_out[i])
    return x


def setup_inputs(seed: int = 0) -> dict:
    key = jax.random.key(seed)
    ks = jax.random.split(key, 24)
    f32 = jnp.float32
    nrm = lambda k, shape, s: jax.random.normal(k, shape, f32) * s
    D = D_MODEL
    return {
        "x_prompt": nrm(ks[0], (BATCH, SEQ, D), 1.0),
        "x_sample": nrm(ks[1], (DEC_BATCH, DEC_SEQ, D), 1.0),
        "c_prompt": nrm(ks[2], (BATCH, D), 1.0),
        "c_sample": nrm(ks[3], (DEC_BATCH, D), 1.0),
        "rel_bias_table": nrm(ks[4], (REL_BUCKETS, N_HEADS_TOTAL), 0.5),
        "norm1_g": 1.0 + nrm(ks[5], (DEPTH, D), 0.02),
        "norm2_g": 1.0 + nrm(ks[6], (DEPTH, D), 0.02),
        "ada_w": nrm(ks[7], (DEPTH, D, 6 * D), 0.3 * D ** -0.5),
        "ada_b": nrm(ks[8], (DEPTH, 6 * D), 0.01),
        "attn_w_qkv": nrm(ks[9], (N_ATTN_LAYERS, D, QKV_WIDTH), D ** -0.5),
        "attn_q_gain": 1.0 + nrm(ks[10], (N_ATTN_LAYERS, HEAD_DIM), 0.02),
        "attn_k_gain": 1.0 + nrm(ks[11], (N_ATTN_LAYERS, HEAD_DIM), 0.02),
        "attn_w_o": nrm(ks[12], (N_ATTN_LAYERS, ATTN_OUT_WIDTH, D), ATTN_OUT_WIDTH ** -0.5),
        "conv_w_pw1": nrm(ks[13], (N_CONV_LAYERS, D, 2 * D), D ** -0.5),
        "conv_b_pw1": nrm(ks[14], (N_CONV_LAYERS, 2 * D), 0.01),
        "conv_w_dw": nrm(ks[15], (N_CONV_LAYERS, CONV_WIDTH, D), CONV_WIDTH ** -0.5),
        "conv_b_dw": nrm(ks[16], (N_CONV_LAYERS, D), 0.01),
        "conv_ln_g": 1.0 + nrm(ks[17], (N_CONV_LAYERS, D), 0.02),
        "conv_ln_b": nrm(ks[18], (N_CONV_LAYERS, D), 0.01),
        "conv_w_pw2": nrm(ks[19], (N_CONV_LAYERS, D, D), D ** -0.5),
        "conv_b_pw2": nrm(ks[20], (N_CONV_LAYERS, D), 0.01),
        "ffn_w_in": nrm(ks[21], (DEPTH, D, 2 * D_FF), D ** -0.5),
        "ffn_w_out": nrm(ks[22], (DEPTH, D_FF, D), D_FF ** -0.5),
    }


def reference(x_prompt, x_sample, c_prompt, c_sample, rel_bias_table, norm1_g, norm2_g,
              ada_w, ada_b, attn_w_qkv, attn_q_gain, attn_k_gain, attn_w_o,
              conv_w_pw1, conv_b_pw1, conv_w_dw, conv_b_dw, conv_ln_g, conv_ln_b,
              conv_w_pw2, conv_b_pw2, ffn_w_in, ffn_w_out):
    y_prompt = _trunk(x_prompt, c_prompt, rel_bias_table, norm1_g, norm2_g, ada_w, ada_b,
                      attn_w_qkv, attn_q_gain, attn_k_gain, attn_w_o,
                      conv_w_pw1, conv_b_pw1, conv_w_dw, conv_b_dw, conv_ln_g, conv_ln_b,
                      conv_w_pw2, conv_b_pw2, ffn_w_in, ffn_w_out)
    y_sample = _trunk(x_sample, c_sample, rel_bias_table, norm1_g, norm2_g, ada_w, ada_b,
                      attn_w_qkv, attn_q_gain, attn_k_gain, attn_w_o,
                      conv_w_pw1, conv_b_pw1, conv_w_dw, conv_b_dw, conv_ln_g, conv_ln_b,
                      conv_w_pw2, conv_b_pw2, ffn_w_in, ffn_w_out)
    return (y_prompt, y_sample)
```

```python
import functools
import math

import jax
import jax.numpy as jnp
from jax import lax
from jax.experimental import pallas as pl
from jax.experimental.pallas import tpu as pltpu

F32 = jnp.float32
BF16 = jnp.bfloat16

D_MODEL = 1024
DEPTH = 4
N_MIXERS = 2
DILATED_GROUPS = ((128, 1), (512, 4), (2048, 16))
N_GROUPS = len(DILATED_GROUPS)
HEADS_PER_GROUP = 8
HEAD_DIM = 128
REL_BUCKETS = 32
REL_MAX_DIST = 1024
CONV_WIDTH = 31
CONV_PAD = (CONV_WIDTH - 1) // 2
D_FF = 2816
NEG_INF = -1e30
EPS = 1e-6

HALF = DILATED_GROUPS[0][0] // (2 * DILATED_GROUPS[0][1])
assert all(w // (2 * d) == HALF for w, d in DILATED_GROUPS)

TOKEN_TILE = 512
QUERY_BLOCK = 128
HALO_ROWS = 16
CONV_ROW_CHUNK = 32
SUBLANES = 8
CONV_SUB_TILE = 128
CONV_SHIFT_SLACK = (HALO_ROWS - CONV_PAD + CONV_WIDTH - 1) // SUBLANES * SUBLANES
FF_CHUNKS = ((0, 768), (768, 768), (1536, 768), (2304, 512))
VMEM_LIMIT_BYTES = 60000 * 1024


def _params(n_axes):
    return pltpu.CompilerParams(dimension_semantics=("parallel",) * n_axes,
                                vmem_limit_bytes=VMEM_LIMIT_BYTES)


def _resident(shape):
    nd = len(shape)
    return pl.BlockSpec(shape, lambda *_: (0,) * nd, pipeline_mode=pl.Buffered(1))


def _silu(x):
    return x * jax.nn.sigmoid(x)


def _mod_rmsnorm(x, gain, shift, scale):
    ms = jnp.mean(x * x, axis=-1, keepdims=True)
    y = x * lax.rsqrt(ms + EPS) * gain
    return y * (1.0 + scale) + shift


def _ada_kernel(c_ref, w_ref, b_ref, o_ref):
    ca = _silu(c_ref[...]).astype(BF16)
    o_ref[0] = jnp.dot(ca, w_ref[0].astype(BF16), preferred_element_type=F32) + b_ref[0]


def _ada_all(c, ada_w, ada_b):
    nb, d = c.shape
    depth, _, width = ada_w.shape
    tn = 1536
    assert width % tn == 0
    return pl.pallas_call(
        _ada_kernel,
        out_shape=jax.ShapeDtypeStruct((depth, nb, width), F32),
        grid=(depth, width // tn),
        in_specs=[pl.BlockSpec((nb, d), lambda i, j: (0, 0)),
                  pl.BlockSpec((1, d, tn), lambda i, j: (i, 0, j)),
                  pl.BlockSpec((1, 1, tn), lambda i, j: (i, 0, j))],
        out_specs=pl.BlockSpec((1, nb, tn), lambda i, j: (i, 0, j)),
        compiler_params=_params(2),
        name="ada_proj",
    )(c, ada_w, ada_b.reshape(depth, 1, width))


def _t5_bucket(rel):
    half = REL_BUCKETS // 2
    max_exact = half // 2
    ret = jnp.where(rel > 0, half, 0)
    n = jnp.abs(rel)
    nf = jnp.maximum(n, 1).astype(F32)
    large = max_exact + (jnp.log(nf / max_exact) / math.log(REL_MAX_DIST / max_exact)
                         * (half - max_exact)).astype(jnp.int32)
    large = jnp.minimum(large, half - 1)
    return ret + jnp.where(n < max_exact, n, large)


def _attn_geometry(seq_len, dil):
    sub_len = seq_len // dil
    qb = min(QUERY_BLOCK, sub_len)
    kw = min(qb + 2 * HALF, sub_len)
    nblk = sub_len // qb
    assert sub_len * dil == seq_len and nblk * qb == sub_len and qb % HALF == 0
    return sub_len, qb, kw, nblk


def _key_offset(n, qb, kw, sub_len):
    return min(max(n * qb - HALF, 0), sub_len - kw)


def _bias_bucket_index(seq_len, dil):
    sub_len, qb, kw, nblk = _attn_geometry(seq_len, dil)
    slots = []
    for n in (0, min(1, nblk - 1), nblk - 1):
        delta = n * qb - _key_offset(n, qb, kw, sub_len)
        rel = jnp.arange(kw)[None, :] - jnp.arange(qb)[:, None] - delta
        slots.append(jnp.where(jnp.abs(rel) <= HALF, _t5_bucket(rel * dil), -1))
    return jnp.stack(slots).astype(jnp.int32)


def _bias_kernel(tab_ref, idx_ref, o_ref, *, head0):
    h = pl.program_id(0)
    idx = idx_ref[...]
    acc = jnp.full(idx.shape, NEG_INF, F32)
    for b in range(REL_BUCKETS):
        acc = jnp.where(idx == b, tab_ref[b, head0 + h], acc)
    o_ref[0] = acc


def _expand_bias(rel_table, idx, group):
    _, qb, kw = idx.shape
    return pl.pallas_call(
        functools.partial(_bias_kernel, head0=group * HEADS_PER_GROUP),
        out_shape=jax.ShapeDtypeStruct((HEADS_PER_GROUP, 3, qb, kw), F32),
        grid=(HEADS_PER_GROUP,),
        in_specs=[pl.BlockSpec(memory_space=pltpu.SMEM),
                  pl.BlockSpec((3, qb, kw), lambda h: (0, 0, 0))],
        out_specs=pl.BlockSpec((1, 3, qb, kw), lambda h: (h, 0, 0, 0)),
        compiler_params=_params(1),
        name="rel_bias_expand",
    )(rel_table, idx)


def _qkv_kernel(x_ref, ada_ref, g_ref, w_ref, qg_ref, kg_ref, o_ref):
    h = _mod_rmsnorm(x_ref[0], g_ref[...], ada_ref[0, 0:1, :], ada_ref[0, 1:2, :]).astype(BF16)
    width = HEADS_PER_GROUP * HEAD_DIM
    for t in range(3):
        acc = jnp.dot(h, w_ref[:, t * width:(t + 1) * width], preferred_element_type=F32)
        for hd in range(HEADS_PER_GROUP):
            c = acc[:, hd * HEAD_DIM:(hd + 1) * HEAD_DIM]
            if t < 2:
                ms = jnp.mean(c * c, axis=-1, keepdims=True)
                c = c * lax.rsqrt(ms + EPS) * (qg_ref[...] if t == 0 else kg_ref[...])
            if t == 0:
                c = c * (HEAD_DIM ** -0.5)
            o_ref[0, t, hd] = c.astype(BF16)


def _qkv_proj(x, ada, gain, w, q_gain, k_gain, dil):
    b, s, d = x.shape
    sub_len = s // dil
    tm = min(TOKEN_TILE, sub_len)
    assert sub_len % tm == 0
    width = 3 * HEADS_PER_GROUP * HEAD_DIM
    return pl.pallas_call(
        _qkv_kernel,
        out_shape=jax.ShapeDtypeStruct((b, 3, HEADS_PER_GROUP, dil, sub_len, HEAD_DIM), BF16),
        grid=(b, dil, sub_len // tm),
        in_specs=[pl.BlockSpec((1, tm, d), lambda bi, r, i: (bi, i, r)),
                  pl.BlockSpec((1, 6, d), lambda bi, r, i: (bi, 0, 0)),
                  _resident((1, d)),
                  _resident((d, width)),
                  _resident((1, HEAD_DIM)),
                  _resident((1, HEAD_DIM))],
        out_specs=pl.BlockSpec((1, 3, HEADS_PER_GROUP, None, tm, HEAD_DIM),
                               lambda bi, r, i: (bi, 0, 0, r, i, 0)),
        compiler_params=_params(3),
        name=f"qkv_proj_d{dil}",
    )(x.reshape(b, sub_len, dil * d), ada, gain, w, q_gain, k_gain)


def _attn_kernel(qkv0_ref, qkv1_ref, qkv2_ref, b0_ref, b1_ref, b2_ref, o_ref, og_ref, lse_ref,
                 *, seq_len):
    for g, (qkv_ref, bias_ref) in enumerate(((qkv0_ref, b0_ref), (qkv1_ref, b1_ref),
                                              (qkv2_ref, b2_ref))):
        dil = DILATED_GROUPS[g][1]
        sub_len, qb, kw, nblk = _attn_geometry(seq_len, dil)

        def block(r, n, g=g, dil=dil, sub_len=sub_len, qb=qb, kw=kw, nblk=nblk,
                  qkv_ref=qkv_ref, bias_ref=bias_ref):
            q0 = pl.multiple_of(r * sub_len + n * qb, qb)
            koff = jnp.clip(n * qb - HALF, 0, sub_len - kw)
            k0 = pl.multiple_of(r * sub_len + koff, HALF)
            q = qkv_ref[0, 0, 0, pl.ds(q0, qb), :]
            k = qkv_ref[0, 1, 0, pl.ds(k0, kw), :]
            v = qkv_ref[0, 2, 0, pl.ds(k0, kw), :]
            slot = jnp.where(n == 0, 0, jnp.where(n == nblk - 1, 2, 1))
            s = lax.dot_general(q, k, (((1,), (1,)), ((), ())), preferred_element_type=F32)
            s = s + bias_ref[0, slot]
            m = jnp.max(s, axis=-1, keepdims=True)
            p = jnp.exp(s - m)
            l = jnp.sum(p, axis=-1, keepdims=True)
            o = jnp.dot(p.astype(BF16), v, preferred_element_type=F32) / l
            lse = jnp.broadcast_to(m + jnp.log(l), (qb, HEAD_DIM))
            start = r + n * (qb * dil)
            rows = pl.ds(start, qb) if dil == 1 else pl.ds(start, qb, stride=dil)
            og_ref[g, rows, :] = o
            lse_ref[g, rows, :] = lse

        def residue(r, carry, block=block, nblk=nblk):
            def qblock(n, c):
                block(r, n)
                return c
            return lax.fori_loop(0, nblk, qblock, carry)

        lax.fori_loop(0, dil, residue, 0)

    rc = 256

    def merge(c, carry):
        rows = pl.ds(pl.multiple_of(c * rc, rc), rc)
        l0, l1, l2 = lse_ref[0, rows, :], lse_ref[1, rows, :], lse_ref[2, rows, :]
        mx = jnp.maximum(jnp.maximum(l0, l1), l2)
        e0, e1, e2 = jnp.exp(l0 - mx), jnp.exp(l1 - mx), jnp.exp(l2 - mx)
        o = (e0 * og_ref[0, rows, :] + e1 * og_ref[1, rows, :] + e2 * og_ref[2, rows, :])
        o_ref[0, 0, rows, :] = (o / (e0 + e1 + e2)).astype(BF16)
        return carry

    lax.fori_loop(0, seq_len // rc, merge, 0)


def _attention(qkvs, biases):
    b = qkvs[0].shape[0]
    s = qkvs[0].shape[3] * qkvs[0].shape[4]
    qkvs = [t.reshape(b, 3, HEADS_PER_GROUP, s, HEAD_DIM) for t in qkvs]
    qkv_spec = pl.BlockSpec((1, 3, 1, s, HEAD_DIM), lambda bi, h: (bi, 0, h, 0, 0))
    bias_specs = [pl.BlockSpec((1,) + t.shape[1:], lambda bi, h: (h, 0, 0, 0)) for t in biases]
    return pl.pallas_call(
        functools.partial(_attn_kernel, seq_len=s),
        out_shape=jax.ShapeDtypeStruct((b, HEADS_PER_GROUP, s, HEAD_DIM), BF16),
        grid=(b, HEADS_PER_GROUP),
        in_specs=[qkv_spec] * N_GROUPS + bias_specs,
        out_specs=pl.BlockSpec((1, 1, s, HEAD_DIM), lambda bi, h: (bi, h, 0, 0)),
        scratch_shapes=[pltpu.VMEM((N_GROUPS, s, HEAD_DIM), F32),
                        pltpu.VMEM((N_GROUPS, s, HEAD_DIM), F32)],
        compiler_params=_params(2),
        name="dilated_attention",
    )(*qkvs, *biases)


def _residual_ffn(x, y, ada_ref, g2_ref, w_in_ref, w_out_ref):
    x1 = x + ada_ref[0, 2:3, :] * y
    h = _mod_rmsnorm(x1, g2_ref[...], ada_ref[0, 3:4, :], ada_ref[0, 4:5, :]).astype(BF16)
    acc = None
    for c0, cw in FF_CHUNKS:
        gate = jnp.dot(h, w_in_ref[:, c0:c0 + cw], preferred_element_type=F32)
        up = jnp.dot(h, w_in_ref[:, D_FF + c0:D_FF + c0 + cw], preferred_element_type=F32)
        a = (_silu(gate) * up).astype(BF16)
        part = jnp.dot(a, w_out_ref[c0:c0 + cw, :], preferred_element_type=F32)
        acc = part if acc is None else acc + part
    return x1 + ada_ref[0, 5:6, :] * acc


def _attn_tail_kernel(o_ref, x_ref, ada_ref, wo_ref, g2_ref, w_in_ref, w_out_ref, out_ref):
    o = jnp.concatenate([o_ref[0, h] for h in range(HEADS_PER_GROUP)], axis=-1)
    y = jnp.dot(o, wo_ref[...], preferred_element_type=F32)
    out_ref[0] = _residual_ffn(x_ref[0], y, ada_ref, g2_ref, w_in_ref, w_out_ref)


def _attn_tail(o, x, ada, w_o, gain2, w_in, w_out):
    b, s, d = x.shape
    tm = min(TOKEN_TILE, s)
    return pl.pallas_call(
        _attn_tail_kernel,
        out_shape=jax.ShapeDtypeStruct((b, s, d), F32),
        grid=(b, s // tm),
        in_specs=[pl.BlockSpec((1, HEADS_PER_GROUP, tm, HEAD_DIM), lambda bi, i: (bi, 0, i, 0)),
                  pl.BlockSpec((1, tm, d), lambda bi, i: (bi, i, 0)),
                  pl.BlockSpec((1, 6, d), lambda bi, i: (bi, 0, 0)),
                  _resident(w_o.shape), _resident((1, d)),
                  _resident(w_in.shape), _resident(w_out.shape)],
        out_specs=pl.BlockSpec((1, tm, d), lambda bi, i: (bi, i, 0)),
        compiler_params=_params(2),
        name="attn_out_ffn",
    )(o, x, ada, w_o, gain2, w_in, w_out)


def _conv_head_kernel(x_ref, ada_ref, g_ref, w_ref, b_ref, u_ref):
    d = x_ref.shape[-1]
    h = _mod_rmsnorm(x_ref[0], g_ref[...], ada_ref[0, 0:1, :], ada_ref[0, 1:2, :]).astype(BF16)
    a = jnp.dot(h, w_ref[:, :d], preferred_element_type=F32) + b_ref[:, :d]
    gt = jnp.dot(h, w_ref[:, d:], preferred_element_type=F32) + b_ref[:, d:]
    u_ref[0] = a * jax.nn.sigmoid(gt)


def _conv_head(x, ada, gain, w_pw1, b_pw1):
    b, s, d = x.shape
    tm = min(TOKEN_TILE, s)
    return pl.pallas_call(
        _conv_head_kernel,
        out_shape=jax.ShapeDtypeStruct((b, s, d), F32),
        grid=(b, s // tm),
        in_specs=[pl.BlockSpec((1, tm, d), lambda bi, i: (bi, i, 0)),
                  pl.BlockSpec((1, 6, d), lambda bi, i: (bi, 0, 0)),
                  _resident((1, d)), _resident(w_pw1.shape), _resident((1, 2 * d))],
        out_specs=pl.BlockSpec((1, tm, d), lambda bi, i: (bi, i, 0)),
        compiler_params=_params(2),
        name="conv_pw1_glu",
    )(x, ada, gain, w_pw1, b_pw1)


def _conv_tail_kernel(prev_ref, u_ref, next_ref, x_ref, ada_ref, wdw_ref, bdw_ref, lng_ref,
                      lnb_ref, w2_ref, b2_ref, g2_ref, w_in_ref, w_out_ref, out_ref,
                      win_ref, shift_ref, cv_ref):
    tm, d = u_ref.shape[1], u_ref.shape[2]
    i = pl.program_id(1)
    last = pl.num_programs(1) - 1
    win_ref[0:HALO_ROWS, :] = jnp.where(i > 0, prev_ref[0], 0.0)
    win_ref[HALO_ROWS:HALO_ROWS + tm, :] = u_ref[0]
    win_ref[HALO_ROWS + tm:, :] = jnp.where(i < last, next_ref[0], 0.0)

    for sub in range(tm // CONV_SUB_TILE):
        base = sub * CONV_SUB_TILE
        for b in range(SUBLANES):
            shift_ref[b] = win_ref[base + b:base + b + CONV_SUB_TILE + CONV_SHIFT_SLACK, :]

        def conv_rows(c, carry, base=base):
            r0 = pl.multiple_of(c * CONV_ROW_CHUNK, CONV_ROW_CHUNK)
            acc = jnp.zeros((CONV_ROW_CHUNK, d), F32)
            for k in range(CONV_WIDTH):
                a, b = divmod(HALO_ROWS - CONV_PAD + k, SUBLANES)
                rows = pl.ds(pl.multiple_of(r0 + SUBLANES * a, SUBLANES), CONV_ROW_CHUNK)
                acc = acc + shift_ref[b, rows, :] * wdw_ref[k:k + 1, :]
            cv_ref[pl.ds(base + r0, CONV_ROW_CHUNK), :] = acc
            return carry

        lax.fori_loop(0, CONV_SUB_TILE // CONV_ROW_CHUNK, conv_rows, 0)

    cv = cv_ref[...] + bdw_ref[...]
    mu = jnp.mean(cv, axis=-1, keepdims=True)
    xc = cv - mu
    var = jnp.mean(xc * xc, axis=-1, keepdims=True)
    ln = xc * lax.rsqrt(var + EPS) * lng_ref[...] + lnb_ref[...]
    y = jnp.dot(_silu(ln).astype(BF16), w2_ref[...], preferred_element_type=F32) + b2_ref[...]
    out_ref[0] = _residual_ffn(x_ref[0], y, ada_ref, g2_ref, w_in_ref, w_out_ref)


def _conv_tail(u, x, ada, w_dw, b_dw, ln_g, ln_b, w_pw2, b_pw2, gain2, w_in, w_out):
    b, s, d = x.shape
    tm = min(TOKEN_TILE, s)
    assert s % tm == 0 and tm % CONV_SUB_TILE == 0 and CONV_SUB_TILE % CONV_ROW_CHUNK == 0
    hb = tm // HALO_ROWS
    n_halo = s // HALO_ROWS
    return pl.pallas_call(
        _conv_tail_kernel,
        out_shape=jax.ShapeDtypeStruct((b, s, d), F32),
        grid=(b, s // tm),
        in_specs=[pl.BlockSpec((1, HALO_ROWS, d), lambda bi, i: (bi, jnp.maximum(i * hb - 1, 0), 0)),
                  pl.BlockSpec((1, tm, d), lambda bi, i: (bi, i, 0)),
                  pl.BlockSpec((1, HALO_ROWS, d),
                               lambda bi, i: (bi, jnp.minimum((i + 1) * hb, n_halo - 1), 0)),
                  pl.BlockSpec((1, tm, d), lambda bi, i: (bi, i, 0)),
                  pl.BlockSpec((1, 6, d), lambda bi, i: (bi, 0, 0)),
                  _resident(w_dw.shape), _resident((1, d)), _resident((1, d)), _resident((1, d)),
                  _resident(w_pw2.shape), _resident((1, d)), _resident((1, d)),
                  _resident(w_in.shape), _resident(w_out.shape)],
        out_specs=pl.BlockSpec((1, tm, d), lambda bi, i: (bi, i, 0)),
        scratch_shapes=[pltpu.VMEM((tm + 2 * HALO_ROWS, d), F32),
                        pltpu.VMEM((SUBLANES, CONV_SUB_TILE + CONV_SHIFT_SLACK, d), F32),
                        pltpu.VMEM((tm, d), F32)],
        compiler_params=_params(2),
        name="conv_dw_pw2_ffn",
    )(u, u, u, x, ada, w_dw, b_dw, ln_g, ln_b, w_pw2, b_pw2, gain2, w_in, w_out)


def _trunk(x, ada, biases, p):
    d = x.shape[-1]
    row = lambda v: v.reshape(1, -1)
    for i in range(DEPTH):
        j = i // N_MIXERS
        gain1, gain2 = row(p["norm1_g"][i]), row(p["norm2_g"][i])
        w_in, w_out = p["ffn_w_in"][i], p["ffn_w_out"][i]
        if i % N_MIXERS == 0:
            w4 = p["attn_w_qkv"][j].reshape(d, 3, N_GROUPS, HEADS_PER_GROUP * HEAD_DIM)
            qkvs = [_qkv_proj(x, ada[i], gain1, w4[:, :, g, :].reshape(d, -1),
                              row(p["attn_q_gain"][j]), row(p["attn_k_gain"][j]), dil)
                    for g, (_, dil) in enumerate(DILATED_GROUPS)]
            o = _attention(qkvs, biases)
            x = _attn_tail(o, x, ada[i], p["attn_w_o"][j], gain2, w_in, w_out)
        else:
            u = _conv_head(x, ada[i], gain1, p["conv_w_pw1"][j], row(p["conv_b_pw1"][j]))
            x = _conv_tail(u, x, ada[i], p["conv_w_dw"][j], row(p["conv_b_dw"][j]),
                           row(p["conv_ln_g"][j]), row(p["conv_ln_b"][j]), p["conv_w_pw2"][j],
                           row(p["conv_b_pw2"][j]), gain2, w_in, w_out)
    return x


def kernel(x_prompt, x_sample, c_prompt, c_sample, rel_bias_table, norm1_g, norm2_g, ada_w, ada_b, attn_w_qkv, attn_q_gain, attn_k_gain, attn_w_o, conv_w_pw1, conv_b_pw1, conv_w_dw, conv_b_dw, conv_ln_g, conv_ln_b, conv_w_pw2, conv_b_pw2, ffn_w_in, ffn_w_out):
    p = dict(norm1_g=norm1_g, norm2_g=norm2_g, attn_q_gain=attn_q_gain, attn_k_gain=attn_k_gain,
             conv_b_pw1=conv_b_pw1, conv_w_dw=conv_w_dw, conv_b_dw=conv_b_dw, conv_ln_g=conv_ln_g,
             conv_ln_b=conv_ln_b, conv_b_pw2=conv_b_pw2,
             attn_w_qkv=attn_w_qkv.astype(BF16), attn_w_o=attn_w_o.astype(BF16),
             conv_w_pw1=conv_w_pw1.astype(BF16), conv_w_pw2=conv_w_pw2.astype(BF16),
             ffn_w_in=ffn_w_in.astype(BF16), ffn_w_out=ffn_w_out.astype(BF16))
    nb_p = c_prompt.shape[0]
    d = x_prompt.shape[-1]
    c_all = jnp.concatenate([c_prompt, c_sample], axis=0)
    ada = _ada_all(c_all, ada_w, ada_b).reshape(DEPTH, c_all.shape[0], 6, d)

    bias_cache = {}

    def biases_for(seq_len):
        out = []
        for g, (_, dil) in enumerate(DILATED_GROUPS):
            key = (g,) + _attn_geometry(seq_len, dil)
            if key not in bias_cache:
                bias_cache[key] = _expand_bias(rel_bias_table, _bias_bucket_index(seq_len, dil), g)
            out.append(bias_cache[key])
        return out

    y_prompt = _trunk(x_prompt, ada[:, :nb_p], biases_for(x_prompt.shape[1]), p)
    y_sample = _trunk(x_sample, ada[:, nb_p:], biases_for(x_sample.shape[1]), p)
    return (y_prompt, y_sample)
```

```python
import functools
import math

import jax
import jax.numpy as jnp
from jax import lax
from jax.experimental import pallas as pl
from jax.experimental.pallas import tpu as pltpu

F32 = jnp.float32
BF16 = jnp.bfloat16

D_MODEL = 1024
DEPTH = 4
N_MIXERS = 2
DILATED_GROUPS = ((128, 1), (512, 4), (2048, 16))
N_GROUPS = len(DILATED_GROUPS)
HEADS_PER_GROUP = 8
HEAD_DIM = 128
REL_BUCKETS = 32
REL_MAX_DIST = 1024
CONV_WIDTH = 31
CONV_PAD = (CONV_WIDTH - 1) // 2
D_FF = 2816
NEG_INF = -1e30
EPS = 1e-6

HALF = DILATED_GROUPS[0][0] // (2 * DILATED_GROUPS[0][1])
assert all(w // (2 * d) == HALF for w, d in DILATED_GROUPS)

TOKEN_TILE = 512
QUERY_BLOCK = 128
ATTN_BLOCKS_PER_STEP = 8
HALO_ROWS = 16
CONV_ROW_CHUNK = 32
SUBLANES = 8
BF16_SUBLANES = 16
LANES = 128
CONV_SUB_TILE = 128
CONV_SHIFT_SLACK = (HALO_ROWS - CONV_PAD + CONV_WIDTH - 1) // SUBLANES * SUBLANES
FF_CHUNKS = ((0, 768), (768, 768), (1536, 768), (2304, 512))
VMEM_LIMIT_BYTES = 60000 * 1024


def _params(n_axes):
    return pltpu.CompilerParams(dimension_semantics=("parallel",) * n_axes,
                                vmem_limit_bytes=VMEM_LIMIT_BYTES)


def _resident(shape):
    nd = len(shape)
    return pl.BlockSpec(shape, lambda *_: (0,) * nd, pipeline_mode=pl.Buffered(1))


def _silu(x):
    return x * jax.nn.sigmoid(x)


def _mod_rmsnorm(x, gain, shift, scale):
    ms = jnp.mean(x * x, axis=-1, keepdims=True)
    y = x * lax.rsqrt(ms + EPS) * gain
    return y * (1.0 + scale) + shift


def _ada_kernel(c_ref, w_ref, b_ref, o_ref):
    ca = _silu(c_ref[...]).astype(BF16)
    o_ref[0] = jnp.dot(ca, w_ref[0].astype(BF16), preferred_element_type=F32) + b_ref[0]


def _ada_all(c, ada_w, ada_b):
    nb, d = c.shape
    depth, _, width = ada_w.shape
    tn = 1536
    assert width % tn == 0
    return pl.pallas_call(
        _ada_kernel,
        out_shape=jax.ShapeDtypeStruct((depth, nb, width), F32),
        grid=(depth, width // tn),
        in_specs=[pl.BlockSpec((nb, d), lambda i, j: (0, 0)),
                  pl.BlockSpec((1, d, tn), lambda i, j: (i, 0, j)),
                  pl.BlockSpec((1, 1, tn), lambda i, j: (i, 0, j))],
        out_specs=pl.BlockSpec((1, nb, tn), lambda i, j: (i, 0, j)),
        compiler_params=_params(2),
        name="ada_proj",
    )(c, ada_w, ada_b.reshape(depth, 1, width))


def _t5_bucket(rel):
    half = REL_BUCKETS // 2
    max_exact = half // 2
    ret = jnp.where(rel > 0, half, 0)
    n = jnp.abs(rel)
    nf = jnp.maximum(n, 1).astype(F32)
    large = max_exact + (jnp.log(nf / max_exact) / math.log(REL_MAX_DIST / max_exact)
                         * (half - max_exact)).astype(jnp.int32)
    large = jnp.minimum(large, half - 1)
    return ret + jnp.where(n < max_exact, n, large)


def _attn_geometry(seq_len, dil):
    sub_len = seq_len // dil
    qb = min(QUERY_BLOCK, sub_len)
    kw = min(qb + 2 * HALF, sub_len)
    nblk = sub_len // qb
    assert sub_len * dil == seq_len and nblk * qb == sub_len and qb % HALF == 0
    return sub_len, qb, kw, nblk


def _key_offset(n, qb, kw, sub_len):
    return min(max(n * qb - HALF, 0), sub_len - kw)


def _bias_bucket_index(seq_len, dil):
    sub_len, qb, kw, nblk = _attn_geometry(seq_len, dil)
    slots = []
    for n in (0, min(1, nblk - 1), nblk - 1):
        delta = n * qb - _key_offset(n, qb, kw, sub_len)
        rel = jnp.arange(kw)[None, :] - jnp.arange(qb)[:, None] - delta
        slots.append(jnp.where(jnp.abs(rel) <= HALF, _t5_bucket(rel * dil), -1))
    return jnp.stack(slots).astype(jnp.int32)


def _bias_kernel(tab_ref, idx_ref, o_ref, *, head0):
    h = pl.program_id(0)
    idx = idx_ref[...]
    acc = jnp.full(idx.shape, NEG_INF, F32)
    for b in range(REL_BUCKETS):
        acc = jnp.where(idx == b, tab_ref[b, head0 + h], acc)
    o_ref[0] = acc


def _expand_bias(rel_table, idx, group):
    _, qb, kw = idx.shape
    return pl.pallas_call(
        functools.partial(_bias_kernel, head0=group * HEADS_PER_GROUP),
        out_shape=jax.ShapeDtypeStruct((HEADS_PER_GROUP, 3, qb, kw), F32),
        grid=(HEADS_PER_GROUP,),
        in_specs=[pl.BlockSpec(memory_space=pltpu.SMEM),
                  pl.BlockSpec((3, qb, kw), lambda h: (0, 0, 0))],
        out_specs=pl.BlockSpec((1, 3, qb, kw), lambda h: (h, 0, 0, 0)),
        compiler_params=_params(1),
        name="rel_bias_expand",
    )(rel_table, idx)


def _qkv_kernel(x_ref, ada_ref, g_ref, w_ref, qg_ref, kg_ref, o_ref, *scratch, dil):
    tm, d = x_ref.shape[1], x_ref.shape[2]
    sub = tm // dil
    h = _mod_rmsnorm(x_ref[0], g_ref[...], ada_ref[0, 0:1, :], ada_ref[0, 1:2, :])
    if dil == 1:
        h = h.astype(BF16)
    else:
        slab_ref, perm_ref = scratch
        for c in range(d // LANES):
            slab_ref[c] = h[:, c * LANES:(c + 1) * LANES]
        for c in range(d // LANES):
            for r in range(dil):
                perm_ref[r * sub:(r + 1) * sub, c * LANES:(c + 1) * LANES] = (
                    slab_ref[c, pl.ds(r, sub, stride=dil), :].astype(BF16))
        h = perm_ref[...]
    width = HEADS_PER_GROUP * HEAD_DIM
    for t in range(3):
        acc = jnp.dot(h, w_ref[:, t * width:(t + 1) * width], preferred_element_type=F32)
        for hd in range(HEADS_PER_GROUP):
            c = acc[:, hd * HEAD_DIM:(hd + 1) * HEAD_DIM]
            if t < 2:
                ms = jnp.mean(c * c, axis=-1, keepdims=True)
                c = c * lax.rsqrt(ms + EPS) * (qg_ref[...] if t == 0 else kg_ref[...])
            if t == 0:
                c = c * (HEAD_DIM ** -0.5)
            o_ref[0, t, hd] = c.astype(BF16).reshape(dil, sub, HEAD_DIM)


def _qkv_proj(x, ada, gain, w, q_gain, k_gain, dil):
    b, s, d = x.shape
    sub_len = s // dil
    tm = min(TOKEN_TILE, s)
    assert s % tm == 0 and tm % (dil * BF16_SUBLANES) == 0
    width = 3 * HEADS_PER_GROUP * HEAD_DIM
    scratch = [] if dil == 1 else [pltpu.VMEM((d // LANES, tm, LANES), F32),
                                   pltpu.VMEM((tm, d), BF16)]
    return pl.pallas_call(
        functools.partial(_qkv_kernel, dil=dil),
        out_shape=jax.ShapeDtypeStruct((b, 3, HEADS_PER_GROUP, dil, sub_len, HEAD_DIM), BF16),
        grid=(b, s // tm),
        in_specs=[pl.BlockSpec((1, tm, d), lambda bi, i: (bi, i, 0)),
                  pl.BlockSpec((1, 6, d), lambda bi, i: (bi, 0, 0)),
                  _resident((1, d)),
                  _resident((d, width)),
                  _resident((1, HEAD_DIM)),
                  _resident((1, HEAD_DIM))],
        out_specs=pl.BlockSpec((1, 3, HEADS_PER_GROUP, dil, tm // dil, HEAD_DIM),
                               lambda bi, i: (bi, 0, 0, 0, i, 0)),
        scratch_shapes=scratch,
        compiler_params=_params(2),
        name=f"qkv_proj_d{dil}",
    )(x, ada, gain, w, q_gain, k_gain)


def _attn_kernel(qkv0_ref, qkv1_ref, qkv2_ref, b0_ref, b1_ref, b2_ref, o_ref, og_ref, lse_ref,
                 *, seq_len):
    for g, (qkv_ref, bias_ref) in enumerate(((qkv0_ref, b0_ref), (qkv1_ref, b1_ref),
                                              (qkv2_ref, b2_ref))):
        dil = DILATED_GROUPS[g][1]
        sub_len, qb, kw, nblk = _attn_geometry(seq_len, dil)

        def block(r, n, g=g, dil=dil, sub_len=sub_len, qb=qb, kw=kw, nblk=nblk,
                  qkv_ref=qkv_ref, bias_ref=bias_ref):
            q0 = pl.multiple_of(r * sub_len + n * qb, qb)
            koff = jnp.clip(n * qb - HALF, 0, sub_len - kw)
            k0 = pl.multiple_of(r * sub_len + koff, HALF)
            q = qkv_ref[0, 0, 0, pl.ds(q0, qb), :]
            k = qkv_ref[0, 1, 0, pl.ds(k0, kw), :]
            v = qkv_ref[0, 2, 0, pl.ds(k0, kw), :]
            slot = jnp.where(n == 0, 0, jnp.where(n == nblk - 1, 2, 1))
            s = lax.dot_general(q, k, (((1,), (1,)), ((), ())), preferred_element_type=F32)
            s = s + bias_ref[0, slot]
            m = jnp.max(s, axis=-1, keepdims=True)
            p = jnp.exp(s - m)
            l = jnp.sum(p, axis=-1, keepdims=True)
            o = jnp.dot(p.astype(BF16), v, preferred_element_type=F32) / l
            lse = jnp.broadcast_to(m + jnp.log(l), (qb, HEAD_DIM))
            start = r + n * (qb * dil)
            rows = pl.ds(start, qb) if dil == 1 else pl.ds(start, qb, stride=dil)
            og_ref[g, rows, :] = o
            lse_ref[g, rows, :] = lse

        total = dil * nblk
        par = math.gcd(total, ATTN_BLOCKS_PER_STEP)

        def step(it, carry, block=block, nblk=nblk, par=par):
            for u in range(par):
                idx = it * par + u
                block(lax.div(idx, nblk), lax.rem(idx, nblk))
            return carry

        lax.fori_loop(0, total // par, step, 0)

    rc = 256

    def merge(c, carry):
        rows = pl.ds(pl.multiple_of(c * rc, rc), rc)
        l0, l1, l2 = lse_ref[0, rows, :], lse_ref[1, rows, :], lse_ref[2, rows, :]
        mx = jnp.maximum(jnp.maximum(l0, l1), l2)
        e0, e1, e2 = jnp.exp(l0 - mx), jnp.exp(l1 - mx), jnp.exp(l2 - mx)
        o = (e0 * og_ref[0, rows, :] + e1 * og_ref[1, rows, :] + e2 * og_ref[2, rows, :])
        o_ref[0, 0, rows, :] = (o / (e0 + e1 + e2)).astype(BF16)
        return carry

    lax.fori_loop(0, seq_len // rc, merge, 0)


def _attention(qkvs, biases):
    b = qkvs[0].shape[0]
    s = qkvs[0].shape[3] * qkvs[0].shape[4]
    qkvs = [t.reshape(b, 3, HEADS_PER_GROUP, s, HEAD_DIM) for t in qkvs]
    qkv_spec = pl.BlockSpec((1, 3, 1, s, HEAD_DIM), lambda bi, h: (bi, 0, h, 0, 0))
    bias_specs = [pl.BlockSpec((1,) + t.shape[1:], lambda bi, h: (h, 0, 0, 0)) for t in biases]
    return pl.pallas_call(
        functools.partial(_attn_kernel, seq_len=s),
        out_shape=jax.ShapeDtypeStruct((b, HEADS_PER_GROUP, s, HEAD_DIM), BF16),
        grid=(b, HEADS_PER_GROUP),
        in_specs=[qkv_spec] * N_GROUPS + bias_specs,
        out_specs=pl.BlockSpec((1, 1, s, HEAD_DIM), lambda bi, h: (bi, h, 0, 0)),
        scratch_shapes=[pltpu.VMEM((N_GROUPS, s, HEAD_DIM), F32),
                        pltpu.VMEM((N_GROUPS, s, HEAD_DIM), F32)],
        compiler_params=_params(2),
        name="dilated_attention",
    )(*qkvs, *biases)


def _residual_ffn(x, y, ada_ref, g2_ref, w_in_ref, w_out_ref):
    x1 = x + ada_ref[0, 2:3, :] * y
    h = _mod_rmsnorm(x1, g2_ref[...], ada_ref[0, 3:4, :], ada_ref[0, 4:5, :]).astype(BF16)
    acc = None
    for c0, cw in FF_CHUNKS:
        gate = jnp.dot(h, w_in_ref[:, c0:c0 + cw], preferred_element_type=F32)
        up = jnp.dot(h, w_in_ref[:, D_FF + c0:D_FF + c0 + cw], preferred_element_type=F32)
        a = (_silu(gate) * up).astype(BF16)
        part = jnp.dot(a, w_out_ref[c0:c0 + cw, :], preferred_element_type=F32)
        acc = part if acc is None else acc + part
    return x1 + ada_ref[0, 5:6, :] * acc


def _attn_tail_kernel(o_ref, x_ref, ada_ref, wo_ref, g2_ref, w_in_ref, w_out_ref, out_ref):
    o = jnp.concatenate([o_ref[0, h] for h in range(HEADS_PER_GROUP)], axis=-1)
    y = jnp.dot(o, wo_ref[...], preferred_element_type=F32)
    out_ref[0] = _residual_ffn(x_ref[0], y, ada_ref, g2_ref, w_in_ref, w_out_ref)


def _attn_tail(o, x, ada, w_o, gain2, w_in, w_out):
    b, s, d = x.shape
    tm = min(TOKEN_TILE, s)
    return pl.pallas_call(
        _attn_tail_kernel,
        out_shape=jax.ShapeDtypeStruct((b, s, d), F32),
        grid=(b, s // tm),
        in_specs=[pl.BlockSpec((1, HEADS_PER_GROUP, tm, HEAD_DIM), lambda bi, i: (bi, 0, i, 0)),
                  pl.BlockSpec((1, tm, d), lambda bi, i: (bi, i, 0)),
                  pl.BlockSpec((1, 6, d), lambda bi, i: (bi, 0, 0)),
                  _resident(w_o.shape), _resident((1, d)),
                  _resident(w_in.shape), _resident(w_out.shape)],
        out_specs=pl.BlockSpec((1, tm, d), lambda bi, i: (bi, i, 0)),
        compiler_params=_params(2),
        name="attn_out_ffn",
    )(o, x, ada, w_o, gain2, w_in, w_out)


def _conv_head_kernel(x_ref, ada_ref, g_ref, w_ref, b_ref, u_ref):
    d = x_ref.shape[-1]
    h = _mod_rmsnorm(x_ref[0], g_ref[...], ada_ref[0, 0:1, :], ada_ref[0, 1:2, :]).astype(BF16)
    a = jnp.dot(h, w_ref[:, :d], preferred_element_type=F32) + b_ref[:, :d]
    gt = jnp.dot(h, w_ref[:, d:], preferred_element_type=F32) + b_ref[:, d:]
    u_ref[0] = a * jax.nn.sigmoid(gt)


def _conv_head(x, ada, gain, w_pw1, b_pw1):
    b, s, d = x.shape
    tm = min(TOKEN_TILE, s)
    return pl.pallas_call(
        _conv_head_kernel,
        out_shape=jax.ShapeDtypeStruct((b, s, d), F32),
        grid=(b, s // tm),
        in_specs=[pl.BlockSpec((1, tm, d), lambda bi, i: (bi, i, 0)),
                  pl.BlockSpec((1, 6, d), lambda bi, i: (bi, 0, 0)),
                  _resident((1, d)), _resident(w_pw1.shape), _resident((1, 2 * d))],
        out_specs=pl.BlockSpec((1, tm, d), lambda bi, i: (bi, i, 0)),
        compiler_params=_params(2),
        name="conv_pw1_glu",
    )(x, ada, gain, w_pw1, b_pw1)


def _conv_tail_kernel(prev_ref, u_ref, next_ref, x_ref, ada_ref, wdw_ref, bdw_ref, lng_ref,
                      lnb_ref, w2_ref, b2_ref, g2_ref, w_in_ref, w_out_ref, out_ref,
                      win_ref, shift_ref, cv_ref):
    tm, d = u_ref.shape[1], u_ref.shape[2]
    i = pl.program_id(1)
    last = pl.num_programs(1) - 1
    win_ref[0:HALO_ROWS, :] = jnp.where(i > 0, prev_ref[0], 0.0)
    win_ref[HALO_ROWS:HALO_ROWS + tm, :] = u_ref[0]
    win_ref[HALO_ROWS + tm:, :] = jnp.where(i < last, next_ref[0], 0.0)

    for sub in range(tm // CONV_SUB_TILE):
        base = sub * CONV_SUB_TILE
        for b in range(SUBLANES):
            shift_ref[b] = win_ref[base + b:base + b + CONV_SUB_TILE + CONV_SHIFT_SLACK, :]

        def conv_rows(c, carry, base=base):
            r0 = pl.multiple_of(c * CONV_ROW_CHUNK, CONV_ROW_CHUNK)
            acc = jnp.zeros((CONV_ROW_CHUNK, d), F32)
            for k in range(CONV_WIDTH):
                a, b = divmod(HALO_ROWS - CONV_PAD + k, SUBLANES)
                rows = pl.ds(pl.multiple_of(r0 + SUBLANES * a, SUBLANES), CONV_ROW_CHUNK)
                acc = acc + shift_ref[b, rows, :] * wdw_ref[k:k + 1, :]
            cv_ref[pl.ds(base + r0, CONV_ROW_CHUNK), :] = acc
            return carry

        lax.fori_loop(0, CONV_SUB_TILE // CONV_ROW_CHUNK, conv_rows, 0)

    cv = cv_ref[...] + bdw_ref[...]
    mu = jnp.mean(cv, axis=-1, keepdims=True)
    xc = cv - mu
    var = jnp.mean(xc * xc, axis=-1, keepdims=True)
    ln = xc * lax.rsqrt(var + EPS) * lng_ref[...] + lnb_ref[...]
    y = jnp.dot(_silu(ln).astype(BF16), w2_ref[...], preferred_element_type=F32) + b2_ref[...]
    out_ref[0] = _residual_ffn(x_ref[0], y, ada_ref, g2_ref, w_in_ref, w_out_ref)


def _conv_tail(u, x, ada, w_dw, b_dw, ln_g, ln_b, w_pw2, b_pw2, gain2, w_in, w_out):
    b, s, d = x.shape
    tm = min(TOKEN_TILE, s)
    assert s % tm == 0 and tm % CONV_SUB_TILE == 0 and CONV_SUB_TILE % CONV_ROW_CHUNK == 0
    hb = tm // HALO_ROWS
    n_halo = s // HALO_ROWS
    return pl.pallas_call(
        _conv_tail_kernel,
        out_shape=jax.ShapeDtypeStruct((b, s, d), F32),
        grid=(b, s // tm),
        in_specs=[pl.BlockSpec((1, HALO_ROWS, d), lambda bi, i: (bi, jnp.maximum(i * hb - 1, 0), 0)),
                  pl.BlockSpec((1, tm, d), lambda bi, i: (bi, i, 0)),
                  pl.BlockSpec((1, HALO_ROWS, d),
                               lambda bi, i: (bi, jnp.minimum((i + 1) * hb, n_halo - 1), 0)),
                  pl.BlockSpec((1, tm, d), lambda bi, i: (bi, i, 0)),
                  pl.BlockSpec((1, 6, d), lambda bi, i: (bi, 0, 0)),
                  _resident(w_dw.shape), _resident((1, d)), _resident((1, d)), _resident((1, d)),
                  _resident(w_pw2.shape), _resident((1, d)), _resident((1, d)),
                  _resident(w_in.shape), _resident(w_out.shape)],
        out_specs=pl.BlockSpec((1, tm, d), lambda bi, i: (bi, i, 0)),
        scratch_shapes=[pltpu.VMEM((tm + 2 * HALO_ROWS, d), F32),
                        pltpu.VMEM((SUBLANES, CONV_SUB_TILE + CONV_SHIFT_SLACK, d), F32),
                        pltpu.VMEM((tm, d), F32)],
        compiler_params=_params(2),
        name="conv_dw_pw2_ffn",
    )(u, u, u, x, ada, w_dw, b_dw, ln_g, ln_b, w_pw2, b_pw2, gain2, w_in, w_out)


def _trunk(x, ada, biases, p):
    d = x.shape[-1]
    row = lambda v: v.reshape(1, -1)
    for i in range(DEPTH):
        j = i // N_MIXERS
        gain1, gain2 = row(p["norm1_g"][i]), row(p["norm2_g"][i])
        w_in, w_out = p["ffn_w_in"][i], p["ffn_w_out"][i]
        if i % N_MIXERS == 0:
            w4 = p["attn_w_qkv"][j].reshape(d, 3, N_GROUPS, HEADS_PER_GROUP * HEAD_DIM)
            qkvs = [_qkv_proj(x, ada[i], gain1, w4[:, :, g, :].reshape(d, -1),
                              row(p["attn_q_gain"][j]), row(p["attn_k_gain"][j]), dil)
                    for g, (_, dil) in enumerate(DILATED_GROUPS)]
            o = _attention(qkvs, biases)
            x = _attn_tail(o, x, ada[i], p["attn_w_o"][j], gain2, w_in, w_out)
        else:
            u = _conv_head(x, ada[i], gain1, p["conv_w_pw1"][j], row(p["conv_b_pw1"][j]))
            x = _conv_tail(u, x, ada[i], p["conv_w_dw"][j], row(p["conv_b_dw"][j]),
                           row(p["conv_ln_g"][j]), row(p["conv_ln_b"][j]), p["conv_w_pw2"][j],
                           row(p["conv_b_pw2"][j]), gain2, w_in, w_out)
    return x


def kernel(x_prompt, x_sample, c_prompt, c_sample, rel_bias_table, norm1_g, norm2_g, ada_w, ada_b, attn_w_qkv, attn_q_gain, attn_k_gain, attn_w_o, conv_w_pw1, conv_b_pw1, conv_w_dw, conv_b_dw, conv_ln_g, conv_ln_b, conv_w_pw2, conv_b_pw2, ffn_w_in, ffn_w_out):
    p = dict(norm1_g=norm1_g, norm2_g=norm2_g, attn_q_gain=attn_q_gain, attn_k_gain=attn_k_gain,
             conv_b_pw1=conv_b_pw1, conv_w_dw=conv_w_dw, conv_b_dw=conv_b_dw, conv_ln_g=conv_ln_g,
             conv_ln_b=conv_ln_b, conv_b_pw2=conv_b_pw2,
             attn_w_qkv=attn_w_qkv.astype(BF16), attn_w_o=attn_w_o.astype(BF16),
             conv_w_pw1=conv_w_pw1.astype(BF16), conv_w_pw2=conv_w_pw2.astype(BF16),
             ffn_w_in=ffn_w_in.astype(BF16), ffn_w_out=ffn_w_out.astype(BF16))
    nb_p = c_prompt.shape[0]
    d = x_prompt.shape[-1]
    c_all = jnp.concatenate([c_prompt, c_sample], axis=0)
    ada = _ada_all(c_all, ada_w, ada_b).reshape(DEPTH, c_all.shape[0], 6, d)

    bias_cache = {}

    def biases_for(seq_len):
        out = []
        for g, (_, dil) in enumerate(DILATED_GROUPS):
            key = (g,) + _attn_geometry(seq_len, dil)
            if key not in bias_cache:
                bias_cache[key] = _expand_bias(rel_bias_table, _bias_bucket_index(seq_len, dil), g)
            out.append(bias_cache[key])
        return out

    y_prompt = _trunk(x_prompt, ada[:, :nb_p], biases_for(x_prompt.shape[1]), p)
    y_sample = _trunk(x_sample, ada[:, nb_p:], biases_for(x_sample.shape[1]), p)
    return (y_prompt, y_sample)
```

```python
import functools
import math

import jax
import jax.numpy as jnp
from jax import lax
from jax.experimental import pallas as pl
from jax.experimental.pallas import tpu as pltpu

F32 = jnp.float32
BF16 = jnp.bfloat16

D_MODEL = 1024
DEPTH = 4
N_MIXERS = 2
DILATED_GROUPS = ((128, 1), (512, 4), (2048, 16))
N_GROUPS = len(DILATED_GROUPS)
HEADS_PER_GROUP = 8
HEAD_DIM = 128
REL_BUCKETS = 32
REL_MAX_DIST = 1024
CONV_WIDTH = 31
CONV_PAD = (CONV_WIDTH - 1) // 2
D_FF = 2816
NEG_INF = -1e30
EPS = 1e-6

HALF = DILATED_GROUPS[0][0] // (2 * DILATED_GROUPS[0][1])
assert all(w // (2 * d) == HALF for w, d in DILATED_GROUPS)

TOKEN_TILE = 512
QUERY_BLOCK = 128
ATTN_BLOCKS_PER_STEP = 16
HALO_ROWS = 16
CONV_ROW_CHUNK = 32
SUBLANES = 8
BF16_SUBLANES = 16
LANES = 128
FF_CHUNKS = ((0, 768), (768, 768), (1536, 768), (2304, 512))
VMEM_LIMIT_BYTES = 60000 * 1024


def _params(n_axes):
    return pltpu.CompilerParams(dimension_semantics=("parallel",) * n_axes,
                                vmem_limit_bytes=VMEM_LIMIT_BYTES)


def _resident(shape):
    nd = len(shape)
    return pl.BlockSpec(shape, lambda *_: (0,) * nd, pipeline_mode=pl.Buffered(1))


def _silu(x):
    return x * jax.nn.sigmoid(x)


def _mod_rmsnorm(x, gain, shift, scale):
    ms = jnp.mean(x * x, axis=-1, keepdims=True)
    y = x * lax.rsqrt(ms + EPS) * gain
    return y * (1.0 + scale) + shift


def _ada_kernel(c_ref, w_ref, b_ref, o_ref):
    ca = _silu(c_ref[...]).astype(BF16)
    o_ref[0] = jnp.dot(ca, w_ref[0].astype(BF16), preferred_element_type=F32) + b_ref[0]


def _ada_all(c, ada_w, ada_b):
    nb, d = c.shape
    depth, _, width = ada_w.shape
    tn = 1536
    assert width % tn == 0
    return pl.pallas_call(
        _ada_kernel,
        out_shape=jax.ShapeDtypeStruct((depth, nb, width), F32),
        grid=(depth, width // tn),
        in_specs=[pl.BlockSpec((nb, d), lambda i, j: (0, 0)),
                  pl.BlockSpec((1, d, tn), lambda i, j: (i, 0, j)),
                  pl.BlockSpec((1, 1, tn), lambda i, j: (i, 0, j))],
        out_specs=pl.BlockSpec((1, nb, tn), lambda i, j: (i, 0, j)),
        compiler_params=_params(2),
        name="ada_proj",
    )(c, ada_w, ada_b.reshape(depth, 1, width))


def _t5_bucket(rel):
    half = REL_BUCKETS // 2
    max_exact = half // 2
    ret = jnp.where(rel > 0, half, 0)
    n = jnp.abs(rel)
    nf = jnp.maximum(n, 1).astype(F32)
    large = max_exact + (jnp.log(nf / max_exact) / math.log(REL_MAX_DIST / max_exact)
                         * (half - max_exact)).astype(jnp.int32)
    large = jnp.minimum(large, half - 1)
    return ret + jnp.where(n < max_exact, n, large)


def _attn_geometry(seq_len, dil):
    sub_len = seq_len // dil
    qb = min(QUERY_BLOCK, sub_len)
    kw = min(qb + 2 * HALF, sub_len)
    nblk = sub_len // qb
    assert sub_len * dil == seq_len and nblk * qb == sub_len and qb % HALF == 0
    return sub_len, qb, kw, nblk


def _key_offset(n, qb, kw, sub_len):
    return min(max(n * qb - HALF, 0), sub_len - kw)


def _bias_bucket_index(seq_len, dil):
    sub_len, qb, kw, nblk = _attn_geometry(seq_len, dil)
    slots = []
    for n in (0, min(1, nblk - 1), nblk - 1):
        delta = n * qb - _key_offset(n, qb, kw, sub_len)
        rel = jnp.arange(kw)[None, :] - jnp.arange(qb)[:, None] - delta
        slots.append(jnp.where(jnp.abs(rel) <= HALF, _t5_bucket(rel * dil), -1))
    return jnp.stack(slots).astype(jnp.int32)


def _bias_kernel(tab_ref, idx_ref, o_ref, *, head0):
    h = pl.program_id(0)
    idx = idx_ref[...]
    acc = jnp.full(idx.shape, NEG_INF, F32)
    for b in range(REL_BUCKETS):
        acc = jnp.where(idx == b, tab_ref[b, head0 + h], acc)
    o_ref[0] = acc


def _expand_bias(rel_table, idx, group):
    _, qb, kw = idx.shape
    return pl.pallas_call(
        functools.partial(_bias_kernel, head0=group * HEADS_PER_GROUP),
        out_shape=jax.ShapeDtypeStruct((HEADS_PER_GROUP, 3, qb, kw), F32),
        grid=(HEADS_PER_GROUP,),
        in_specs=[pl.BlockSpec(memory_space=pltpu.SMEM),
                  pl.BlockSpec((3, qb, kw), lambda h: (0, 0, 0))],
        out_specs=pl.BlockSpec((1, 3, qb, kw), lambda h: (h, 0, 0, 0)),
        compiler_params=_params(1),
        name="rel_bias_expand",
    )(rel_table, idx)


def _qkv_kernel(x_ref, ada_ref, g_ref, w_ref, qg_ref, kg_ref, o_ref, *scratch, dil):
    tm, d = x_ref.shape[1], x_ref.shape[2]
    sub = tm // dil
    h = _mod_rmsnorm(x_ref[0], g_ref[...], ada_ref[0, 0:1, :], ada_ref[0, 1:2, :])
    if dil == 1:
        h = h.astype(BF16)
    else:
        slab_ref, perm_ref = scratch
        for c in range(d // LANES):
            slab_ref[c] = h[:, c * LANES:(c + 1) * LANES]
        for c in range(d // LANES):
            for r in range(dil):
                perm_ref[r * sub:(r + 1) * sub, c * LANES:(c + 1) * LANES] = (
                    slab_ref[c, pl.ds(r, sub, stride=dil), :].astype(BF16))
        h = perm_ref[...]
    width = HEADS_PER_GROUP * HEAD_DIM
    for t in range(3):
        acc = jnp.dot(h, w_ref[:, t * width:(t + 1) * width], preferred_element_type=F32)
        for hd in range(HEADS_PER_GROUP):
            c = acc[:, hd * HEAD_DIM:(hd + 1) * HEAD_DIM]
            if t < 2:
                ms = jnp.mean(c * c, axis=-1, keepdims=True)
                c = c * lax.rsqrt(ms + EPS) * (qg_ref[...] if t == 0 else kg_ref[...])
            if t == 0:
                c = c * (HEAD_DIM ** -0.5)
            o_ref[0, t, hd] = c.astype(BF16).reshape(dil, sub, HEAD_DIM)


def _qkv_proj(x, ada, gain, w, q_gain, k_gain, dil):
    b, s, d = x.shape
    sub_len = s // dil
    tm = min(TOKEN_TILE, s)
    assert s % tm == 0 and tm % (dil * BF16_SUBLANES) == 0
    width = 3 * HEADS_PER_GROUP * HEAD_DIM
    scratch = [] if dil == 1 else [pltpu.VMEM((d // LANES, tm, LANES), F32),
                                   pltpu.VMEM((tm, d), BF16)]
    return pl.pallas_call(
        functools.partial(_qkv_kernel, dil=dil),
        out_shape=jax.ShapeDtypeStruct((b, 3, HEADS_PER_GROUP, dil, sub_len, HEAD_DIM), BF16),
        grid=(b, s // tm),
        in_specs=[pl.BlockSpec((1, tm, d), lambda bi, i: (bi, i, 0)),
                  pl.BlockSpec((1, 6, d), lambda bi, i: (bi, 0, 0)),
                  _resident((1, d)),
                  _resident((d, width)),
                  _resident((1, HEAD_DIM)),
                  _resident((1, HEAD_DIM))],
        out_specs=pl.BlockSpec((1, 3, HEADS_PER_GROUP, dil, tm // dil, HEAD_DIM),
                               lambda bi, i: (bi, 0, 0, 0, i, 0)),
        scratch_shapes=scratch,
        compiler_params=_params(2),
        name=f"qkv_proj_d{dil}",
    )(x, ada, gain, w, q_gain, k_gain)


def _attn_kernel(qkv0_ref, qkv1_ref, qkv2_ref, b0_ref, b1_ref, b2_ref, o_ref, og_ref, mg_ref,
                 lg_ref, *, seq_len):
    for g,(qkv_ref, bias_ref) in enumerate(((qkv0_ref, b0_ref), (qkv1_ref, b1_ref),
                                              (qkv2_ref, b2_ref))):
        dil = DILATED_GROUPS[g][1]
        sub_len, qb, kw, nblk = _attn_geometry(seq_len, dil)

        def block(r, n, g=g, dil=dil, sub_len=sub_len, qb=qb, kw=kw, nblk=nblk,
                  qkv_ref=qkv_ref, bias_ref=bias_ref):
            q0 = pl.multiple_of(r * sub_len + n * qb, qb)
            koff = jnp.clip(n * qb - HALF, 0, sub_len - kw)
            k0 = pl.multiple_of(r * sub_len + koff, HALF)
            q = qkv_ref[0, 0, 0, pl.ds(q0, qb), :]
            k = qkv_ref[0, 1, 0, pl.ds(k0, kw), :]
            v = qkv_ref[0, 2, 0, pl.ds(k0, kw), :]
            slot = jnp.where(n == 0, 0, jnp.where(n == nblk - 1, 2, 1))
            s = lax.dot_general(q, k, (((1,), (1,)), ((), ())), preferred_element_type=F32)
            s = s + bias_ref[0, slot]
            m = jnp.max(s, axis=-1, keepdims=True)
            p = jnp.exp(s - m).astype(BF16)
            v_ones = jnp.concatenate([v, jnp.ones((kw, HEAD_DIM), BF16)], axis=1)
            o_l = jnp.dot(p, v_ones, preferred_element_type=F32)
            start = r + n * (qb * dil)
            rows = pl.ds(start, qb) if dil == 1 else pl.ds(start, qb, stride=dil)
            og_ref[g, rows, :] = o_l[:, :HEAD_DIM]
            mg_ref[g, rows, :] = jnp.broadcast_to(m, (qb, HEAD_DIM))
            lg_ref[g, rows, :] = o_l[:, HEAD_DIM:]

        total = dil * nblk
        par = math.gcd(total, ATTN_BLOCKS_PER_STEP)

        def step(it, carry, block=block, nblk=nblk, par=par):
            for u in range(par):
                idx = it * par + u
                block(lax.div(idx, nblk), lax.rem(idx, nblk))
            return carry

        lax.fori_loop(0, total // par, step, 0)

    rc = 256

    def merge(c, carry):
        rows = pl.ds(pl.multiple_of(c * rc, rc), rc)
        ms = [mg_ref[g, rows, :] for g in range(N_GROUPS)]
        mx = jnp.maximum(jnp.maximum(ms[0], ms[1]), ms[2])
        num = den = None
        for g in range(N_GROUPS):
            w = jnp.exp(ms[g] - mx)
            num = w * og_ref[g, rows, :] if num is None else num + w * og_ref[g, rows, :]
            den = w * lg_ref[g, rows, :] if den is None else den + w * lg_ref[g, rows, :]
        o_ref[0, 0, rows, :] = (num / den).astype(BF16)
        return carry

    lax.fori_loop(0, seq_len // rc, merge, 0)


def _attention(qkvs, biases):
    b = qkvs[0].shape[0]
    s = qkvs[0].shape[3] * qkvs[0].shape[4]
    qkvs = [t.reshape(b, 3, HEADS_PER_GROUP, s, HEAD_DIM) for t in qkvs]
    qkv_spec = pl.BlockSpec((1, 3, 1, s, HEAD_DIM), lambda bi, h: (bi, 0, h, 0, 0))
    bias_specs = [pl.BlockSpec((1,) + t.shape[1:], lambda bi, h: (h, 0, 0, 0)) for t in biases]
    return pl.pallas_call(
        functools.partial(_attn_kernel, seq_len=s),
        out_shape=jax.ShapeDtypeStruct((b, HEADS_PER_GROUP, s, HEAD_DIM), BF16),
        grid=(b, HEADS_PER_GROUP),
        in_specs=[qkv_spec] * N_GROUPS + bias_specs,
        out_specs=pl.BlockSpec((1, 1, s, HEAD_DIM), lambda bi, h: (bi, h, 0, 0)),
        scratch_shapes=[pltpu.VMEM((N_GROUPS, s, HEAD_DIM), F32)] * 3,
        compiler_params=_params(2),
        name="dilated_attention",
    )(*qkvs, *biases)


def _residual_ffn(x, y, ada_ref, g2_ref, w_in_ref, w_out_ref):
    x1 = x + ada_ref[0, 2:3, :] * y
    h = _mod_rmsnorm(x1, g2_ref[...], ada_ref[0, 3:4, :], ada_ref[0, 4:5, :]).astype(BF16)
    acc = None
    for c0, cw in FF_CHUNKS:
        gate = jnp.dot(h, w_in_ref[:, c0:c0 + cw], preferred_element_type=F32)
        up = jnp.dot(h, w_in_ref[:, D_FF + c0:D_FF + c0 + cw], preferred_element_type=F32)
        a = (_silu(gate) * up).astype(BF16)
        part = jnp.dot(a, w_out_ref[c0:c0 + cw, :], preferred_element_type=F32)
        acc = part if acc is None else acc + part
    return x1 + ada_ref[0, 5:6, :] * acc


def _attn_tail_kernel(o_ref, x_ref, ada_ref, wo_ref, g2_ref, w_in_ref, w_out_ref, out_ref):
    o = jnp.concatenate([o_ref[0, h] for h in range(HEADS_PER_GROUP)], axis=-1)
    y = jnp.dot(o, wo_ref[...], preferred_element_type=F32)
    out_ref[0] = _residual_ffn(x_ref[0], y, ada_ref, g2_ref, w_in_ref, w_out_ref)


def _attn_tail(o, x, ada, w_o, gain2, w_in, w_out):
    b, s, d = x.shape
    tm = min(TOKEN_TILE, s)
    return pl.pallas_call(
        _attn_tail_kernel,
        out_shape=jax.ShapeDtypeStruct((b, s, d), F32),
        grid=(b, s // tm),
        in_specs=[pl.BlockSpec((1, HEADS_PER_GROUP, tm, HEAD_DIM), lambda bi, i: (bi, 0, i, 0)),
                  pl.BlockSpec((1, tm, d), lambda bi, i: (bi, i, 0)),
                  pl.BlockSpec((1, 6, d), lambda bi, i: (bi, 0, 0)),
                  _resident(w_o.shape), _resident((1, d)),
                  _resident(w_in.shape), _resident(w_out.shape)],
        out_specs=pl.BlockSpec((1, tm, d), lambda bi, i: (bi, i, 0)),
        compiler_params=_params(2),
        name="attn_out_ffn",
    )(o, x, ada, w_o, gain2, w_in, w_out)


def _conv_head_kernel(x_ref, ada_ref, g_ref, w_ref, b_ref, u_ref):
    d = x_ref.shape[-1]
    h = _mod_rmsnorm(x_ref[0], g_ref[...], ada_ref[0, 0:1, :], ada_ref[0, 1:2, :]).astype(BF16)
    a = jnp.dot(h, w_ref[:, :d], preferred_element_type=F32) + b_ref[:, :d]
    gt = jnp.dot(h, w_ref[:, d:], preferred_element_type=F32) + b_ref[:, d:]
    u_ref[0] = a * jax.nn.sigmoid(gt)


def _conv_head(x, ada, gain, w_pw1, b_pw1):
    b, s, d = x.shape
    tm = min(TOKEN_TILE, s)
    return pl.pallas_call(
        _conv_head_kernel,
        out_shape=jax.ShapeDtypeStruct((b, s, d), F32),
        grid=(b, s // tm),
        in_specs=[pl.BlockSpec((1, tm, d), lambda bi, i: (bi, i, 0)),
                  pl.BlockSpec((1, 6, d), lambda bi, i: (bi, 0, 0)),
                  _resident((1, d)), _resident(w_pw1.shape), _resident((1, 2 * d))],
        out_specs=pl.BlockSpec((1, tm, d), lambda bi, i: (bi, i, 0)),
        compiler_params=_params(2),
        name="conv_pw1_glu",
    )(x, ada, gain, w_pw1, b_pw1)


def _conv_tail_kernel(prev_ref, u_ref, next_ref, x_ref, ada_ref, wdw_ref, bdw_ref, lng_ref,
                      lnb_ref, w2_ref, b2_ref, g2_ref, w_in_ref, w_out_ref, out_ref,
                      win_ref, cv_ref):
    tm, d = u_ref.shape[1], u_ref.shape[2]
    i = pl.program_id(1)
    last = pl.num_programs(1) - 1
    prev = jnp.where(i > 0, prev_ref[0], 0.0)
    nxt = jnp.where(i < last, next_ref[0], 0.0)
    for c in range(d // LANES):
        lanes = slice(c * LANES, (c + 1) * LANES)
        win_ref[c, 0:HALO_ROWS, :] = prev[:, lanes]
        win_ref[c, HALO_ROWS:HALO_ROWS + tm, :] = u_ref[0, :, lanes]
        win_ref[c, HALO_ROWS + tm:, :] = nxt[:, lanes]

    for c in range(d // LANES):
        lanes = slice(c * LANES, (c + 1) * LANES)

        def conv_rows(j, carry, c=c, lanes=lanes):
            r0 = j * (2 * CONV_ROW_CHUNK)
            for parity in range(2):
                acc = None
                for k in range(CONV_WIDTH):
                    start = r0 + (parity + HALO_ROWS - CONV_PAD + k)
                    tap = win_ref[c, pl.ds(start, CONV_ROW_CHUNK, stride=2), :] * wdw_ref[k:k + 1, lanes]
                    acc = tap if acc is None else acc + tap
                cv_ref[c, pl.ds(r0 + parity, CONV_ROW_CHUNK, stride=2), :] = acc
            return carry

        lax.fori_loop(0, tm // (2 * CONV_ROW_CHUNK), conv_rows, 0)

    cv = jnp.concatenate([cv_ref[c] for c in range(d // LANES)], axis=1) + bdw_ref[...]
    mu = jnp.mean(cv, axis=-1, keepdims=True)
    xc = cv - mu
    var = jnp.mean(xc * xc, axis=-1, keepdims=True)
    ln = xc * lax.rsqrt(var + EPS) * lng_ref[...] + lnb_ref[...]
    y = jnp.dot(_silu(ln).astype(BF16), w2_ref[...], preferred_element_type=F32) + b2_ref[...]
    out_ref[0] = _residual_ffn(x_ref[0], y, ada_ref, g2_ref, w_in_ref, w_out_ref)


def _conv_tail(u, x, ada, w_dw, b_dw, ln_g, ln_b, w_pw2, b_pw2, gain2, w_in, w_out):
    b, s, d = x.shape
    tm = min(TOKEN_TILE, s)
    assert s % tm == 0 and tm % (2 * CONV_ROW_CHUNK) == 0 and HALO_ROWS >= CONV_PAD + 1
    hb = tm // HALO_ROWS
    n_halo = s // HALO_ROWS
    return pl.pallas_call(
        _conv_tail_kernel,
        out_shape=jax.ShapeDtypeStruct((b, s, d), F32),
        grid=(b, s // tm),
        in_specs=[pl.BlockSpec((1, HALO_ROWS, d), lambda bi, i: (bi, jnp.maximum(i * hb - 1, 0), 0)),
                  pl.BlockSpec((1, tm, d), lambda bi, i: (bi, i, 0)),
                  pl.BlockSpec((1, HALO_ROWS, d),
                               lambda bi, i: (bi, jnp.minimum((i + 1) * hb, n_halo - 1), 0)),
                  pl.BlockSpec((1, tm, d), lambda bi, i: (bi, i, 0)),
                  pl.BlockSpec((1, 6, d), lambda bi, i: (bi, 0, 0)),
                  _resident(w_dw.shape), _resident((1, d)), _resident((1, d)), _resident((1, d)),
                  _resident(w_pw2.shape), _resident((1, d)), _resident((1, d)),
                  _resident(w_in.shape), _resident(w_out.shape)],
        out_specs=pl.BlockSpec((1, tm, d), lambda bi, i: (bi, i, 0)),
        scratch_shapes=[pltpu.VMEM((d // LANES, tm + 2 * HALO_ROWS, LANES), F32),
                        pltpu.VMEM((d // LANES, tm, LANES), F32)],
        compiler_params=_params(2),
        name="conv_dw_pw2_ffn",
    )(u, u, u, x, ada, w_dw, b_dw, ln_g, ln_b, w_pw2, b_pw2, gain2, w_in, w_out)


def _trunk(x, ada, biases, p):
    d = x.shape[-1]
    row = lambda v: v.reshape(1, -1)
    for i in range(DEPTH):
        j = i // N_MIXERS
        gain1, gain2 = row(p["norm1_g"][i]), row(p["norm2_g"][i])
        w_in, w_out = p["ffn_w_in"][i], p["ffn_w_out"][i]
        if i % N_MIXERS == 0:
            w4 = p["attn_w_qkv"][j].reshape(d, 3, N_GROUPS, HEADS_PER_GROUP * HEAD_DIM)
            qkvs = [_qkv_proj(x, ada[i], gain1, w4[:, :, g, :].reshape(d, -1),
                              row(p["attn_q_gain"][j]), row(p["attn_k_gain"][j]), dil)
                    for g, (_, dil) in enumerate(DILATED_GROUPS)]
            o = _attention(qkvs, biases)
            x = _attn_tail(o, x, ada[i], p["attn_w_o"][j], gain2, w_in, w_out)
        else:
            u = _conv_head(x, ada[i], gain1, p["conv_w_pw1"][j], row(p["conv_b_pw1"][j]))
            x = _conv_tail(u, x, ada[i], p["conv_w_dw"][j], row(p["conv_b_dw"][j]),
                           row(p["conv_ln_g"][j]), row(p["conv_ln_b"][j]), p["conv_w_pw2"][j],
                           row(p["conv_b_pw2"][j]), gain2, w_in, w_out)
    return x


def kernel(x_prompt, x_sample, c_prompt, c_sample, rel_bias_table, norm1_g, norm2_g, ada_w, ada_b, attn_w_qkv, attn_q_gain, attn_k_gain, attn_w_o, conv_w_pw1, conv_b_pw1, conv_w_dw, conv_b_dw, conv_ln_g, conv_ln_b, conv_w_pw2, conv_b_pw2, ffn_w_in, ffn_w_out):
    p = dict(norm1_g=norm1_g, norm2_g=norm2_g, attn_q_gain=attn_q_gain, attn_k_gain=attn_k_gain,
             conv_b_pw1=conv_b_pw1, conv_w_dw=conv_w_dw, conv_b_dw=conv_b_dw, conv_ln_g=conv_ln_g,
             conv_ln_b=conv_ln_b, conv_b_pw2=conv_b_pw2,
             attn_w_qkv=attn_w_qkv.astype(BF16), attn_w_o=attn_w_o.astype(BF16),
             conv_w_pw1=conv_w_pw1.astype(BF16), conv_w_pw2=conv_w_pw2.astype(BF16),
             ffn_w_in=ffn_w_in.astype(BF16), ffn_w_out=ffn_w_out.astype(BF16))
    nb_p = c_prompt.shape[0]
    d = x_prompt.shape[-1]
    c_all = jnp.concatenate([c_prompt, c_sample], axis=0)
    ada = _ada_all(c_all, ada_w, ada_b).reshape(DEPTH, c_all.shape[0], 6, d)

    bias_cache = {}

    def biases_for(seq_len):
        out = []
        for g, (_, dil) in enumerate(DILATED_GROUPS):
            key = (g,) + _attn_geometry(seq_len, dil)
            if key not in bias_cache:
                bias_cache[key] = _expand_bias(rel_bias_table, _bias_bucket_index(seq_len, dil), g)
            out.append(bias_cache[key])
        return out

    y_prompt = _trunk(x_prompt, ada[:, :nb_p], biases_for(x_prompt.shape[1]), p)
    y_sample = _trunk(x_sample, ada[:, nb_p:], biases_for(x_sample.shape[1]), p)
    return (y_prompt, y_sample)
```

```python
import functools
import math

import jax
import jax.numpy as jnp
from jax import lax
from jax.experimental import pallas as pl
from jax.experimental.pallas import tpu as pltpu

F32 = jnp.float32
BF16 = jnp.bfloat16

D_MODEL = 1024
DEPTH = 4
N_MIXERS = 2
DILATED_GROUPS = ((128, 1), (512, 4), (2048, 16))
N_GROUPS = len(DILATED_GROUPS)
HEADS_PER_GROUP = 8
HEAD_DIM = 128
REL_BUCKETS = 32
REL_MAX_DIST = 1024
CONV_WIDTH = 31
CONV_PAD = (CONV_WIDTH - 1) // 2
D_FF = 2816
NEG_INF = -1e30
EPS = 1e-6

HALF = DILATED_GROUPS[0][0] // (2 * DILATED_GROUPS[0][1])
assert all(w // (2 * d) == HALF for w, d in DILATED_GROUPS)

TOKEN_TILE = 512
QUERY_BLOCK = 128
ATTN_BLOCKS_PER_STEP = 16
HALO_ROWS = 16
CONV_ROW_CHUNK = 32
SUBLANES = 8
BF16_SUBLANES = 16
LANES = 128
FF_LOOP_CHUNK = 256
FF_CHUNKS = ((0, 768), (768, 768), (1536, 768), (2304, 512))
VMEM_LIMIT_BYTES = 60000 * 1024


def _params(n_axes):
    return pltpu.CompilerParams(dimension_semantics=("parallel",) * n_axes,
                                vmem_limit_bytes=VMEM_LIMIT_BYTES)


def _resident(shape):
    nd = len(shape)
    return pl.BlockSpec(shape, lambda *_: (0,) * nd, pipeline_mode=pl.Buffered(1))


def _silu(x):
    return x * jax.nn.sigmoid(x)


def _mod_rmsnorm(x, gain, shift, scale):
    ms = jnp.mean(x * x, axis=-1, keepdims=True)
    y = x * lax.rsqrt(ms + EPS) * gain
    return y * (1.0 + scale) + shift


def _ada_kernel(c_ref, w_ref, b_ref, o_ref):
    ca = _silu(c_ref[...]).astype(BF16)
    o_ref[0] = jnp.dot(ca, w_ref[0].astype(BF16), preferred_element_type=F32) + b_ref[0]


def _ada_all(c, ada_w, ada_b):
    nb, d = c.shape
    depth, _, width = ada_w.shape
    tn = 1536
    assert width % tn == 0
    return pl.pallas_call(
        _ada_kernel,
        out_shape=jax.ShapeDtypeStruct((depth, nb, width), F32),
        grid=(depth, width // tn),
        in_specs=[pl.BlockSpec((nb, d), lambda i, j: (0, 0)),
                  pl.BlockSpec((1, d, tn), lambda i, j: (i, 0, j)),
                  pl.BlockSpec((1, 1, tn), lambda i, j: (i, 0, j))],
        out_specs=pl.BlockSpec((1, nb, tn), lambda i, j: (i, 0, j)),
        compiler_params=_params(2),
        name="ada_proj",
    )(c, ada_w, ada_b.reshape(depth, 1, width))


def _t5_bucket(rel):
    half = REL_BUCKETS // 2
    max_exact = half // 2
    ret = jnp.where(rel > 0, half, 0)
    n = jnp.abs(rel)
    nf = jnp.maximum(n, 1).astype(F32)
    large = max_exact + (jnp.log(nf / max_exact) / math.log(REL_MAX_DIST / max_exact)
                         * (half - max_exact)).astype(jnp.int32)
    large = jnp.minimum(large, half - 1)
    return ret + jnp.where(n < max_exact, n, large)


def _attn_geometry(seq_len, dil):
    sub_len = seq_len // dil
    qb = min(QUERY_BLOCK, sub_len)
    kw = min(qb + 2 * HALF, sub_len)
    nblk = sub_len // qb
    assert sub_len * dil == seq_len and nblk * qb == sub_len and qb % HALF == 0
    return sub_len, qb, kw, nblk


def _key_offset(n, qb, kw, sub_len):
    return min(max(n * qb - HALF, 0), sub_len - kw)


def _bias_bucket_index(seq_len, dil):
    sub_len, qb, kw, nblk = _attn_geometry(seq_len, dil)
    slots = []
    for n in (0, min(1, nblk - 1), nblk - 1):
        delta = n * qb - _key_offset(n, qb, kw, sub_len)
        rel = jnp.arange(kw)[None, :] - jnp.arange(qb)[:, None] - delta
        slots.append(jnp.where(jnp.abs(rel) <= HALF, _t5_bucket(rel * dil), -1))
    return jnp.stack(slots).astype(jnp.int32)


def _bias_kernel(tab_ref, idx_ref, o_ref, *, head0):
    h = pl.program_id(0)
    idx = idx_ref[...]
    acc = jnp.full(idx.shape, NEG_INF, F32)
    for b in range(REL_BUCKETS):
        acc = jnp.where(idx == b, tab_ref[b, head0 + h], acc)
    o_ref[0] = acc


def _expand_bias(rel_table, idx, group):
    _, qb, kw = idx.shape
    return pl.pallas_call(
        functools.partial(_bias_kernel, head0=group * HEADS_PER_GROUP),
        out_shape=jax.ShapeDtypeStruct((HEADS_PER_GROUP, 3, qb, kw), F32),
        grid=(HEADS_PER_GROUP,),
        in_specs=[pl.BlockSpec(memory_space=pltpu.SMEM),
                  pl.BlockSpec((3, qb, kw), lambda h: (0, 0, 0))],
        out_specs=pl.BlockSpec((1, 3, qb, kw), lambda h: (h, 0, 0, 0)),
        compiler_params=_params(1),
        name="rel_bias_expand",
    )(rel_table, idx)


def _qkv_kernel(x_ref, ada_ref, g_ref, w_ref, qg_ref, kg_ref, o_ref, *scratch, dil):
    tm, d = x_ref.shape[1], x_ref.shape[2]
    sub = tm // dil
    h = _mod_rmsnorm(x_ref[0], g_ref[...], ada_ref[0, 0:1, :], ada_ref[0, 1:2, :])
    if dil == 1:
        h = h.astype(BF16)
    else:
        slab_ref, perm_ref = scratch
        for c in range(d // LANES):
            slab_ref[c] = h[:, c * LANES:(c + 1) * LANES]
        for c in range(d // LANES):
            for r in range(dil):
                perm_ref[r * sub:(r + 1) * sub, c * LANES:(c + 1) * LANES] = (
                    slab_ref[c, pl.ds(r, sub, stride=dil), :].astype(BF16))
        h = perm_ref[...]
    width = HEADS_PER_GROUP * HEAD_DIM
    for t in range(3):
        acc = jnp.dot(h, w_ref[:, t * width:(t + 1) * width], preferred_element_type=F32)
        for hd in range(HEADS_PER_GROUP):
            c = acc[:, hd * HEAD_DIM:(hd + 1) * HEAD_DIM]
            if t < 2:
                ms = jnp.mean(c * c, axis=-1, keepdims=True)
                c = c * lax.rsqrt(ms + EPS) * (qg_ref[...] if t == 0 else kg_ref[...])
            if t == 0:
                c = c * (HEAD_DIM ** -0.5)
            o_ref[0, t, hd] = c.astype(BF16).reshape(dil, sub, HEAD_DIM)


def _qkv_proj(x, ada, gain, w, q_gain, k_gain, dil):
    b, s, d = x.shape
    sub_len = s // dil
    tm = min(TOKEN_TILE, s)
    assert s % tm == 0 and tm % (dil * BF16_SUBLANES) == 0
    width = 3 * HEADS_PER_GROUP * HEAD_DIM
    scratch = [] if dil == 1 else [pltpu.VMEM((d // LANES, tm, LANES), F32),
                                   pltpu.VMEM((tm, d), BF16)]
    return pl.pallas_call(
        functools.partial(_qkv_kernel, dil=dil),
        out_shape=jax.ShapeDtypeStruct((b, 3, HEADS_PER_GROUP, dil, sub_len, HEAD_DIM), BF16),
        grid=(b, s // tm),
        in_specs=[pl.BlockSpec((1, tm, d), lambda bi, i: (bi, i, 0)),
                  pl.BlockSpec((1, 6, d), lambda bi, i: (bi, 0, 0)),
                  _resident((1, d)),
                  _resident((d, width)),
                  _resident((1, HEAD_DIM)),
                  _resident((1, HEAD_DIM))],
        out_specs=pl.BlockSpec((1, 3, HEADS_PER_GROUP, dil, tm // dil, HEAD_DIM),
                               lambda bi, i: (bi, 0, 0, 0, i, 0)),
        scratch_shapes=scratch,
        compiler_params=_params(2),
        name=f"qkv_proj_d{dil}",
    )(x, ada, gain, w, q_gain, k_gain)


def _attn_kernel(qkv0_ref, qkv1_ref, qkv2_ref, b0_ref, b1_ref, b2_ref, o_ref, og_ref, mg_ref,
                 lg_ref, *, seq_len):
    for g,(qkv_ref, bias_ref) in enumerate(((qkv0_ref, b0_ref), (qkv1_ref, b1_ref),
                                              (qkv2_ref, b2_ref))):
        dil = DILATED_GROUPS[g][1]
        sub_len, qb, kw, nblk = _attn_geometry(seq_len, dil)

        def block(r, n, g=g, dil=dil, sub_len=sub_len, qb=qb, kw=kw, nblk=nblk,
                  qkv_ref=qkv_ref, bias_ref=bias_ref):
            q0 = pl.multiple_of(r * sub_len + n * qb, qb)
            koff = jnp.clip(n * qb - HALF, 0, sub_len - kw)
            k0 = pl.multiple_of(r * sub_len + koff, HALF)
            q = qkv_ref[0, 0, 0, pl.ds(q0, qb), :]
            k = qkv_ref[0, 1, 0, pl.ds(k0, kw), :]
            v = qkv_ref[0, 2, 0, pl.ds(k0, kw), :]
            slot = jnp.where(n == 0, 0, jnp.where(n == nblk - 1, 2, 1))
            s = lax.dot_general(q, k, (((1,), (1,)), ((), ())), preferred_element_type=F32)
            s = s + bias_ref[0, slot]
            m = jnp.max(s, axis=-1, keepdims=True)
            p = jnp.exp(s - m).astype(BF16)
            v_ones = jnp.concatenate([v, jnp.ones((kw, HEAD_DIM), BF16)], axis=1)
            o_l = jnp.dot(p, v_ones, preferred_element_type=F32)
            start = r + n * (qb * dil)
            rows = pl.ds(start, qb) if dil == 1 else pl.ds(start, qb, stride=dil)
            og_ref[g, rows, :] = o_l[:, :HEAD_DIM]
            mg_ref[g, rows, :] = jnp.broadcast_to(m, (qb, HEAD_DIM))
            lg_ref[g, rows, :] = o_l[:, HEAD_DIM:]

        total = dil * nblk
        par = math.gcd(total, ATTN_BLOCKS_PER_STEP)

        def step(it, carry, block=block, nblk=nblk, par=par):
            for u in range(par):
                idx = it * par + u
                block(lax.div(idx, nblk), lax.rem(idx, nblk))
            return carry

        lax.fori_loop(0, total // par, step, 0)

    rc = 256

    def merge(c, carry):
        rows = pl.ds(pl.multiple_of(c * rc, rc), rc)
        ms = [mg_ref[g, rows, :] for g in range(N_GROUPS)]
        mx = jnp.maximum(jnp.maximum(ms[0], ms[1]), ms[2])
        num = den = None
        for g in range(N_GROUPS):
            w = jnp.exp(ms[g] - mx)
            num = w * og_ref[g, rows, :] if num is None else num + w * og_ref[g, rows, :]
            den = w * lg_ref[g, rows, :] if den is None else den + w * lg_ref[g, rows, :]
        o_ref[0, 0, rows, :] = (num / den).astype(BF16)
        return carry

    lax.fori_loop(0, seq_len // rc, merge, 0)


def _attention(qkvs, biases):
    b = qkvs[0].shape[0]
    s = qkvs[0].shape[3] * qkvs[0].shape[4]
    qkvs = [t.reshape(b, 3, HEADS_PER_GROUP, s, HEAD_DIM) for t in qkvs]
    qkv_spec = pl.BlockSpec((1, 3, 1, s, HEAD_DIM), lambda bi, h: (bi, 0, h, 0, 0))
    bias_specs = [pl.BlockSpec((1,) + t.shape[1:], lambda bi, h: (h, 0, 0, 0)) for t in biases]
    return pl.pallas_call(
        functools.partial(_attn_kernel, seq_len=s),
        out_shape=jax.ShapeDtypeStruct((b, HEADS_PER_GROUP, s, HEAD_DIM), BF16),
        grid=(b, HEADS_PER_GROUP),
        in_specs=[qkv_spec] * N_GROUPS + bias_specs,
        out_specs=pl.BlockSpec((1, 1, s, HEAD_DIM), lambda bi, h: (bi, h, 0, 0)),
        scratch_shapes=[pltpu.VMEM((N_GROUPS, s, HEAD_DIM), F32)] * 3,
        compiler_params=_params(2),
        name="dilated_attention",
    )(*qkvs, *biases)


def _residual_ffn(x, y, ada_ref, g2_ref, w_in_ref, w_out_ref):
    x1 = x + ada_ref[0, 2:3, :] * y
    h = _mod_rmsnorm(x1, g2_ref[...], ada_ref[0, 3:4, :], ada_ref[0, 4:5, :]).astype(BF16)
    acc = None
    for c0, cw in FF_CHUNKS:
        gate = jnp.dot(h, w_in_ref[:, c0:c0 + cw], preferred_element_type=F32)
        up = jnp.dot(h, w_in_ref[:, D_FF + c0:D_FF + c0 + cw], preferred_element_type=F32)
        a = (_silu(gate) * up).astype(BF16)
        part = jnp.dot(a, w_out_ref[c0:c0 + cw, :], preferred_element_type=F32)
        acc = part if acc is None else acc + part
    return x1 + ada_ref[0, 5:6, :] * acc


def _attn_tail_kernel(o_ref, x_ref, ada_ref, wo_ref, g2_ref, w_in_ref, w_out_ref, out_ref):
    o = jnp.concatenate([o_ref[0, h] for h in range(HEADS_PER_GROUP)], axis=-1)
    y = jnp.dot(o, wo_ref[...], preferred_element_type=F32)
    out_ref[0] = _residual_ffn(x_ref[0], y, ada_ref, g2_ref, w_in_ref, w_out_ref)


def _attn_tail(o, x, ada, w_o, gain2, w_in, w_out):
    b, s, d = x.shape
    tm = min(TOKEN_TILE, s)
    return pl.pallas_call(
        _attn_tail_kernel,
        out_shape=jax.ShapeDtypeStruct((b, s, d), F32),
        grid=(b, s // tm),
        in_specs=[pl.BlockSpec((1, HEADS_PER_GROUP, tm, HEAD_DIM), lambda bi, i: (bi, 0, i, 0)),
                  pl.BlockSpec((1, tm, d), lambda bi, i: (bi, i, 0)),
                  pl.BlockSpec((1, 6, d), lambda bi, i: (bi, 0, 0)),
                  _resident(w_o.shape), _resident((1, d)),
                  _resident(w_in.shape), _resident(w_out.shape)],
        out_specs=pl.BlockSpec((1, tm, d), lambda bi, i: (bi, i, 0)),
        compiler_params=_params(2),
        name="attn_out_ffn",
    )(o, x, ada, w_o, gain2, w_in, w_out)


def _conv_head_kernel(x_ref, ada_ref, g_ref, w_ref, b_ref, u_ref):
    d = x_ref.shape[-1]
    h = _mod_rmsnorm(x_ref[0], g_ref[...], ada_ref[0, 0:1, :], ada_ref[0, 1:2, :]).astype(BF16)
    a = jnp.dot(h, w_ref[:, :d], preferred_element_type=F32) + b_ref[:, :d]
    gt = jnp.dot(h, w_ref[:, d:], preferred_element_type=F32) + b_ref[:, d:]
    u_ref[0] = a * jax.nn.sigmoid(gt)


def _conv_head(x, ada, gain, w_pw1, b_pw1):
    b, s, d = x.shape
    tm = min(TOKEN_TILE, s)
    return pl.pallas_call(
        _conv_head_kernel,
        out_shape=jax.ShapeDtypeStruct((b, s, d), F32),
        grid=(b, s // tm),
        in_specs=[pl.BlockSpec((1, tm, d), lambda bi, i: (bi, i, 0)),
                  pl.BlockSpec((1, 6, d), lambda bi, i: (bi, 0, 0)),
                  _resident((1, d)), _resident(w_pw1.shape), _resident((1, 2 * d))],
        out_specs=pl.BlockSpec((1, tm, d), lambda bi, i: (bi, i, 0)),
        compiler_params=_params(2),
        name="conv_pw1_glu",
    )(x, ada, gain, w_pw1, b_pw1)


def _fill_conv_window(win_ref, prev, cur_ref, nxt):
    tm, d = cur_ref.shape[1], cur_ref.shape[2]
    for c in range(d // LANES):
        lanes = slice(c * LANES, (c + 1) * LANES)
        win_ref[c, 0:HALO_ROWS, :] = prev[:, lanes]
        win_ref[c, HALO_ROWS:HALO_ROWS + tm, :] = cur_ref[0, :, lanes]
        win_ref[c, HALO_ROWS + tm:, :] = nxt[:, lanes]


def _conv_rows(win_ref, wdw_ref, cv_ref, c, r0):
    acc = [None, None]
    taps = {}
    for s in range(CONV_WIDTH + 1):
        rows = pl.ds(r0 + (HALO_ROWS - CONV_PAD + s), CONV_ROW_CHUNK, stride=2)
        x = win_ref[c, rows, :]
        for parity in range(2):
            k = s - parity
            if 0 <= k < CONV_WIDTH:
                if k not in taps:
                    taps[k] = wdw_ref[c, k:k + 1, :]
                tap = x * taps[k]
                acc[parity] = tap if acc[parity] is None else acc[parity] + tap
    for parity in range(2):
        cv_ref[c, pl.ds(r0 + parity, CONV_ROW_CHUNK, stride=2), :] = acc[parity]


def _conv_tail_kernel(u0_ref, next0_ref, prev_ref, u_ref, next_ref, x_ref, ada_ref, wdw_ref,
                      bdw_ref, lng_ref, lnb_ref, w2_ref, b2_ref, g2_ref, w_gu_ref, w_out_ref,
                      out_ref, win_ref, cv_ref, x1_ref, h_ref, acc_ref, a_ref, *, tiles_per_seq):
    tm, d = u_ref.shape[1], u_ref.shape[2]
    n_chunks = tm // (2 * CONV_ROW_CHUNK)
    t = pl.program_id(0)

    @pl.when(t == 0)
    def _():
        nxt0 = next0_ref[0] if tiles_per_seq > 1 else jnp.zeros((HALO_ROWS, d), F32)
        _fill_conv_window(win_ref, jnp.zeros((HALO_ROWS, d), F32), u0_ref, nxt0)
        for c in range(d // LANES):
            def rows(j, carry, c=c):
                _conv_rows(win_ref, wdw_ref, cv_ref, c, j * (2 * CONV_ROW_CHUNK))
                return carry
            lax.fori_loop(0, n_chunks, rows, 0)

    cv = jnp.concatenate([cv_ref[c] for c in range(d // LANES)], axis=1) + bdw_ref[...]
    mu = jnp.mean(cv, axis=-1, keepdims=True)
    xc = cv - mu
    var = jnp.mean(xc * xc, axis=-1, keepdims=True)
    ln = xc * lax.rsqrt(var + EPS) * lng_ref[...] + lnb_ref[...]
    act = _silu(ln).astype(BF16)

    i_next = lax.rem(t + 1, tiles_per_seq)
    prev = jnp.where(i_next > 0, prev_ref[0], 0.0)
    nxt = jnp.where(i_next < tiles_per_seq - 1, next_ref[0], 0.0)
    _fill_conv_window(win_ref, prev, u_ref, nxt)

    y = jnp.dot(act, w2_ref[...], preferred_element_type=F32) + b2_ref[...]
    x1 = x_ref[0] + ada_ref[0, 2:3, :] * y
    x1_ref[...] = x1
    h_ref[...] = _mod_rmsnorm(x1, g2_ref[...], ada_ref[0, 3:4, :], ada_ref[0, 4:5, :]).astype(BF16)
    acc_ref[...] = jnp.zeros_like(acc_ref)

    def gated(j):
        gu = jnp.dot(h_ref[...], w_gu_ref[j], preferred_element_type=F32)
        return (_silu(gu[:, :FF_LOOP_CHUNK]) * gu[:, FF_LOOP_CHUNK:]).astype(BF16)

    n_ff = w_gu_ref.shape[0]
    n_units = (d // LANES) * n_chunks
    units_per_iter = -(-n_units // (n_ff - 1))
    a_ref[...] = gated(0)

    def ffn_chunk(j, carry):
        a_cur = a_ref[...]
        a_ref[...] = gated(j + 1)
        acc_ref[...] += jnp.dot(a_cur, w_out_ref[j], preferred_element_type=F32)
        for i in range(units_per_iter):
            unit = jnp.minimum(j * units_per_iter + i, n_units - 1)
            _conv_rows(win_ref, wdw_ref, cv_ref, lax.div(unit, n_chunks),
                       lax.rem(unit, n_chunks) * (2 * CONV_ROW_CHUNK))
        return carry

    lax.fori_loop(0, n_ff - 1, ffn_chunk, 0)
    ffn = acc_ref[...] + jnp.dot(a_ref[...], w_out_ref[n_ff - 1], preferred_element_type=F32)
    out_ref[0] = x1_ref[...] + ada_ref[0, 5:6, :] * ffn


def _conv_tail(u, x, ada, w_dw, b_dw, ln_g, ln_b, w_pw2, b_pw2, gain2, w_in, w_out):
    b, s, d = x.shape
    tm = min(TOKEN_TILE, s)
    assert s % tm == 0 and tm % (2 * CONV_ROW_CHUNK) == 0 and HALO_ROWS >= CONV_PAD + 1
    nt = s // tm
    hb = tm // HALO_ROWS
    n_halo = s // HALO_ROWS
    last = b * nt - 1

    def conv_tile(t):
        tn = jnp.minimum(t + 1, last)
        return tn // nt, tn % nt

    def u_next(t):
        bi, i = conv_tile(t)
        return bi, i, 0

    def halo_before(t):
        bi, i = conv_tile(t)
        return bi, jnp.maximum(i * hb - 1, 0), 0

    def halo_after(t):
        bi, i = conv_tile(t)
        return bi, jnp.minimum((i + 1) * hb, n_halo - 1), 0

    tile = lambda t: (t // nt, t % nt, 0)
    n_ff = D_FF // FF_LOOP_CHUNK
    w_gu = w_in.reshape(d, 2, n_ff, FF_LOOP_CHUNK).transpose(2, 0, 1, 3).reshape(
        n_ff, d, 2 * FF_LOOP_CHUNK)
    w_out = w_out.reshape(n_ff, FF_LOOP_CHUNK, d)
    w_dw = w_dw.reshape(CONV_WIDTH, d // LANES, LANES).transpose(1, 0, 2)
    return pl.pallas_call(
        functools.partial(_conv_tail_kernel, tiles_per_seq=nt),
        out_shape=jax.ShapeDtypeStruct((b, s, d), F32),
        grid=(b * nt,),
        in_specs=[pl.BlockSpec((1, tm, d), lambda t: (0, 0, 0), pipeline_mode=pl.Buffered(1)),
                  pl.BlockSpec((1, HALO_ROWS, d), lambda t: (0, min(hb, n_halo - 1), 0),
                               pipeline_mode=pl.Buffered(1)),
                  pl.BlockSpec((1, HALO_ROWS, d), halo_before),
                  pl.BlockSpec((1, tm, d), u_next),
                  pl.BlockSpec((1, HALO_ROWS, d), halo_after),
                  pl.BlockSpec((1, tm, d), tile),
                  pl.BlockSpec((1, 6, d), lambda t: (t // nt, 0, 0)),
                  _resident(w_dw.shape), _resident((1, d)), _resident((1, d)), _resident((1, d)),
                  _resident(w_pw2.shape), _resident((1, d)), _resident((1, d)),
                  _resident(w_gu.shape), _resident(w_out.shape)],
        out_specs=pl.BlockSpec((1, tm, d), tile),
        scratch_shapes=[pltpu.VMEM((d // LANES, tm + 2 * HALO_ROWS, LANES), F32),
                        pltpu.VMEM((d // LANES, tm, LANES), F32),
                        pltpu.VMEM((tm, d), F32), pltpu.VMEM((tm, d), BF16),
                        pltpu.VMEM((tm, d), F32), pltpu.VMEM((tm, FF_LOOP_CHUNK), BF16)],
        compiler_params=pltpu.CompilerParams(dimension_semantics=("arbitrary",),
                                             vmem_limit_bytes=VMEM_LIMIT_BYTES),
        name="conv_dw_pw2_ffn",
    )(u, u, u, u, u, x, ada, w_dw, b_dw, ln_g, ln_b, w_pw2, b_pw2, gain2, w_gu, w_out)


def _trunk(x, ada, biases, p):
    d = x.shape[-1]
    row = lambda v: v.reshape(1, -1)
    for i in range(DEPTH):
        j = i // N_MIXERS
        gain1, gain2 = row(p["norm1_g"][i]), row(p["norm2_g"][i])
        w_in, w_out = p["ffn_w_in"][i], p["ffn_w_out"][i]
        if i % N_MIXERS == 0:
            w4 = p["attn_w_qkv"][j].reshape(d, 3, N_GROUPS, HEADS_PER_GROUP * HEAD_DIM)
            qkvs = [_qkv_proj(x, ada[i], gain1, w4[:, :, g, :].reshape(d, -1),
                              row(p["attn_q_gain"][j]), row(p["attn_k_gain"][j]), dil)
                    for g, (_, dil) in enumerate(DILATED_GROUPS)]
            o = _attention(qkvs, biases)
            x = _attn_tail(o, x, ada[i], p["attn_w_o"][j], gain2, w_in, w_out)
        else:
            u = _conv_head(x, ada[i], gain1, p["conv_w_pw1"][j], row(p["conv_b_pw1"][j]))
            x = _conv_tail(u, x, ada[i], p["conv_w_dw"][j], row(p["conv_b_dw"][j]),
                           row(p["conv_ln_g"][j]), row(p["conv_ln_b"][j]), p["conv_w_pw2"][j],
                           row(p["conv_b_pw2"][j]), gain2, w_in, w_out)
    return x


def kernel(x_prompt, x_sample, c_prompt, c_sample, rel_bias_table, norm1_g, norm2_g, ada_w, ada_b, attn_w_qkv, attn_q_gain, attn_k_gain, attn_w_o, conv_w_pw1, conv_b_pw1, conv_w_dw, conv_b_dw, conv_ln_g, conv_ln_b, conv_w_pw2, conv_b_pw2, ffn_w_in, ffn_w_out):
    p = dict(norm1_g=norm1_g, norm2_g=norm2_g, attn_q_gain=attn_q_gain, attn_k_gain=attn_k_gain,
             conv_b_pw1=conv_b_pw1, conv_w_dw=conv_w_dw, conv_b_dw=conv_b_dw, conv_ln_g=conv_ln_g,
             conv_ln_b=conv_ln_b, conv_b_pw2=conv_b_pw2,
             attn_w_qkv=attn_w_qkv.astype(BF16), attn_w_o=attn_w_o.astype(BF16),
             conv_w_pw1=conv_w_pw1.astype(BF16), conv_w_pw2=conv_w_pw2.astype(BF16),
             ffn_w_in=ffn_w_in.astype(BF16), ffn_w_out=ffn_w_out.astype(BF16))
    nb_p = c_prompt.shape[0]
    d = x_prompt.shape[-1]
    c_all = jnp.concatenate([c_prompt, c_sample], axis=0)
    ada = _ada_all(c_all, ada_w, ada_b).reshape(DEPTH, c_all.shape[0], 6, d)

    bias_cache = {}

    def biases_for(seq_len):
        out = []
        for g, (_, dil) in enumerate(DILATED_GROUPS):
            key = (g,) + _attn_geometry(seq_len, dil)
            if key not in bias_cache:
                bias_cache[key] = _expand_bias(rel_bias_table, _bias_bucket_index(seq_len, dil), g)
            out.append(bias_cache[key])
        return out

    y_prompt = _trunk(x_prompt, ada[:, :nb_p], biases_for(x_prompt.shape[1]), p)
    y_sample = _trunk(x_sample, ada[:, nb_p:], biases_for(x_sample.shape[1]), p)
    return (y_prompt, y_sample)
```

```python
import functools
import math

import jax
import jax.numpy as jnp
from jax import lax
from jax.experimental import pallas as pl
from jax.experimental.pallas import tpu as pltpu

F32 = jnp.float32
BF16 = jnp.bfloat16

D_MODEL = 1024
DEPTH = 4
N_MIXERS = 2
DILATED_GROUPS = ((128, 1), (512, 4), (2048, 16))
N_GROUPS = len(DILATED_GROUPS)
HEADS_PER_GROUP = 8
HEAD_DIM = 128
REL_BUCKETS = 32
REL_MAX_DIST = 1024
CONV_WIDTH = 31
CONV_PAD = (CONV_WIDTH - 1) // 2
D_FF = 2816
NEG_INF = -1e30
EPS = 1e-6

HALF = DILATED_GROUPS[0][0] // (2 * DILATED_GROUPS[0][1])
assert all(w // (2 * d) == HALF for w, d in DILATED_GROUPS)

TOKEN_TILE = 512
QUERY_BLOCK = 128
ATTN_BLOCKS_PER_STEP = 16
HALO_ROWS = 16
CONV_ROW_CHUNK = 32
SUBLANES = 8
BF16_SUBLANES = 16
LANES = 128
FF_LOOP_CHUNK = 256
CONV_UNITS_PER_FFN_ITER = 0
FF_CHUNKS = ((0, 768), (768, 768), (1536, 768), (2304, 512))
VMEM_LIMIT_BYTES = 60000 * 1024


def _params(n_axes):
    return pltpu.CompilerParams(dimension_semantics=("parallel",) * n_axes,
                                vmem_limit_bytes=VMEM_LIMIT_BYTES)


def _resident(shape):
    nd = len(shape)
    return pl.BlockSpec(shape, lambda *_: (0,) * nd, pipeline_mode=pl.Buffered(1))


def _silu(x):
    return x * jax.nn.sigmoid(x)


def _mod_rmsnorm(x, gain, shift, scale):
    ms = jnp.mean(x * x, axis=-1, keepdims=True)
    y = x * lax.rsqrt(ms + EPS) * gain
    return y * (1.0 + scale) + shift


def _ada_kernel(c_ref, w_ref, b_ref, o_ref):
    ca = _silu(c_ref[...]).astype(BF16)
    o_ref[0] = jnp.dot(ca, w_ref[0].astype(BF16), preferred_element_type=F32) + b_ref[0]


def _ada_all(c, ada_w, ada_b):
    nb, d = c.shape
    depth, _, width = ada_w.shape
    tn = 1536
    assert width % tn == 0
    return pl.pallas_call(
        _ada_kernel,
        out_shape=jax.ShapeDtypeStruct((depth, nb, width), F32),
        grid=(depth, width // tn),
        in_specs=[pl.BlockSpec((nb, d), lambda i, j: (0, 0)),
                  pl.BlockSpec((1, d, tn), lambda i, j: (i, 0, j)),
                  pl.BlockSpec((1, 1, tn), lambda i, j: (i, 0, j))],
        out_specs=pl.BlockSpec((1, nb, tn), lambda i, j: (i, 0, j)),
        compiler_params=_params(2),
        name="ada_proj",
    )(c, ada_w, ada_b.reshape(depth, 1, width))


def _t5_bucket(rel):
    half = REL_BUCKETS // 2
    max_exact = half // 2
    ret = jnp.where(rel > 0, half, 0)
    n = jnp.abs(rel)
    nf = jnp.maximum(n, 1).astype(F32)
    large = max_exact + (jnp.log(nf / max_exact) / math.log(REL_MAX_DIST / max_exact)
                         * (half - max_exact)).astype(jnp.int32)
    large = jnp.minimum(large, half - 1)
    return ret + jnp.where(n < max_exact, n, large)


def _attn_geometry(seq_len, dil):
    sub_len = seq_len // dil
    qb = min(QUERY_BLOCK, sub_len)
    kw = min(qb + 2 * HALF, sub_len)
    nblk = sub_len // qb
    assert sub_len * dil == seq_len and nblk * qb == sub_len and qb % HALF == 0
    return sub_len, qb, kw, nblk


def _key_offset(n, qb, kw, sub_len):
    return min(max(n * qb - HALF, 0), sub_len - kw)


def _bias_bucket_index(seq_len, dil):
    sub_len, qb, kw, nblk = _attn_geometry(seq_len, dil)
    slots = []
    for n in (0, min(1, nblk - 1), nblk - 1):
        delta = n * qb - _key_offset(n, qb, kw, sub_len)
        rel = jnp.arange(kw)[None, :] - jnp.arange(qb)[:, None] - delta
        slots.append(jnp.where(jnp.abs(rel) <= HALF, _t5_bucket(rel * dil), -1))
    return jnp.stack(slots).astype(jnp.int32)


def _bias_kernel(tab_ref, idx_ref, o_ref, *, head0):
    h = pl.program_id(0)
    idx = idx_ref[...]
    acc = jnp.full(idx.shape, NEG_INF, F32)
    for b in range(REL_BUCKETS):
        acc = jnp.where(idx == b, tab_ref[b, head0 + h], acc)
    o_ref[0] = acc


def _expand_bias(rel_table, idx, group):
    _, qb, kw = idx.shape
    return pl.pallas_call(
        functools.partial(_bias_kernel, head0=group * HEADS_PER_GROUP),
        out_shape=jax.ShapeDtypeStruct((HEADS_PER_GROUP, 3, qb, kw), F32),
        grid=(HEADS_PER_GROUP,),
        in_specs=[pl.BlockSpec(memory_space=pltpu.SMEM),
                  pl.BlockSpec((3, qb, kw), lambda h: (0, 0, 0))],
        out_specs=pl.BlockSpec((1, 3, qb, kw), lambda h: (h, 0, 0, 0)),
        compiler_params=_params(1),
        name="rel_bias_expand",
    )(rel_table, idx)


def _qkv_kernel(x_ref, ada_ref, g_ref, w_ref, qg_ref, kg_ref, o_ref, *scratch, dil):
    tm, d = x_ref.shape[1], x_ref.shape[2]
    sub = tm // dil
    h = _mod_rmsnorm(x_ref[0], g_ref[...], ada_ref[0, 0:1, :], ada_ref[0, 1:2, :])
    if dil == 1:
        h = h.astype(BF16)
    else:
        slab_ref, perm_ref = scratch
        for c in range(d // LANES):
            slab_ref[c] = h[:, c * LANES:(c + 1) * LANES]
        for c in range(d // LANES):
            for r in range(dil):
                perm_ref[r * sub:(r + 1) * sub, c * LANES:(c + 1) * LANES] = (
                    slab_ref[c, pl.ds(r, sub, stride=dil), :].astype(BF16))
        h = perm_ref[...]
    width = HEADS_PER_GROUP * HEAD_DIM
    for t in range(3):
        acc = jnp.dot(h, w_ref[:, t * width:(t + 1) * width], preferred_element_type=F32)
        for hd in range(HEADS_PER_GROUP):
            c = acc[:, hd * HEAD_DIM:(hd + 1) * HEAD_DIM]
            if t < 2:
                ms = jnp.mean(c * c, axis=-1, keepdims=True)
                c = c * lax.rsqrt(ms + EPS) * (qg_ref[...] if t == 0 else kg_ref[...])
            if t == 0:
                c = c * (HEAD_DIM ** -0.5)
            o_ref[0, t, hd] = c.astype(BF16).reshape(dil, sub, HEAD_DIM)


def _qkv_proj(x, ada, gain, w, q_gain, k_gain, dil):
    b, s, d = x.shape
    sub_len = s // dil
    tm = min(2 * TOKEN_TILE if s == 4096 else TOKEN_TILE, s)
    assert s % tm == 0 and tm % (dil * BF16_SUBLANES) == 0
    width = 3 * HEADS_PER_GROUP * HEAD_DIM
    scratch = [] if dil == 1 else [pltpu.VMEM((d // LANES, tm, LANES), F32),
                                   pltpu.VMEM((tm, d), BF16)]
    return pl.pallas_call(
        functools.partial(_qkv_kernel, dil=dil),
        out_shape=jax.ShapeDtypeStruct((b, 3, HEADS_PER_GROUP, dil, sub_len, HEAD_DIM), BF16),
        grid=(b, s // tm),
        in_specs=[pl.BlockSpec((1, tm, d), lambda bi, i: (bi, i, 0)),
                  pl.BlockSpec((1, 6, d), lambda bi, i: (bi, 0, 0)),
                  _resident((1, d)),
                  _resident((d, width)),
                  _resident((1, HEAD_DIM)),
                  _resident((1, HEAD_DIM))],
        out_specs=pl.BlockSpec((1, 3, HEADS_PER_GROUP, dil, tm // dil, HEAD_DIM),
                               lambda bi, i: (bi, 0, 0, 0, i, 0)),
        scratch_shapes=scratch,
        compiler_params=_params(2),
        name=f"qkv_proj_d{dil}",
    )(x, ada, gain, w, q_gain, k_gain)


def _attn_kernel(qkv0_ref, qkv1_ref, qkv2_ref, b0_ref, b1_ref, b2_ref, o_ref, og_ref, mg_ref,
                 lg_ref, *, seq_len):
    for g,(qkv_ref, bias_ref) in enumerate(((qkv0_ref, b0_ref), (qkv1_ref, b1_ref),
                                              (qkv2_ref, b2_ref))):
        dil = DILATED_GROUPS[g][1]
        sub_len, qb, kw, nblk = _attn_geometry(seq_len, dil)

        def block(r, n, g=g, dil=dil, sub_len=sub_len, qb=qb, kw=kw, nblk=nblk,
                  qkv_ref=qkv_ref, bias_ref=bias_ref):
            q0 = pl.multiple_of(r * sub_len + n * qb, qb)
            koff = jnp.clip(n * qb - HALF, 0, sub_len - kw)
            k0 = pl.multiple_of(r * sub_len + koff, HALF)
            q = qkv_ref[0, 0, 0, pl.ds(q0, qb), :]
            k = qkv_ref[0, 1, 0, pl.ds(k0, kw), :]
            v = qkv_ref[0, 2, 0, pl.ds(k0, kw), :]
            slot = jnp.where(n == 0, 0, jnp.where(n == nblk - 1, 2, 1))
            s = lax.dot_general(q, k, (((1,), (1,)), ((), ())), preferred_element_type=F32)
            s = s + bias_ref[0, slot]
            m = jnp.max(s, axis=-1, keepdims=True)
            p = jnp.exp(s - m).astype(BF16)
            v_ones = jnp.concatenate([v, jnp.ones((kw, HEAD_DIM), BF16)], axis=1)
            o_l = jnp.dot(p, v_ones, preferred_element_type=F32)
            start = r + n * (qb * dil)
            rows = pl.ds(start, qb) if dil == 1 else pl.ds(start, qb, stride=dil)
            og_ref[g, rows, :] = o_l[:, :HEAD_DIM]
            mg_ref[g, rows, :] = jnp.broadcast_to(m, (qb, HEAD_DIM))
            lg_ref[g, rows, :] = o_l[:, HEAD_DIM:]

        total = dil * nblk
        par = math.gcd(total, ATTN_BLOCKS_PER_STEP)

        def step(it, carry, block=block, nblk=nblk, par=par):
            for u in range(par):
                idx = it * par + u
                block(lax.div(idx, nblk), lax.rem(idx, nblk))
            return carry

        lax.fori_loop(0, total // par, step, 0)

    rc = 256

    def merge(c, carry):
        rows = pl.ds(pl.multiple_of(c * rc, rc), rc)
        ms = [mg_ref[g, rows, :] for g in range(N_GROUPS)]
        mx = jnp.maximum(jnp.maximum(ms[0], ms[1]), ms[2])
        num = den = None
        for g in range(N_GROUPS):
            w = jnp.exp(ms[g] - mx)
            num = w * og_ref[g, rows, :] if num is None else num + w * og_ref[g, rows, :]
            den = w * lg_ref[g, rows, :] if den is None else den + w * lg_ref[g, rows, :]
        o_ref[0, 0, rows, :] = (num / den).astype(BF16)
        return carry

    lax.fori_loop(0, seq_len // rc, merge, 0)


def _attention(qkvs, biases):
    b = qkvs[0].shape[0]
    s = qkvs[0].shape[3] * qkvs[0].shape[4]
    qkvs = [t.reshape(b, 3, HEADS_PER_GROUP, s, HEAD_DIM) for t in qkvs]
    qkv_spec = pl.BlockSpec((1, 3, 1, s, HEAD_DIM), lambda bi, h: (bi, 0, h, 0, 0))
    bias_specs = [pl.BlockSpec((1,) + t.shape[1:], lambda bi, h: (h, 0, 0, 0)) for t in biases]
    return pl.pallas_call(
        functools.partial(_attn_kernel, seq_len=s),
        out_shape=jax.ShapeDtypeStruct((b, HEADS_PER_GROUP, s, HEAD_DIM), BF16),
        grid=(b, HEADS_PER_GROUP),
        in_specs=[qkv_spec] * N_GROUPS + bias_specs,
        out_specs=pl.BlockSpec((1, 1, s, HEAD_DIM), lambda bi, h: (bi, h, 0, 0)),
        scratch_shapes=[pltpu.VMEM((N_GROUPS, s, HEAD_DIM), F32)] * 3,
        compiler_params=_params(2),
        name="dilated_attention",
    )(*qkvs, *biases)


def _residual_ffn(x, y, ada_ref, g2_ref, w_in_ref, w_out_ref):
    x1 = x + ada_ref[0, 2:3, :] * y
    h = _mod_rmsnorm(x1, g2_ref[...], ada_ref[0, 3:4, :], ada_ref[0, 4:5, :]).astype(BF16)
    acc = None
    for c0, cw in FF_CHUNKS:
        gate = jnp.dot(h, w_in_ref[:, c0:c0 + cw], preferred_element_type=F32)
        up = jnp.dot(h, w_in_ref[:, D_FF + c0:D_FF + c0 + cw], preferred_element_type=F32)
        a = (_silu(gate) * up).astype(BF16)
        part = jnp.dot(a, w_out_ref[c0:c0 + cw, :], preferred_element_type=F32)
        acc = part if acc is None else acc + part
    return x1 + ada_ref[0, 5:6, :] * acc


def _attn_tail_kernel(o_ref, x_ref, ada_ref, wo_ref, g2_ref, w_in_ref, w_out_ref, out_ref):
    o = jnp.concatenate([o_ref[0, h] for h in range(HEADS_PER_GROUP)], axis=-1)
    y = jnp.dot(o, wo_ref[...], preferred_element_type=F32)
    out_ref[0] = _residual_ffn(x_ref[0], y, ada_ref, g2_ref, w_in_ref, w_out_ref)


def _attn_tail(o, x, ada, w_o, gain2, w_in, w_out):
    b, s, d = x.shape
    tm = min(TOKEN_TILE, s)
    return pl.pallas_call(
        _attn_tail_kernel,
        out_shape=jax.ShapeDtypeStruct((b, s, d), F32),
        grid=(b, s // tm),
        in_specs=[pl.BlockSpec((1, HEADS_PER_GROUP, tm, HEAD_DIM), lambda bi, i: (bi, 0, i, 0)),
                  pl.BlockSpec((1, tm, d), lambda bi, i: (bi, i, 0)),
                  pl.BlockSpec((1, 6, d), lambda bi, i: (bi, 0, 0)),
                  _resident(w_o.shape), _resident((1, d)),
                  _resident(w_in.shape), _resident(w_out.shape)],
        out_specs=pl.BlockSpec((1, tm, d), lambda bi, i: (bi, i, 0)),
        compiler_params=_params(2),
        name="attn_out_ffn",
    )(o, x, ada, w_o, gain2, w_in, w_out)


def _conv_head_kernel(x_ref, ada_ref, g_ref, w_ref, b_ref, u_ref):
    d = x_ref.shape[-1]
    h = _mod_rmsnorm(x_ref[0], g_ref[...], ada_ref[0, 0:1, :], ada_ref[0, 1:2, :]).astype(BF16)
    a = jnp.dot(h, w_ref[:, :d], preferred_element_type=F32) + b_ref[:, :d]
    gt = jnp.dot(h, w_ref[:, d:], preferred_element_type=F32) + b_ref[:, d:]
    u_ref[0] = a * jax.nn.sigmoid(gt)


def _conv_head(x, ada, gain, w_pw1, b_pw1):
    b, s, d = x.shape
    tm = min(TOKEN_TILE, s)
    return pl.pallas_call(
        _conv_head_kernel,
        out_shape=jax.ShapeDtypeStruct((b, s, d), F32),
        grid=(b, s // tm),
        in_specs=[pl.BlockSpec((1, tm, d), lambda bi, i: (bi, i, 0)),
                  pl.BlockSpec((1, 6, d), lambda bi, i: (bi, 0, 0)),
                  _resident((1, d)), _resident(w_pw1.shape), _resident((1, 2 * d))],
        out_specs=pl.BlockSpec((1, tm, d), lambda bi, i: (bi, i, 0)),
        compiler_params=_params(2),
        name="conv_pw1_glu",
    )(x, ada, gain, w_pw1, b_pw1)


def _fill_conv_window(win_ref, prev, cur_ref, nxt):
    tm, d = cur_ref.shape[1], cur_ref.shape[2]
    for c in range(d // LANES):
        lanes = slice(c * LANES, (c + 1) * LANES)
        win_ref[c, 0:HALO_ROWS, :] = prev[:, lanes]
        win_ref[c, HALO_ROWS:HALO_ROWS + tm, :] = cur_ref[0, :, lanes]
        win_ref[c, HALO_ROWS + tm:, :] = nxt[:, lanes]


def _conv_rows(win_ref, wdw_ref, cv_ref, c, r0):
    acc = [None, None]
    taps = {}
    for s in range(CONV_WIDTH + 1):
        rows = pl.ds(r0 + (HALO_ROWS - CONV_PAD + s), CONV_ROW_CHUNK, stride=2)
        x = win_ref[c, rows, :]
        for parity in range(2):
            k = s - parity
            if 0 <= k < CONV_WIDTH:
                if k not in taps:
                    taps[k] = wdw_ref[c, k:k + 1, :]
                tap = x * taps[k]
                acc[parity] = tap if acc[parity] is None else acc[parity] + tap
    for parity in range(2):
        cv_ref[c, pl.ds(r0 + parity, CONV_ROW_CHUNK, stride=2), :] = acc[parity]


def _conv_tail_kernel(u0_ref, next0_ref, prev_ref, u_ref, next_ref, x_ref, ada_ref, wdw_ref,
                      bdw_ref, lng_ref, lnb_ref, w2_ref, b2_ref, g2_ref, w_gu_ref, w_out_ref,
                      out_ref, win_ref, cv_ref, x1_ref, h_ref, acc_ref, a_ref, *, tiles_per_seq):
    tm, d = u_ref.shape[1], u_ref.shape[2]
    n_chunks = tm // (2 * CONV_ROW_CHUNK)
    t = pl.program_id(0)

    @pl.when(t == 0)
    def _():
        nxt0 = next0_ref[0] if tiles_per_seq > 1 else jnp.zeros((HALO_ROWS, d), F32)
        _fill_conv_window(win_ref, jnp.zeros((HALO_ROWS, d), F32), u0_ref, nxt0)
        for c in range(d // LANES):
            def rows(j, carry, c=c):
                _conv_rows(win_ref, wdw_ref, cv_ref, c, j * (2 * CONV_ROW_CHUNK))
                return carry
            lax.fori_loop(0, n_chunks, rows, 0)

    cv = jnp.concatenate([cv_ref[c] for c in range(d // LANES)], axis=1) + bdw_ref[...]
    mu = jnp.mean(cv, axis=-1, keepdims=True)
    xc = cv - mu
    var = jnp.mean(xc * xc, axis=-1, keepdims=True)
    ln = xc * lax.rsqrt(var + EPS) * lng_ref[...] + lnb_ref[...]
    act = _silu(ln).astype(BF16)

    i_next = lax.rem(t + 1, tiles_per_seq)
    prev = jnp.where(i_next > 0, prev_ref[0], 0.0)
    nxt = jnp.where(i_next < tiles_per_seq - 1, next_ref[0], 0.0)
    _fill_conv_window(win_ref, prev, u_ref, nxt)

    y = jnp.dot(act, w2_ref[...], preferred_element_type=F32) + b2_ref[...]
    x1 = x_ref[0] + ada_ref[0, 2:3, :] * y
    x1_ref[...] = x1
    h_ref[...] = _mod_rmsnorm(x1, g2_ref[...], ada_ref[0, 3:4, :], ada_ref[0, 4:5, :]).astype(BF16)
    acc_ref[...] = jnp.zeros_like(acc_ref)

    def gated(j):
        gu = jnp.dot(h_ref[...], w_gu_ref[j], preferred_element_type=F32)
        return (_silu(gu[:, :FF_LOOP_CHUNK]) * gu[:, FF_LOOP_CHUNK:]).astype(BF16)

    n_ff = w_gu_ref.shape[0]
    n_units = (d // LANES) * n_chunks
    units_per_iter = 0 if tiles_per_seq == 8 else 3
    a_ref[...] = gated(0)

    def conv_unit(unit):
        _conv_rows(win_ref, wdw_ref, cv_ref, lax.div(unit, n_chunks),
                   lax.rem(unit, n_chunks) * (2 * CONV_ROW_CHUNK))

    def ffn_chunk(j, carry):
        a_cur = a_ref[...]
        a_ref[...] = gated(j + 1)
        acc_ref[...] += jnp.dot(a_cur, w_out_ref[j], preferred_element_type=F32)
        for i in range(units_per_iter):
            conv_unit(jnp.minimum(j * units_per_iter + i, n_units - 1))
        return carry

    lax.fori_loop(0, n_ff - 1, ffn_chunk, 0)

    def conv_rest(unit, carry):
        conv_unit(unit)
        return carry

    lax.fori_loop(min((n_ff - 1) * units_per_iter, n_units), n_units, conv_rest, 0)
    ffn = acc_ref[...] + jnp.dot(a_ref[...], w_out_ref[n_ff - 1], preferred_element_type=F32)
    out_ref[0] = x1_ref[...] + ada_ref[0, 5:6, :] * ffn


def _conv_tail(u, x, ada, w_dw, b_dw, ln_g, ln_b, w_pw2, b_pw2, gain2, w_in, w_out):
    b, s, d = x.shape
    tm = min(TOKEN_TILE, s)
    assert s % tm == 0 and tm % (2 * CONV_ROW_CHUNK) == 0 and HALO_ROWS >= CONV_PAD + 1
    nt = s // tm
    hb = tm // HALO_ROWS
    n_halo = s // HALO_ROWS
    last = b * nt - 1

    def conv_tile(t):
        tn = jnp.minimum(t + 1, last)
        return tn // nt, tn % nt

    def u_next(t):
        bi, i = conv_tile(t)
        return bi, i, 0

    def halo_before(t):
        bi, i = conv_tile(t)
        return bi, jnp.maximum(i * hb - 1, 0), 0

    def halo_after(t):
        bi, i = conv_tile(t)
        return bi, jnp.minimum((i + 1) * hb, n_halo - 1), 0

    tile = lambda t: (t // nt, t % nt, 0)
    n_ff = D_FF // FF_LOOP_CHUNK
    w_gu = w_in.reshape(d, 2, n_ff, FF_LOOP_CHUNK).transpose(2, 0, 1, 3).reshape(
        n_ff, d, 2 * FF_LOOP_CHUNK)
    w_out = w_out.reshape(n_ff, FF_LOOP_CHUNK, d)
    w_dw = w_dw.reshape(CONV_WIDTH, d // LANES, LANES).transpose(1, 0, 2)
    return pl.pallas_call(
        functools.partial(_conv_tail_kernel, tiles_per_seq=nt),
        out_shape=jax.ShapeDtypeStruct((b, s, d), F32),
        grid=(b * nt,),
        in_specs=[pl.BlockSpec((1, tm, d), lambda t: (0, 0, 0), pipeline_mode=pl.Buffered(1)),
                  pl.BlockSpec((1, HALO_ROWS, d), lambda t: (0, min(hb, n_halo - 1), 0),
                               pipeline_mode=pl.Buffered(1)),
                  pl.BlockSpec((1, HALO_ROWS, d), halo_before),
                  pl.BlockSpec((1, tm, d), u_next),
                  pl.BlockSpec((1, HALO_ROWS, d), halo_after),
                  pl.BlockSpec((1, tm, d), tile),
                  pl.BlockSpec((1, 6, d), lambda t: (t // nt, 0, 0)),
                  _resident(w_dw.shape), _resident((1, d)), _resident((1, d)), _resident((1, d)),
                  _resident(w_pw2.shape), _resident((1, d)), _resident((1, d)),
                  _resident(w_gu.shape), _resident(w_out.shape)],
        out_specs=pl.BlockSpec((1, tm, d), tile),
        scratch_shapes=[pltpu.VMEM((d // LANES, tm + 2 * HALO_ROWS, LANES), F32),
                        pltpu.VMEM((d // LANES, tm, LANES), F32),
                        pltpu.VMEM((tm, d), F32), pltpu.VMEM((tm, d), BF16),
                        pltpu.VMEM((tm, d), F32), pltpu.VMEM((tm, FF_LOOP_CHUNK), BF16)],
        compiler_params=pltpu.CompilerParams(dimension_semantics=("arbitrary",),
                                             vmem_limit_bytes=VMEM_LIMIT_BYTES),
        name="conv_dw_pw2_ffn",
    )(u, u, u, u, u, x, ada, w_dw, b_dw, ln_g, ln_b, w_pw2, b_pw2, gain2, w_gu, w_out)


def _trunk(x, ada, biases, p):
    d = x.shape[-1]
    row = lambda v: v.reshape(1, -1)
    for i in range(DEPTH):
        j = i // N_MIXERS
        gain1, gain2 = row(p["norm1_g"][i]), row(p["norm2_g"][i])
        w_in, w_out = p["ffn_w_in"][i], p["ffn_w_out"][i]
        if i % N_MIXERS == 0:
            w4 = p["attn_w_qkv"][j].reshape(d, 3, N_GROUPS, HEADS_PER_GROUP * HEAD_DIM)
            qkvs = [_qkv_proj(x, ada[i], gain1, w4[:, :, g, :].reshape(d, -1),
                              row(p["attn_q_gain"][j]), row(p["attn_k_gain"][j]), dil)
                    for g, (_, dil) in enumerate(DILATED_GROUPS)]
            o = _attention(qkvs, biases)
            x = _attn_tail(o, x, ada[i], p["attn_w_o"][j], gain2, w_in, w_out)
        else:
            u = _conv_head(x, ada[i], gain1, p["conv_w_pw1"][j], row(p["conv_b_pw1"][j]))
            x = _conv_tail(u, x, ada[i], p["conv_w_dw"][j], row(p["conv_b_dw"][j]),
                           row(p["conv_ln_g"][j]), row(p["conv_ln_b"][j]), p["conv_w_pw2"][j],
                           row(p["conv_b_pw2"][j]), gain2, w_in, w_out)
    return x


def kernel(x_prompt, x_sample, c_prompt, c_sample, rel_bias_table, norm1_g, norm2_g, ada_w, ada_b, attn_w_qkv, attn_q_gain, attn_k_gain, attn_w_o, conv_w_pw1, conv_b_pw1, conv_w_dw, conv_b_dw, conv_ln_g, conv_ln_b, conv_w_pw2, conv_b_pw2, ffn_w_in, ffn_w_out):
    p = dict(norm1_g=norm1_g, norm2_g=norm2_g, attn_q_gain=attn_q_gain, attn_k_gain=attn_k_gain,
             conv_b_pw1=conv_b_pw1, conv_w_dw=conv_w_dw, conv_b_dw=conv_b_dw, conv_ln_g=conv_ln_g,
             conv_ln_b=conv_ln_b, conv_b_pw2=conv_b_pw2,
             attn_w_qkv=attn_w_qkv.astype(BF16), attn_w_o=attn_w_o.astype(BF16),
             conv_w_pw1=conv_w_pw1.astype(BF16), conv_w_pw2=conv_w_pw2.astype(BF16),
             ffn_w_in=ffn_w_in.astype(BF16), ffn_w_out=ffn_w_out.astype(BF16))
    nb_p = c_prompt.shape[0]
    d = x_prompt.shape[-1]
    c_all = jnp.concatenate([c_prompt, c_sample], axis=0)
    ada = _ada_all(c_all, ada_w, ada_b).reshape(DEPTH, c_all.shape[0], 6, d)

    bias_cache = {}

    def biases_for(seq_len):
        out = []
        for g, (_, dil) in enumerate(DILATED_GROUPS):
            key = (g,) + _attn_geometry(seq_len, dil)
            if key not in bias_cache:
                bias_cache[key] = _expand_bias(rel_bias_table, _bias_bucket_index(seq_len, dil), g)
            out.append(bias_cache[key])
        return out

    y_prompt = _trunk(x_prompt, ada[:, :nb_p], biases_for(x_prompt.shape[1]), p)
    y_sample = _trunk(x_sample, ada[:, nb_p:], biases_for(x_sample.shape[1]), p)
    return (y_prompt, y_sample)
```

```python
import functools
import math

import jax
import jax.numpy as jnp
from jax import lax
from jax.experimental import pallas as pl
from jax.experimental.pallas import tpu as pltpu

F32 = jnp.float32
BF16 = jnp.bfloat16

D_MODEL = 1024
DEPTH = 4
N_MIXERS = 2
DILATED_GROUPS = ((128, 1), (512, 4), (2048, 16))
N_GROUPS = len(DILATED_GROUPS)
HEADS_PER_GROUP = 8
HEAD_DIM = 128
REL_BUCKETS = 32
REL_MAX_DIST = 1024
CONV_WIDTH = 31
CONV_PAD = (CONV_WIDTH - 1) // 2
D_FF = 2816
NEG_INF = -1e30
EPS = 1e-6

HALF = DILATED_GROUPS[0][0] // (2 * DILATED_GROUPS[0][1])
assert all(w // (2 * d) == HALF for w, d in DILATED_GROUPS)

TOKEN_TILE = 512
QKV_TOKEN_TILE = 1024
QUERY_BLOCK = 128
ATTN_BLOCKS_PER_STEP = 16
HALO_ROWS = 16
CONV_ROW_CHUNK = 32
SUBLANES = 8
BF16_SUBLANES = 16
LANES = 128
FF_CHUNKS = ((0, 768), (768, 768), (1536, 768), (2304, 512))
VMEM_LIMIT_BYTES = 60000 * 1024


def _params(n_axes):
    return pltpu.CompilerParams(dimension_semantics=("parallel",) * n_axes,
                                vmem_limit_bytes=VMEM_LIMIT_BYTES)


def _resident(shape):
    nd = len(shape)
    return pl.BlockSpec(shape, lambda *_: (0,) * nd, pipeline_mode=pl.Buffered(1))


def _silu(x):
    return x * jax.nn.sigmoid(x)


def _mod_rmsnorm(x, gain, shift, scale):
    ms = jnp.mean(x * x, axis=-1, keepdims=True)
    y = x * lax.rsqrt(ms + EPS) * gain
    return y * (1.0 + scale) + shift


def _ada_kernel(c_ref, w_ref, b_ref, o_ref):
    ca = _silu(c_ref[...]).astype(BF16)
    o_ref[0] = jnp.dot(ca, w_ref[0].astype(BF16), preferred_element_type=F32) + b_ref[0]


def _ada_all(c, ada_w, ada_b):
    nb, d = c.shape
    depth, _, width = ada_w.shape
    tn = 1536
    assert width % tn == 0
    return pl.pallas_call(
        _ada_kernel,
        out_shape=jax.ShapeDtypeStruct((depth, nb, width), F32),
        grid=(depth, width // tn),
        in_specs=[pl.BlockSpec((nb, d), lambda i, j: (0, 0)),
                  pl.BlockSpec((1, d, tn), lambda i, j: (i, 0, j)),
                  pl.BlockSpec((1, 1, tn), lambda i, j: (i, 0, j))],
        out_specs=pl.BlockSpec((1, nb, tn), lambda i, j: (i, 0, j)),
        compiler_params=_params(2),
        name="ada_proj",
    )(c, ada_w, ada_b.reshape(depth, 1, width))


def _t5_bucket(rel):
    half = REL_BUCKETS // 2
    max_exact = half // 2
    ret = jnp.where(rel > 0, half, 0)
    n = jnp.abs(rel)
    nf = jnp.maximum(n, 1).astype(F32)
    large = max_exact + (jnp.log(nf / max_exact) / math.log(REL_MAX_DIST / max_exact)
                         * (half - max_exact)).astype(jnp.int32)
    large = jnp.minimum(large, half - 1)
    return ret + jnp.where(n < max_exact, n, large)


def _attn_geometry(seq_len, dil):
    sub_len = seq_len // dil
    qb = min(QUERY_BLOCK, sub_len)
    kw = min(qb + 2 * HALF, sub_len)
    nblk = sub_len // qb
    assert sub_len * dil == seq_len and nblk * qb == sub_len and qb % HALF == 0
    return sub_len, qb, kw, nblk


def _key_offset(n, qb, kw, sub_len):
    return min(max(n * qb - HALF, 0), sub_len - kw)


def _bias_bucket_index(seq_len, dil):
    sub_len, qb, kw, nblk = _attn_geometry(seq_len, dil)
    slots = []
    for n in (0, min(1, nblk - 1), nblk - 1):
        delta = n * qb - _key_offset(n, qb, kw, sub_len)
        rel = jnp.arange(kw)[None, :] - jnp.arange(qb)[:, None] - delta
        slots.append(jnp.where(jnp.abs(rel) <= HALF, _t5_bucket(rel * dil), -1))
    return jnp.stack(slots).astype(jnp.int32)


def _bias_kernel(tab_ref, idx_ref, o_ref, *, head0):
    h = pl.program_id(0)
    idx = idx_ref[...]
    acc = jnp.full(idx.shape, NEG_INF, F32)
    for b in range(REL_BUCKETS):
        acc = jnp.where(idx == b, tab_ref[b, head0 + h], acc)
    o_ref[0] = acc


def _expand_bias(rel_table, idx, group):
    _, qb, kw = idx.shape
    return pl.pallas_call(
        functools.partial(_bias_kernel, head0=group * HEADS_PER_GROUP),
        out_shape=jax.ShapeDtypeStruct((HEADS_PER_GROUP, 3, qb, kw), F32),
        grid=(HEADS_PER_GROUP,),
        in_specs=[pl.BlockSpec(memory_space=pltpu.SMEM),
                  pl.BlockSpec((3, qb, kw), lambda h: (0, 0, 0))],
        out_specs=pl.BlockSpec((1, 3, qb, kw), lambda h: (h, 0, 0, 0)),
        compiler_params=_params(1),
        name="rel_bias_expand",
    )(rel_table, idx)


def _qkv_kernel(x_ref, ada_ref, g_ref, w_ref, qg_ref, kg_ref, o_ref, *scratch, dil):
    tm, d = x_ref.shape[1], x_ref.shape[2]
    sub = tm // dil
    h = _mod_rmsnorm(x_ref[0], g_ref[...], ada_ref[0, 0:1, :], ada_ref[0, 1:2, :])
    if dil == 1:
        h = h.astype(BF16)
    else:
        slab_ref, perm_ref = scratch
        for c in range(d // LANES):
            slab_ref[c] = h[:, c * LANES:(c + 1) * LANES]
        for c in range(d // LANES):
            for r in range(dil):
                perm_ref[r * sub:(r + 1) * sub, c * LANES:(c + 1) * LANES] = (
                    slab_ref[c, pl.ds(r, sub, stride=dil), :].astype(BF16))
        h = perm_ref[...]
    width = HEADS_PER_GROUP * HEAD_DIM
    for t in range(3):
        acc = jnp.dot(h, w_ref[:, t * width:(t + 1) * width], preferred_element_type=F32)
        for hd in range(HEADS_PER_GROUP):
            c = acc[:, hd * HEAD_DIM:(hd + 1) * HEAD_DIM]
            if t < 2:
                ms = jnp.mean(c * c, axis=-1, keepdims=True)
                c = c * lax.rsqrt(ms + EPS) * (qg_ref[...] if t == 0 else kg_ref[...])
            if t == 0:
                c = c * (HEAD_DIM ** -0.5)
            o_ref[0, t, hd] = c.astype(BF16).reshape(dil, sub, HEAD_DIM)


def _qkv_proj(x, ada, gain, w, q_gain, k_gain, dil):
    b, s, d = x.shape
    sub_len = s // dil
    tm = min(QKV_TOKEN_TILE, s)
    assert s % tm == 0 and tm % (dil * BF16_SUBLANES) == 0
    width = 3 * HEADS_PER_GROUP * HEAD_DIM
    scratch = [] if dil == 1 else [pltpu.VMEM((d // LANES, tm, LANES), F32),
                                   pltpu.VMEM((tm, d), BF16)]
    return pl.pallas_call(
        functools.partial(_qkv_kernel, dil=dil),
        out_shape=jax.ShapeDtypeStruct((b, 3, HEADS_PER_GROUP, dil, sub_len, HEAD_DIM), BF16),
        grid=(b, s // tm),
        in_specs=[pl.BlockSpec((1, tm, d), lambda bi, i: (bi, i, 0)),
                  pl.BlockSpec((1, 6, d), lambda bi, i: (bi, 0, 0)),
                  _resident((1, d)),
                  _resident((d, width)),
                  _resident((1, HEAD_DIM)),
                  _resident((1, HEAD_DIM))],
        out_specs=pl.BlockSpec((1, 3, HEADS_PER_GROUP, dil, tm // dil, HEAD_DIM),
                               lambda bi, i: (bi, 0, 0, 0, i, 0)),
        scratch_shapes=scratch,
        compiler_params=_params(2),
        name=f"qkv_proj_d{dil}",
    )(x, ada, gain, w, q_gain, k_gain)


def _attn_kernel(qkv0_ref, qkv1_ref, qkv2_ref, b0_ref, b1_ref, b2_ref, o_ref, og_ref, mg_ref,
                 lg_ref, *, seq_len):
    for g, (qkv_ref, bias_ref) in enumerate(((qkv0_ref, b0_ref), (qkv1_ref, b1_ref),
                                              (qkv2_ref, b2_ref))):
        dil = DILATED_GROUPS[g][1]
        sub_len, qb, kw, nblk = _attn_geometry(seq_len, dil)

        def block(r, n, g=g, dil=dil, sub_len=sub_len, qb=qb, kw=kw, nblk=nblk,
                  qkv_ref=qkv_ref, bias_ref=bias_ref):
            q0 = pl.multiple_of(r * sub_len + n * qb, qb)
            koff = jnp.clip(n * qb - HALF, 0, sub_len - kw)
            k0 = pl.multiple_of(r * sub_len + koff, HALF)
            q = qkv_ref[0, 0, 0, pl.ds(q0, qb), :]
            k = qkv_ref[0, 1, 0, pl.ds(k0, kw), :]
            v = qkv_ref[0, 2, 0, pl.ds(k0, kw), :]
            slot = jnp.where(n == 0, 0, jnp.where(n == nblk - 1, 2, 1))
            s = lax.dot_general(q, k, (((1,), (1,)), ((), ())), preferred_element_type=F32)
            s = s + bias_ref[0, slot]
            m = jnp.max(s, axis=-1, keepdims=True)
            p = jnp.exp(s - m).astype(BF16)
            v_ones = jnp.concatenate([v, jnp.ones((kw, HEAD_DIM), BF16)], axis=1)
            o_l = jnp.dot(p, v_ones, preferred_element_type=F32)
            start = r + n * (qb * dil)
            rows = pl.ds(start, qb) if dil == 1 else pl.ds(start, qb, stride=dil)
            og_ref[g, rows, :] = o_l[:, :HEAD_DIM]
            mg_ref[g, rows, :] = jnp.broadcast_to(m, (qb, HEAD_DIM))
            lg_ref[g, rows, :] = o_l[:, HEAD_DIM:]

        total = dil * nblk
        par = math.gcd(total, ATTN_BLOCKS_PER_STEP)

        def step(it, carry, block=block, nblk=nblk, par=par):
            for u in range(par):
                idx = it * par + u
                block(lax.div(idx, nblk), lax.rem(idx, nblk))
            return carry

        lax.fori_loop(0, total // par, step, 0)

    rc = 256

    def merge(c, carry):
        rows = pl.ds(pl.multiple_of(c * rc, rc), rc)
        ms = [mg_ref[g, rows, :] for g in range(N_GROUPS)]
        mx = jnp.maximum(jnp.maximum(ms[0], ms[1]), ms[2])
        num = den = None
        for g in range(N_GROUPS):
            w = jnp.exp(ms[g] - mx)
            num = w * og_ref[g, rows, :] if num is None else num + w * og_ref[g, rows, :]
            den = w * lg_ref[g, rows, :] if den is None else den + w * lg_ref[g, rows, :]
        o_ref[0, 0, rows, :] = (num / den).astype(BF16)
        return carry

    lax.fori_loop(0, seq_len // rc, merge, 0)


def _attention(qkvs, biases):
    b = qkvs[0].shape[0]
    s = qkvs[0].shape[3] * qkvs[0].shape[4]
    qkvs = [t.reshape(b, 3, HEADS_PER_GROUP, s, HEAD_DIM) for t in qkvs]
    qkv_spec = pl.BlockSpec((1, 3, 1, s, HEAD_DIM), lambda bi, h: (bi, 0, h, 0, 0))
    bias_specs = [pl.BlockSpec((1,) + t.shape[1:], lambda bi, h: (h, 0, 0, 0)) for t in biases]
    return pl.pallas_call(
        functools.partial(_attn_kernel, seq_len=s),
        out_shape=jax.ShapeDtypeStruct((b, HEADS_PER_GROUP, s, HEAD_DIM), BF16),
        grid=(b, HEADS_PER_GROUP),
        in_specs=[qkv_spec] * N_GROUPS + bias_specs,
        out_specs=pl.BlockSpec((1, 1, s, HEAD_DIM), lambda bi, h: (bi, h, 0, 0)),
        scratch_shapes=[pltpu.VMEM((N_GROUPS, s, HEAD_DIM), F32)] * 3,
        compiler_params=_params(2),
        name="dilated_attention",
    )(*qkvs, *biases)


def _residual_ffn(x, y, ada_ref, g2_ref, w_in_ref, w_out_ref):
    x1 = x + ada_ref[0, 2:3, :] * y
    h = _mod_rmsnorm(x1, g2_ref[...], ada_ref[0, 3:4, :], ada_ref[0, 4:5, :]).astype(BF16)
    acc = None
    for c0, cw in FF_CHUNKS:
        gate = jnp.dot(h, w_in_ref[:, c0:c0 + cw], preferred_element_type=F32)
        up = jnp.dot(h, w_in_ref[:, D_FF + c0:D_FF + c0 + cw], preferred_element_type=F32)
        a = (_silu(gate) * up).astype(BF16)
        part = jnp.dot(a, w_out_ref[c0:c0 + cw, :], preferred_element_type=F32)
        acc = part if acc is None else acc + part
    return x1 + ada_ref[0, 5:6, :] * acc


def _attn_tail_kernel(o_ref, x_ref, ada_ref, wo_ref, g2_ref, w_in_ref, w_out_ref, out_ref):
    o = jnp.concatenate([o_ref[0, h] for h in range(HEADS_PER_GROUP)], axis=-1)
    y = jnp.dot(o, wo_ref[...], preferred_element_type=F32)
    out_ref[0] = _residual_ffn(x_ref[0], y, ada_ref, g2_ref, w_in_ref, w_out_ref)


def _attn_tail(o, x, ada, w_o, gain2, w_in, w_out):
    b, s, d = x.shape
    tm = min(TOKEN_TILE, s)
    return pl.pallas_call(
        _attn_tail_kernel,
        out_shape=jax.ShapeDtypeStruct((b, s, d), F32),
        grid=(b, s // tm),
        in_specs=[pl.BlockSpec((1, HEADS_PER_GROUP, tm, HEAD_DIM), lambda bi, i: (bi, 0, i, 0)),
                  pl.BlockSpec((1, tm, d), lambda bi, i: (bi, i, 0)),
                  pl.BlockSpec((1, 6, d), lambda bi, i: (bi, 0, 0)),
                  _resident(w_o.shape), _resident((1, d)),
                  _resident(w_in.shape), _resident(w_out.shape)],
        out_specs=pl.BlockSpec((1, tm, d), lambda bi, i: (bi, i, 0)),
        compiler_params=_params(2),
        name="attn_out_ffn",
    )(o, x, ada, w_o, gain2, w_in, w_out)


def _conv_head_kernel(x_ref, ada_ref, g_ref, w_ref, b_ref, u_ref):
    d = x_ref.shape[-1]
    h = _mod_rmsnorm(x_ref[0], g_ref[...], ada_ref[0, 0:1, :], ada_ref[0, 1:2, :]).astype(BF16)
    a = jnp.dot(h, w_ref[:, :d], preferred_element_type=F32) + b_ref[:, :d]
    gt = jnp.dot(h, w_ref[:, d:], preferred_element_type=F32) + b_ref[:, d:]
    u_ref[0] = a * jax.nn.sigmoid(gt)


def _conv_head(x, ada, gain, w_pw1, b_pw1):
    b, s, d = x.shape
    tm = min(TOKEN_TILE, s)
    return pl.pallas_call(
        _conv_head_kernel,
        out_shape=jax.ShapeDtypeStruct((b, s, d), F32),
        grid=(b, s // tm),
        in_specs=[pl.BlockSpec((1, tm, d), lambda bi, i: (bi, i, 0)),
                  pl.BlockSpec((1, 6, d), lambda bi, i: (bi, 0, 0)),
                  _resident((1, d)), _resident(w_pw1.shape), _resident((1, 2 * d))],
        out_specs=pl.BlockSpec((1, tm, d), lambda bi, i: (bi, i, 0)),
        compiler_params=_params(2),
        name="conv_pw1_glu",
    )(x, ada, gain, w_pw1, b_pw1)


def _conv_tail_kernel(prev_ref, u_ref, next_ref, x_ref, ada_ref, wdw_ref, bdw_ref, lng_ref,
                      lnb_ref, w2_ref, b2_ref, g2_ref, w_in_ref, w_out_ref, out_ref,
                      win_ref, cv_ref, *, row_chunk):
    tm, d = u_ref.shape[1], u_ref.shape[2]
    i = pl.program_id(1)
    last = pl.num_programs(1) - 1
    prev = jnp.where(i > 0, prev_ref[0], 0.0)
    nxt = jnp.where(i < last, next_ref[0], 0.0)
    for c in range(d // LANES):
        lanes = slice(c * LANES, (c + 1) * LANES)
        win_ref[c, 0:HALO_ROWS, :] = prev[:, lanes]
        win_ref[c, HALO_ROWS:HALO_ROWS + tm, :] = u_ref[0, :, lanes]
        win_ref[c, HALO_ROWS + tm:, :] = nxt[:, lanes]

    for c in range(d // LANES):
        lanes = slice(c * LANES, (c + 1) * LANES)

        def conv_rows(j, carry, c=c, lanes=lanes):
            r0 = j * (2 * row_chunk)
            for parity in range(2):
                acc = None
                for k in range(CONV_WIDTH):
                    start = r0 + (parity + HALO_ROWS - CONV_PAD + k)
                    tap = win_ref[c, pl.ds(start, row_chunk, stride=2), :] * wdw_ref[k:k + 1, lanes]
                    acc = tap if acc is None else acc + tap
                cv_ref[c, pl.ds(r0 + parity, row_chunk, stride=2), :] = acc
            return carry

        lax.fori_loop(0, tm // (2 * row_chunk), conv_rows, 0)

    cv = jnp.concatenate([cv_ref[c] for c in range(d // LANES)], axis=1) + bdw_ref[...]
    mu = jnp.mean(cv, axis=-1, keepdims=True)
    xc = cv - mu
    var = jnp.mean(xc * xc, axis=-1, keepdims=True)
    ln = xc * lax.rsqrt(var + EPS) * lng_ref[...] + lnb_ref[...]
    y = jnp.dot(_silu(ln).astype(BF16), w2_ref[...], preferred_element_type=F32) + b2_ref[...]
    out_ref[0] = _residual_ffn(x_ref[0], y, ada_ref, g2_ref, w_in_ref, w_out_ref)


def _conv_tail(u, x, ada, w_dw, b_dw, ln_g, ln_b, w_pw2, b_pw2, gain2, w_in, w_out):
    b, s, d = x.shape
    tm = min(TOKEN_TILE, s)
    row_chunk = CONV_ROW_CHUNK if s == 4096 else 2 * CONV_ROW_CHUNK
    assert s % tm == 0 and tm % (2 * row_chunk) == 0 and HALO_ROWS >= CONV_PAD + 1
    hb = tm // HALO_ROWS
    n_halo = s // HALO_ROWS
    return pl.pallas_call(
        functools.partial(_conv_tail_kernel, row_chunk=row_chunk),
        out_shape=jax.ShapeDtypeStruct((b, s, d), F32),
        grid=(b, s // tm),
        in_specs=[pl.BlockSpec((1, HALO_ROWS, d), lambda bi, i: (bi, jnp.maximum(i * hb - 1, 0), 0)),
                  pl.BlockSpec((1, tm, d), lambda bi, i: (bi, i, 0)),
                  pl.BlockSpec((1, HALO_ROWS, d),
                               lambda bi, i: (bi, jnp.minimum((i + 1) * hb, n_halo - 1), 0)),
                  pl.BlockSpec((1, tm, d), lambda bi, i: (bi, i, 0)),
                  pl.BlockSpec((1, 6, d), lambda bi, i: (bi, 0, 0)),
                  _resident(w_dw.shape), _resident((1, d)), _resident((1, d)), _resident((1, d)),
                  _resident(w_pw2.shape), _resident((1, d)), _resident((1, d)),
                  _resident(w_in.shape), _resident(w_out.shape)],
        out_specs=pl.BlockSpec((1, tm, d), lambda bi, i: (bi, i, 0)),
        scratch_shapes=[pltpu.VMEM((d // LANES, tm + 2 * HALO_ROWS, LANES), F32),
                        pltpu.VMEM((d // LANES, tm, LANES), F32)],
        compiler_params=_params(2),
        name="conv_dw_pw2_ffn",
    )(u, u, u, x, ada, w_dw, b_dw, ln_g, ln_b, w_pw2, b_pw2, gain2, w_in, w_out)


def _trunk(x, ada, biases, p):
    d = x.shape[-1]
    row = lambda v: v.reshape(1, -1)
    for i in range(DEPTH):
        j = i // N_MIXERS
        gain1, gain2 = row(p["norm1_g"][i]), row(p["norm2_g"][i])
        w_in, w_out = p["ffn_w_in"][i], p["ffn_w_out"][i]
        if i % N_MIXERS == 0:
            w4 = p["attn_w_qkv"][j].reshape(d, 3, N_GROUPS, HEADS_PER_GROUP * HEAD_DIM)
            qkvs = [_qkv_proj(x, ada[i], gain1, w4[:, :, g, :].reshape(d, -1),
                              row(p["attn_q_gain"][j]), row(p["attn_k_gain"][j]), dil)
                    for g, (_, dil) in enumerate(DILATED_GROUPS)]
            o = _attention(qkvs, biases)
            x = _attn_tail(o, x, ada[i], p["attn_w_o"][j], gain2, w_in, w_out)
        else:
            u = _conv_head(x, ada[i], gain1, p["conv_w_pw1"][j], row(p["conv_b_pw1"][j]))
            x = _conv_tail(u, x, ada[i], p["conv_w_dw"][j], row(p["conv_b_dw"][j]),
                           row(p["conv_ln_g"][j]), row(p["conv_ln_b"][j]), p["conv_w_pw2"][j],
                           row(p["conv_b_pw2"][j]), gain2, w_in, w_out)
    return x


def kernel(x_prompt, x_sample, c_prompt, c_sample, rel_bias_table, norm1_g, norm2_g, ada_w, ada_b, attn_w_qkv, attn_q_gain, attn_k_gain, attn_w_o, conv_w_pw1, conv_b_pw1, conv_w_dw, conv_b_dw, conv_ln_g, conv_ln_b, conv_w_pw2, conv_b_pw2, ffn_w_in, ffn_w_out):
    p = dict(norm1_g=norm1_g, norm2_g=norm2_g, attn_q_gain=attn_q_gain, attn_k_gain=attn_k_gain,
             conv_b_pw1=conv_b_pw1, conv_w_dw=conv_w_dw, conv_b_dw=conv_b_dw, conv_ln_g=conv_ln_g,
             conv_ln_b=conv_ln_b, conv_b_pw2=conv_b_pw2,
             attn_w_qkv=attn_w_qkv.astype(BF16), attn_w_o=attn_w_o.astype(BF16),
             conv_w_pw1=conv_w_pw1.astype(BF16), conv_w_pw2=conv_w_pw2.astype(BF16),
             ffn_w_in=ffn_w_in.astype(BF16), ffn_w_out=ffn_w_out.astype(BF16))
    nb_p = c_prompt.shape[0]
    d = x_prompt.shape[-1]
    c_all = jnp.concatenate([c_prompt, c_sample], axis=0)
    ada = _ada_all(c_all, ada_w, ada_b).reshape(DEPTH, c_all.shape[0], 6, d)

    bias_cache = {}

    def biases_for(seq_len):
        out = []
        for g, (_, dil) in enumerate(DILATED_GROUPS):
            key = (g,) + _attn_geometry(seq_len, dil)
            if key not in bias_cache:
                bias_cache[key] = _expand_bias(rel_bias_table, _bias_bucket_index(seq_len, dil), g)
            out.append(bias_cache[key])
        return out

    y_prompt = _trunk(x_prompt, ada[:, :nb_p], biases_for(x_prompt.shape[1]), p)
    y_sample = _trunk(x_sample, ada[:, nb_p:], biases_for(x_sample.shape[1]), p)
    return (y_prompt, y_sample)
```

```python
import functools
import math

import jax
import jax.numpy as jnp
from jax import lax
from jax.experimental import pallas as pl
from jax.experimental.pallas import tpu as pltpu

F32 = jnp.float32
BF16 = jnp.bfloat16

D_MODEL = 1024
DEPTH = 4
N_MIXERS = 2
DILATED_GROUPS = ((128, 1), (512, 4), (2048, 16))
N_GROUPS = len(DILATED_GROUPS)
HEADS_PER_GROUP = 8
HEAD_DIM = 128
REL_BUCKETS = 32
REL_MAX_DIST = 1024
CONV_WIDTH = 31
CONV_PAD = (CONV_WIDTH - 1) // 2
D_FF = 2816
NEG_INF = -1e30
EPS = 1e-6

HALF = DILATED_GROUPS[0][0] // (2 * DILATED_GROUPS[0][1])
assert all(w // (2 * d) == HALF for w, d in DILATED_GROUPS)

TOKEN_TILE = 512
QKV_TOKEN_TILE = 1024
QUERY_BLOCK = 128
ATTN_BLOCKS_PER_STEP = 16
HALO_ROWS = 16
CONV_ROW_CHUNK = 64
SUBLANES = 8
BF16_SUBLANES = 16
LANES = 128
FF_CHUNKS = ((0, 768), (768, 768), (1536, 768), (2304, 512))
VMEM_LIMIT_BYTES = 60000 * 1024


def _params(n_axes):
    return pltpu.CompilerParams(dimension_semantics=("parallel",) * n_axes,
                                vmem_limit_bytes=VMEM_LIMIT_BYTES)


def _resident(shape):
    nd = len(shape)
    return pl.BlockSpec(shape, lambda *_: (0,) * nd, pipeline_mode=pl.Buffered(1))


def _silu(x):
    return x * jax.nn.sigmoid(x)


def _mod_rmsnorm(x, gain, shift, scale):
    ms = jnp.mean(x * x, axis=-1, keepdims=True)
    y = x * lax.rsqrt(ms + EPS) * gain
    return y * (1.0 + scale) + shift


def _ada_kernel(c_ref, w_ref, b_ref, o_ref):
    ca = _silu(c_ref[...]).astype(BF16)
    o_ref[0] = jnp.dot(ca, w_ref[0].astype(BF16), preferred_element_type=F32) + b_ref[0]


def _ada_all(c, ada_w, ada_b):
    nb, d = c.shape
    depth, _, width = ada_w.shape
    tn = 1536
    assert width % tn == 0
    return pl.pallas_call(
        _ada_kernel,
        out_shape=jax.ShapeDtypeStruct((depth, nb, width), F32),
        grid=(depth, width // tn),
        in_specs=[pl.BlockSpec((nb, d), lambda i, j: (0, 0)),
                  pl.BlockSpec((1, d, tn), lambda i, j: (i, 0, j)),
                  pl.BlockSpec((1, 1, tn), lambda i, j: (i, 0, j))],
        out_specs=pl.BlockSpec((1, nb, tn), lambda i, j: (i, 0, j)),
        compiler_params=_params(2),
        name="ada_proj",
    )(c, ada_w, ada_b.reshape(depth, 1, width))


def _t5_bucket(rel):
    half = REL_BUCKETS // 2
    max_exact = half // 2
    ret = jnp.where(rel > 0, half, 0)
    n = jnp.abs(rel)
    nf = jnp.maximum(n, 1).astype(F32)
    large = max_exact + (jnp.log(nf / max_exact) / math.log(REL_MAX_DIST / max_exact)
                         * (half - max_exact)).astype(jnp.int32)
    large = jnp.minimum(large, half - 1)
    return ret + jnp.where(n < max_exact, n, large)


def _attn_geometry(seq_len, dil):
    sub_len = seq_len // dil
    qb = min(QUERY_BLOCK, sub_len)
    kw = min(qb + 2 * HALF, sub_len)
    nblk = sub_len // qb
    assert sub_len * dil == seq_len and nblk * qb == sub_len and qb % HALF == 0
    return sub_len, qb, kw, nblk


def _key_offset(n, qb, kw, sub_len):
    return min(max(n * qb - HALF, 0), sub_len - kw)


def _bias_bucket_index(seq_len, dil):
    sub_len, qb, kw, nblk = _attn_geometry(seq_len, dil)
    slots = []
    for n in (0, min(1, nblk - 1), nblk - 1):
        delta = n * qb - _key_offset(n, qb, kw, sub_len)
        rel = jnp.arange(kw)[None, :] - jnp.arange(qb)[:, None] - delta
        slots.append(jnp.where(jnp.abs(rel) <= HALF, _t5_bucket(rel * dil), -1))
    return jnp.stack(slots).astype(jnp.int32)


def _bias_kernel(tab_ref, idx_ref, o_ref, *, head0):
    h = pl.program_id(0)
    idx = idx_ref[...]
    acc = jnp.full(idx.shape, NEG_INF, F32)
    for b in range(REL_BUCKETS):
        acc = jnp.where(idx == b, tab_ref[b, head0 + h], acc)
    o_ref[0] = acc


def _expand_bias(rel_table, idx, group):
    _, qb, kw = idx.shape
    return pl.pallas_call(
        functools.partial(_bias_kernel, head0=group * HEADS_PER_GROUP),
        out_shape=jax.ShapeDtypeStruct((HEADS_PER_GROUP, 3, qb, kw), F32),
        grid=(HEADS_PER_GROUP,),
        in_specs=[pl.BlockSpec(memory_space=pltpu.SMEM),
                  pl.BlockSpec((3, qb, kw), lambda h: (0, 0, 0))],
        out_specs=pl.BlockSpec((1, 3, qb, kw), lambda h: (h, 0, 0, 0)),
        compiler_params=_params(1),
        name="rel_bias_expand",
    )(rel_table, idx)


def _qkv_kernel(x_ref, ada_ref, g_ref, w_ref, qg_ref, kg_ref, o_ref, *scratch, dil):
    tm, d = x_ref.shape[1], x_ref.shape[2]
    sub = tm // dil
    h = _mod_rmsnorm(x_ref[0], g_ref[...], ada_ref[0, 0:1, :], ada_ref[0, 1:2, :])
    if dil == 1:
        h = h.astype(BF16)
    else:
        slab_ref, perm_ref = scratch
        for c in range(d // LANES):
            slab_ref[c] = h[:, c * LANES:(c + 1) * LANES]
        for c in range(d // LANES):
            for r in range(dil):
                perm_ref[r * sub:(r + 1) * sub, c * LANES:(c + 1) * LANES] = (
                    slab_ref[c, pl.ds(r, sub, stride=dil), :].astype(BF16))
        h = perm_ref[...]
    width = HEADS_PER_GROUP * HEAD_DIM
    for t in range(3):
        acc = jnp.dot(h, w_ref[:, t * width:(t + 1) * width], preferred_element_type=F32)
        for hd in range(HEADS_PER_GROUP):
            c = acc[:, hd * HEAD_DIM:(hd + 1) * HEAD_DIM]
            if t < 2:
                ms = jnp.mean(c * c, axis=-1, keepdims=True)
                c = c * lax.rsqrt(ms + EPS) * (qg_ref[...] if t == 0 else kg_ref[...])
            if t == 0:
                c = c * (HEAD_DIM ** -0.5)
            o_ref[0, t, hd] = c.astype(BF16).reshape(dil, sub, HEAD_DIM)


def _qkv_proj(x, ada, gain, w, q_gain, k_gain, dil):
    b, s, d = x.shape
    sub_len = s // dil
    tm = min(QKV_TOKEN_TILE, s)
    assert s % tm == 0 and tm % (dil * BF16_SUBLANES) == 0
    width = 3 * HEADS_PER_GROUP * HEAD_DIM
    scratch = [] if dil == 1 else [pltpu.VMEM((d // LANES, tm, LANES), F32),
                                   pltpu.VMEM((tm, d), BF16)]
    return pl.pallas_call(
        functools.partial(_qkv_kernel, dil=dil),
        out_shape=jax.ShapeDtypeStruct((b, 3, HEADS_PER_GROUP, dil, sub_len, HEAD_DIM), BF16),
        grid=(b, s // tm),
        in_specs=[pl.BlockSpec((1, tm, d), lambda bi, i: (bi, i, 0)),
                  pl.BlockSpec((1, 6, d), lambda bi, i: (bi, 0, 0)),
                  _resident((1, d)),
                  _resident((d, width)),
                  _resident((1, HEAD_DIM)),
                  _resident((1, HEAD_DIM))],
        out_specs=pl.BlockSpec((1, 3, HEADS_PER_GROUP, dil, tm // dil, HEAD_DIM),
                               lambda bi, i: (bi, 0, 0, 0, i, 0)),
        scratch_shapes=scratch,
        compiler_params=_params(2),
        name=f"qkv_proj_d{dil}",
    )(x, ada, gain, w, q_gain, k_gain)


def _attn_kernel(qkv0_ref, qkv1_ref, qkv2_ref, b0_ref, b1_ref, b2_ref, o_ref, og_ref, mg_ref,
                 lg_ref, *, seq_len):
    for g, (qkv_ref, bias_ref) in enumerate(((qkv0_ref, b0_ref), (qkv1_ref, b1_ref),
                                              (qkv2_ref, b2_ref))):
        dil = DILATED_GROUPS[g][1]
        sub_len, qb, kw, nblk = _attn_geometry(seq_len, dil)

        def block(r, n, g=g, dil=dil, sub_len=sub_len, qb=qb, kw=kw, nblk=nblk,
                  qkv_ref=qkv_ref, bias_ref=bias_ref):
            q0 = pl.multiple_of(r * sub_len + n * qb, qb)
            koff = jnp.clip(n * qb - HALF, 0, sub_len - kw)
            k0 = pl.multiple_of(r * sub_len + koff, HALF)
            q = qkv_ref[0, 0, 0, pl.ds(q0, qb), :]
            k = qkv_ref[0, 1, 0, pl.ds(k0, kw), :]
            v = qkv_ref[0, 2, 0, pl.ds(k0, kw), :]
            slot = jnp.where(n == 0, 0, jnp.where(n == nblk - 1, 2, 1))
            s = lax.dot_general(q, k, (((1,), (1,)), ((), ())), preferred_element_type=F32)
            s = s + bias_ref[0, slot]
            m = jnp.max(s, axis=-1, keepdims=True)
            p = jnp.exp(s - m).astype(BF16)
            v_ones = jnp.concatenate([v, jnp.ones((kw, HEAD_DIM), BF16)], axis=1)
            o_l = jnp.dot(p, v_ones, preferred_element_type=F32)
            start = r + n * (qb * dil)
            rows = pl.ds(start, qb) if dil == 1 else pl.ds(start, qb, stride=dil)
            og_ref[g, rows, :] = o_l[:, :HEAD_DIM]
            mg_ref[g, rows, :] = jnp.broadcast_to(m, (qb, HEAD_DIM))
            lg_ref[g, rows, :] = o_l[:, HEAD_DIM:]

        total = dil * nblk
        par = math.gcd(total, ATTN_BLOCKS_PER_STEP)

        def step(it, carry, block=block, nblk=nblk, par=par):
            for u in range(par):
                idx = it * par + u
                block(lax.div(idx, nblk), lax.rem(idx, nblk))
            return carry

        lax.fori_loop(0, total // par, step, 0)

    rc = 256

    def merge(c, carry):
        rows = pl.ds(pl.multiple_of(c * rc, rc), rc)
        ms = [mg_ref[g, rows, :] for g in range(N_GROUPS)]
        mx = jnp.maximum(jnp.maximum(ms[0], ms[1]), ms[2])
        num = den = None
        for g in range(N_GROUPS):
            w = jnp.exp(ms[g] - mx)
            num = w * og_ref[g, rows, :] if num is None else num + w * og_ref[g, rows, :]
            den = w * lg_ref[g, rows, :] if den is None else den + w * lg_ref[g, rows, :]
        o_ref[0, 0, rows, :] = (num / den).astype(BF16)
        return carry

    lax.fori_loop(0, seq_len // rc, merge, 0)


def _attention(qkvs, biases):
    b = qkvs[0].shape[0]
    s = qkvs[0].shape[3] * qkvs[0].shape[4]
    qkvs = [t.reshape(b, 3, HEADS_PER_GROUP, s, HEAD_DIM) for t in qkvs]
    qkv_spec = pl.BlockSpec((1, 3, 1, s, HEAD_DIM), lambda bi, h: (bi, 0, h, 0, 0))
    bias_specs = [pl.BlockSpec((1,) + t.shape[1:], lambda bi, h: (h, 0, 0, 0)) for t in biases]
    return pl.pallas_call(
        functools.partial(_attn_kernel, seq_len=s),
        out_shape=jax.ShapeDtypeStruct((b, HEADS_PER_GROUP, s, HEAD_DIM), BF16),
        grid=(b, HEADS_PER_GROUP),
        in_specs=[qkv_spec] * N_GROUPS + bias_specs,
        out_specs=pl.BlockSpec((1, 1, s, HEAD_DIM), lambda bi, h: (bi, h, 0, 0)),
        scratch_shapes=[pltpu.VMEM((N_GROUPS, s, HEAD_DIM), F32)] * 3,
        compiler_params=_params(2),
        name="dilated_attention",
    )(*qkvs, *biases)


def _residual_ffn(x, y, ada_ref, g2_ref, w_in_ref, w_out_ref):
    x1 = x + ada_ref[0, 2:3, :] * y
    h = _mod_rmsnorm(x1, g2_ref[...], ada_ref[0, 3:4, :], ada_ref[0, 4:5, :]).astype(BF16)
    acc = None
    for c0, cw in FF_CHUNKS:
        gate = jnp.dot(h, w_in_ref[:, c0:c0 + cw], preferred_element_type=F32)
        up = jnp.dot(h, w_in_ref[:, D_FF + c0:D_FF + c0 + cw], preferred_element_type=F32)
        a = (_silu(gate) * up).astype(BF16)
        part = jnp.dot(a, w_out_ref[c0:c0 + cw, :], preferred_element_type=F32)
        acc = part if acc is None else acc + part
    return x1 + ada_ref[0, 5:6, :] * acc


def _glu_head(x, ada_ref, g_ref, w_ref, b_ref):
    d = x.shape[-1]
    h = _mod_rmsnorm(x, g_ref[...], ada_ref[0, 0:1, :], ada_ref[0, 1:2, :]).astype(BF16)
    a = jnp.dot(h, w_ref[:, :d], preferred_element_type=F32) + b_ref[:, :d]
    gt = jnp.dot(h, w_ref[:, d:], preferred_element_type=F32) + b_ref[:, d:]
    return a * jax.nn.sigmoid(gt)


def _attn_tail_kernel(o_ref, x_ref, ada_ref, wo_ref, g2_ref, w_in_ref, w_out_ref, out_ref):
    o = jnp.concatenate([o_ref[0, h] for h in range(HEADS_PER_GROUP)], axis=-1)
    y = jnp.dot(o, wo_ref[...], preferred_element_type=F32)
    out_ref[0] = _residual_ffn(x_ref[0], y, ada_ref, g2_ref, w_in_ref, w_out_ref)


def _attn_tail_head_kernel(o_ref, x_ref, ada_ref, wo_ref, g2_ref, w_in_ref, w_out_ref,
                           ada_n_ref, g1n_ref, w1_ref, b1_ref, out_ref, u_ref):
    o = jnp.concatenate([o_ref[0, h] for h in range(HEADS_PER_GROUP)], axis=-1)
    y = jnp.dot(o, wo_ref[...], preferred_element_type=F32)
    x_new = _residual_ffn(x_ref[0], y, ada_ref, g2_ref, w_in_ref, w_out_ref)
    out_ref[0] = x_new
    u_ref[0] = _glu_head(x_new, ada_n_ref, g1n_ref, w1_ref, b1_ref)


def _attn_tail(o, x, ada, w_o, gain2, w_in, w_out, next_head=None):
    b, s, d = x.shape
    tm = min(TOKEN_TILE, s)
    tile = pl.BlockSpec((1, tm, d), lambda bi, i: (bi, i, 0))
    ada_spec = pl.BlockSpec((1, 6, d), lambda bi, i: (bi, 0, 0))
    in_specs = [pl.BlockSpec((1, HEADS_PER_GROUP, tm, HEAD_DIM), lambda bi, i: (bi, 0, i, 0)),
                tile, ada_spec, _resident(w_o.shape), _resident((1, d)),
                _resident(w_in.shape), _resident(w_out.shape)]
    args = [o, x, ada, w_o, gain2, w_in, w_out]
    x_shape = jax.ShapeDtypeStruct((b, s, d), F32)
    if next_head is None:
        return pl.pallas_call(
            _attn_tail_kernel, out_shape=x_shape, grid=(b, s // tm), in_specs=in_specs,
            out_specs=tile, compiler_params=_params(2), name="attn_out_ffn")(*args)
    ada_n, gain1_n, w_pw1, b_pw1 = next_head
    return pl.pallas_call(
        _attn_tail_head_kernel, out_shape=(x_shape, x_shape), grid=(b, s // tm),
        in_specs=in_specs + [ada_spec, _resident((1, d)), _resident(w_pw1.shape),
                             _resident((1, 2 * d))],
        out_specs=(tile, tile), compiler_params=_params(2), name="attn_out_ffn_pw1",
    )(*args, ada_n, gain1_n, w_pw1, b_pw1)


def _conv_head_kernel(x_ref, ada_ref, g_ref, w_ref, b_ref, u_ref):
    u_ref[0] = _glu_head(x_ref[0], ada_ref, g_ref, w_ref, b_ref)


def _conv_head(x, ada, gain, w_pw1, b_pw1):
    b, s, d = x.shape
    tm = min(TOKEN_TILE, s)
    return pl.pallas_call(
        _conv_head_kernel,
        out_shape=jax.ShapeDtypeStruct((b, s, d), F32),
        grid=(b, s // tm),
        in_specs=[pl.BlockSpec((1, tm, d), lambda bi, i: (bi, i, 0)),
                  pl.BlockSpec((1, 6, d), lambda bi, i: (bi, 0, 0)),
                  _resident((1, d)), _resident(w_pw1.shape), _resident((1, 2 * d))],
        out_specs=pl.BlockSpec((1, tm, d), lambda bi, i: (bi, i, 0)),
        compiler_params=_params(2),
        name="conv_pw1_glu",
    )(x, ada, gain, w_pw1, b_pw1)


def _conv_tail_kernel(prev_ref, u_ref, next_ref, x_ref, ada_ref, wdw_ref, bdw_ref, lng_ref,
                      lnb_ref, w2_ref, b2_ref, g2_ref, w_in_ref, w_out_ref, out_ref,
                      win_ref, cv_ref, *, row_chunk, unroll):
    tm, d = u_ref.shape[1], u_ref.shape[2]
    i = pl.program_id(1)
    last = pl.num_programs(1) - 1
    prev = jnp.where(i > 0, prev_ref[0], 0.0)
    nxt = jnp.where(i < last, next_ref[0], 0.0)
    for c in range(d // LANES):
        lanes = slice(c * LANES, (c + 1) * LANES)
        win_ref[c, 0:HALO_ROWS, :] = prev[:, lanes]
        win_ref[c, HALO_ROWS:HALO_ROWS + tm, :] = u_ref[0, :, lanes]
        win_ref[c, HALO_ROWS + tm:, :] = nxt[:, lanes]

    for c in range(d // LANES):
        lanes = slice(c * LANES, (c + 1) * LANES)

        def conv_rows(j, carry, c=c, lanes=lanes):
            r0 = j * (2 * row_chunk)
            for parity in range(2):
                acc = None
                for k in range(CONV_WIDTH):
                    start = r0 + (parity + HALO_ROWS - CONV_PAD + k)
                    tap = win_ref[c, pl.ds(start, row_chunk, stride=2), :] * wdw_ref[k:k + 1, lanes]
                    acc = tap if acc is None else acc + tap
                cv_ref[c, pl.ds(r0 + parity, row_chunk, stride=2), :] = acc
            return carry

        lax.fori_loop(0, tm // (2 * row_chunk), conv_rows, 0, unroll=unroll)

    cv = jnp.concatenate([cv_ref[c] for c in range(d // LANES)], axis=1) + bdw_ref[...]
    mu = jnp.mean(cv, axis=-1, keepdims=True)
    xc = cv - mu
    var = jnp.mean(xc * xc, axis=-1, keepdims=True)
    ln = xc * lax.rsqrt(var + EPS) * lng_ref[...] + lnb_ref[...]
    y = jnp.dot(_silu(ln).astype(BF16), w2_ref[...], preferred_element_type=F32) + b2_ref[...]
    out_ref[0] = _residual_ffn(x_ref[0], y, ada_ref, g2_ref, w_in_ref, w_out_ref)


def _conv_tail(u, x, ada, w_dw, b_dw, ln_g, ln_b, w_pw2, b_pw2, gain2, w_in, w_out):
    b, s, d = x.shape
    tm = min(TOKEN_TILE, s)
    row_chunk = CONV_ROW_CHUNK
    unroll = 1 if s == 4096 else 2
    assert s % tm == 0 and tm % (2 * row_chunk) == 0 and HALO_ROWS >= CONV_PAD + 1
    hb = tm // HALO_ROWS
    n_halo = s // HALO_ROWS
    return pl.pallas_call(
        functools.partial(_conv_tail_kernel, row_chunk=row_chunk, unroll=unroll),
        out_shape=jax.ShapeDtypeStruct((b, s, d), F32),
        grid=(b, s // tm),
        in_specs=[pl.BlockSpec((1, HALO_ROWS, d), lambda bi, i: (bi, jnp.maximum(i * hb - 1, 0), 0)),
                  pl.BlockSpec((1, tm, d), lambda bi, i: (bi, i, 0)),
                  pl.BlockSpec((1, HALO_ROWS, d),
                               lambda bi, i: (bi, jnp.minimum((i + 1) * hb, n_halo - 1), 0)),
                  pl.BlockSpec((1, tm, d), lambda bi, i: (bi, i, 0)),
                  pl.BlockSpec((1, 6, d), lambda bi, i: (bi, 0, 0)),
                  _resident(w_dw.shape), _resident((1, d)), _resident((1, d)), _resident((1, d)),
                  _resident(w_pw2.shape), _resident((1, d)), _resident((1, d)),
                  _resident(w_in.shape), _resident(w_out.shape)],
        out_specs=pl.BlockSpec((1, tm, d), lambda bi, i: (bi, i, 0)),
        scratch_shapes=[pltpu.VMEM((d // LANES, tm + 2 * HALO_ROWS, LANES), F32),
                        pltpu.VMEM((d // LANES, tm, LANES), F32)],
        compiler_params=_params(2),
        name="conv_dw_pw2_ffn",
    )(u, u, u, x, ada, w_dw, b_dw, ln_g, ln_b, w_pw2, b_pw2, gain2, w_in, w_out)


def _trunk(x, ada, biases, p):
    d = x.shape[-1]
    row = lambda v: v.reshape(1, -1)
    fuse_head = x.shape[1] == 4096
    u = None
    for i in range(DEPTH):
        j = i // N_MIXERS
        gain1, gain2 = row(p["norm1_g"][i]), row(p["norm2_g"][i])
        w_in, w_out = p["ffn_w_in"][i], p["ffn_w_out"][i]
        if i % N_MIXERS == 0:
            w4 = p["attn_w_qkv"][j].reshape(d, 3, N_GROUPS, HEADS_PER_GROUP * HEAD_DIM)
            qkvs = [_qkv_proj(x, ada[i], gain1, w4[:, :, g, :].reshape(d, -1),
                              row(p["attn_q_gain"][j]), row(p["attn_k_gain"][j]), dil)
                    for g, (_, dil) in enumerate(DILATED_GROUPS)]
            o = _attention(qkvs, biases)
            if fuse_head and i + 1 < DEPTH and (i + 1) % N_MIXERS == 1:
                jn = (i + 1) // N_MIXERS
                x, u = _attn_tail(o, x, ada[i], p["attn_w_o"][j], gain2, w_in, w_out,
                                  next_head=(ada[i + 1], row(p["norm1_g"][i + 1]),
                                             p["conv_w_pw1"][jn], row(p["conv_b_pw1"][jn])))
            else:
                x = _attn_tail(o, x, ada[i], p["attn_w_o"][j], gain2, w_in, w_out)
        else:
            if u is None:
                u = _conv_head(x, ada[i], gain1, p["conv_w_pw1"][j], row(p["conv_b_pw1"][j]))
            x = _conv_tail(u, x, ada[i], p["conv_w_dw"][j], row(p["conv_b_dw"][j]),
                           row(p["conv_ln_g"][j]), row(p["conv_ln_b"][j]), p["conv_w_pw2"][j],
                           row(p["conv_b_pw2"][j]), gain2, w_in, w_out)
            u = None
    return x


def kernel(x_prompt, x_sample, c_prompt, c_sample, rel_bias_table, norm1_g, norm2_g, ada_w, ada_b, attn_w_qkv, attn_q_gain, attn_k_gain, attn_w_o, conv_w_pw1, conv_b_pw1, conv_w_dw, conv_b_dw, conv_ln_g, conv_ln_b, conv_w_pw2, conv_b_pw2, ffn_w_in, ffn_w_out):
    p = dict(norm1_g=norm1_g, norm2_g=norm2_g, attn_q_gain=attn_q_gain, attn_k_gain=attn_k_gain,
             conv_b_pw1=conv_b_pw1, conv_w_dw=conv_w_dw, conv_b_dw=conv_b_dw, conv_ln_g=conv_ln_g,
             conv_ln_b=conv_ln_b, conv_b_pw2=conv_b_pw2,
             attn_w_qkv=attn_w_qkv.astype(BF16), attn_w_o=attn_w_o.astype(BF16),
             conv_w_pw1=conv_w_pw1.astype(BF16), conv_w_pw2=conv_w_pw2.astype(BF16),
             ffn_w_in=ffn_w_in.astype(BF16), ffn_w_out=ffn_w_out.astype(BF16))
    nb_p = c_prompt.shape[0]
    d = x_prompt.shape[-1]
    c_all = jnp.concatenate([c_prompt, c_sample], axis=0)
    ada = _ada_all(c_all, ada_w, ada_b).reshape(DEPTH, c_all.shape[0], 6, d)

    bias_cache = {}

    def biases_for(seq_len):
        out = []
        for g, (_, dil) in enumerate(DILATED_GROUPS):
            key = (g,) + _attn_geometry(seq_len, dil)
            if key not in bias_cache:
                bias_cache[key] = _expand_bias(rel_bias_table, _bias_bucket_index(seq_len, dil), g)
            out.append(bias_cache[key])
        return out

    y_prompt = _trunk(x_prompt, ada[:, :nb_p], biases_for(x_prompt.shape[1]), p)
    y_sample = _trunk(x_sample, ada[:, nb_p:], biases_for(x_sample.shape[1]), p)
    return (y_prompt, y_sample)
```

```python
import functools
import math

import jax
import jax.numpy as jnp
from jax import lax
from jax.experimental import pallas as pl
from jax.experimental.pallas import tpu as pltpu

F32 = jnp.float32
BF16 = jnp.bfloat16

D_MODEL = 1024
DEPTH = 4
N_MIXERS = 2
DILATED_GROUPS = ((128, 1), (512, 4), (2048, 16))
N_GROUPS = len(DILATED_GROUPS)
HEADS_PER_GROUP = 8
HEAD_DIM = 128
REL_BUCKETS = 32
REL_MAX_DIST = 1024
CONV_WIDTH = 31
CONV_PAD = (CONV_WIDTH - 1) // 2
D_FF = 2816
NEG_INF = -1e30
EPS = 1e-6

HALF = DILATED_GROUPS[0][0] // (2 * DILATED_GROUPS[0][1])
assert all(w // (2 * d) == HALF for w, d in DILATED_GROUPS)

TOKEN_TILE = 512
QKV_TOKEN_TILE = 1024
QUERY_BLOCK = 128
ATTN_BLOCKS_PER_STEP = 32
HALO_ROWS = 16
CONV_ROW_CHUNK = 64
SUBLANES = 8
BF16_SUBLANES = 16
LANES = 128
FF_CHUNKS = ((0, 768), (768, 768), (1536, 768), (2304, 512))
VMEM_LIMIT_BYTES = 60000 * 1024


def _params(n_axes):
    return pltpu.CompilerParams(dimension_semantics=("parallel",) * n_axes,
                                vmem_limit_bytes=VMEM_LIMIT_BYTES)


def _resident(shape):
    nd = len(shape)
    return pl.BlockSpec(shape, lambda *_: (0,) * nd, pipeline_mode=pl.Buffered(1))


def _silu(x):
    return x * jax.nn.sigmoid(x)


def _mod_rmsnorm(x, gain, shift, scale):
    ms = jnp.mean(x * x, axis=-1, keepdims=True)
    y = x * lax.rsqrt(ms + EPS) * gain
    return y * (1.0 + scale) + shift


def _ada_kernel(c_ref, w_ref, b_ref, o_ref):
    ca = _silu(c_ref[...]).astype(BF16)
    o_ref[0] = jnp.dot(ca, w_ref[0].astype(BF16), preferred_element_type=F32) + b_ref[0]


def _ada_all(c, ada_w, ada_b):
    nb, d = c.shape
    depth, _, width = ada_w.shape
    tn = 1536
    assert width % tn == 0
    return pl.pallas_call(
        _ada_kernel,
        out_shape=jax.ShapeDtypeStruct((depth, nb, width), F32),
        grid=(depth, width // tn),
        in_specs=[pl.BlockSpec((nb, d), lambda i, j: (0, 0)),
                  pl.BlockSpec((1, d, tn), lambda i, j: (i, 0, j)),
                  pl.BlockSpec((1, 1, tn), lambda i, j: (i, 0, j))],
        out_specs=pl.BlockSpec((1, nb, tn), lambda i, j: (i, 0, j)),
        compiler_params=_params(2),
        name="ada_proj",
    )(c, ada_w, ada_b.reshape(depth, 1, width))


def _t5_bucket(rel):
    half = REL_BUCKETS // 2
    max_exact = half // 2
    ret = jnp.where(rel > 0, half, 0)
    n = jnp.abs(rel)
    nf = jnp.maximum(n, 1).astype(F32)
    large = max_exact + (jnp.log(nf / max_exact) / math.log(REL_MAX_DIST / max_exact)
                         * (half - max_exact)).astype(jnp.int32)
    large = jnp.minimum(large, half - 1)
    return ret + jnp.where(n < max_exact, n, large)


def _attn_geometry(seq_len, dil):
    sub_len = seq_len // dil
    qb = min(QUERY_BLOCK, sub_len)
    kw = min(qb + 2 * HALF, sub_len)
    nblk = sub_len // qb
    assert sub_len * dil == seq_len and nblk * qb == sub_len and qb % HALF == 0
    return sub_len, qb, kw, nblk


def _key_offset(n, qb, kw, sub_len):
    return min(max(n * qb - HALF, 0), sub_len - kw)


def _bias_bucket_index(seq_len, dil):
    sub_len, qb, kw, nblk = _attn_geometry(seq_len, dil)
    slots = []
    for n in (0, min(1, nblk - 1), nblk - 1):
        delta = n * qb - _key_offset(n, qb, kw, sub_len)
        rel = jnp.arange(kw)[None, :] - jnp.arange(qb)[:, None] - delta
        slots.append(jnp.where(jnp.abs(rel) <= HALF, _t5_bucket(rel * dil), -1))
    return jnp.stack(slots).astype(jnp.int32)


def _bias_kernel(tab_ref, idx_ref, o_ref, *, head0):
    h = pl.program_id(0)
    idx = idx_ref[...]
    acc = jnp.full(idx.shape, NEG_INF, F32)
    for b in range(REL_BUCKETS):
        acc = jnp.where(idx == b, tab_ref[b, head0 + h], acc)
    o_ref[0] = acc


def _expand_bias(rel_table, idx, group):
    _, qb, kw = idx.shape
    return pl.pallas_call(
        functools.partial(_bias_kernel, head0=group * HEADS_PER_GROUP),
        out_shape=jax.ShapeDtypeStruct((HEADS_PER_GROUP, 3, qb, kw), F32),
        grid=(HEADS_PER_GROUP,),
        in_specs=[pl.BlockSpec(memory_space=pltpu.SMEM),
                  pl.BlockSpec((3, qb, kw), lambda h: (0, 0, 0))],
        out_specs=pl.BlockSpec((1, 3, qb, kw), lambda h: (h, 0, 0, 0)),
        compiler_params=_params(1),
        name="rel_bias_expand",
    )(rel_table, idx)


def _qkv_kernel(x_ref, ada_ref, g_ref, w_ref, qg_ref, kg_ref, o_ref, *scratch, dil, two_level):
    tm, d = x_ref.shape[1], x_ref.shape[2]
    sub = tm // dil
    h = _mod_rmsnorm(x_ref[0], g_ref[...], ada_ref[0, 0:1, :], ada_ref[0, 1:2, :])
    if dil == 1:
        h = h.astype(BF16)
    elif two_level:
        step = math.isqrt(dil)
        slab_ref, perm_ref, mid_ref = scratch
        part = tm // step
        for c in range(d // LANES):
            slab_ref[c] = h[:, c * LANES:(c + 1) * LANES]
        for c in range(d // LANES):
            for r0 in range(step):
                mid_ref[c, r0 * part:(r0 + 1) * part, :] = slab_ref[c, pl.ds(r0, part, stride=step), :]
        for c in range(d // LANES):
            for r in range(dil):
                r0, r1 = r % step, r // step
                perm_ref[r * sub:(r + 1) * sub, c * LANES:(c + 1) * LANES] = (
                    mid_ref[c, pl.ds(r0 * part + r1, sub, stride=step), :].astype(BF16))
        h = perm_ref[...]
    else:
        slab_ref, perm_ref = scratch
        for c in range(d // LANES):
            slab_ref[c] = h[:, c * LANES:(c + 1) * LANES]
        for c in range(d // LANES):
            for r in range(dil):
                perm_ref[r * sub:(r + 1) * sub, c * LANES:(c + 1) * LANES] = (
                    slab_ref[c, pl.ds(r, sub, stride=dil), :].astype(BF16))
        h = perm_ref[...]
    width = HEADS_PER_GROUP * HEAD_DIM
    for t in range(3):
        acc = jnp.dot(h, w_ref[:, t * width:(t + 1) * width], preferred_element_type=F32)
        for hd in range(HEADS_PER_GROUP):
            c = acc[:, hd * HEAD_DIM:(hd + 1) * HEAD_DIM]
            if t < 2:
                ms = jnp.mean(c * c, axis=-1, keepdims=True)
                c = c * lax.rsqrt(ms + EPS) * (qg_ref[...] if t == 0 else kg_ref[...])
            if t == 0:
                c = c * (HEAD_DIM ** -0.5)
            o_ref[0, t, hd] = c.astype(BF16).reshape(dil, sub, HEAD_DIM)


def _qkv_proj(x, ada, gain, w, q_gain, k_gain, dil):
    b, s, d = x.shape
    sub_len = s // dil
    tm = min(QKV_TOKEN_TILE, s)
    assert s % tm == 0 and tm % (dil * BF16_SUBLANES) == 0
    width = 3 * HEADS_PER_GROUP * HEAD_DIM
    scratch = [] if dil == 1 else [pltpu.VMEM((d // LANES, tm, LANES), F32),
                                   pltpu.VMEM((tm, d), BF16)]
    two_level = dil == 16 and s == 4096
    if two_level:
        scratch.append(pltpu.VMEM((d // LANES, tm, LANES), F32))
    return pl.pallas_call(
        functools.partial(_qkv_kernel, dil=dil, two_level=two_level),
        out_shape=jax.ShapeDtypeStruct((b, 3, HEADS_PER_GROUP, dil, sub_len, HEAD_DIM), BF16),
        grid=(b, s // tm),
        in_specs=[pl.BlockSpec((1, tm, d), lambda bi, i: (bi, i, 0)),
                  pl.BlockSpec((1, 6, d), lambda bi, i: (bi, 0, 0)),
                  _resident((1, d)),
                  _resident((d, width)),
                  _resident((1, HEAD_DIM)),
                  _resident((1, HEAD_DIM))],
        out_specs=pl.BlockSpec((1, 3, HEADS_PER_GROUP, dil, tm // dil, HEAD_DIM),
                               lambda bi, i: (bi, 0, 0, 0, i, 0)),
        scratch_shapes=scratch,
        compiler_params=_params(2),
        name=f"qkv_proj_d{dil}",
    )(x, ada, gain, w, q_gain, k_gain)


def _attn_kernel(qkv0_ref, qkv1_ref, qkv2_ref, b0_ref, b1_ref, b2_ref, o_ref, og_ref, mg_ref,
                 lg_ref, *, seq_len):
    for g, (qkv_ref, bias_ref) in enumerate(((qkv0_ref, b0_ref), (qkv1_ref, b1_ref),
                                              (qkv2_ref, b2_ref))):
        dil = DILATED_GROUPS[g][1]
        sub_len, qb, kw, nblk = _attn_geometry(seq_len, dil)

        def block(r, n, g=g, dil=dil, sub_len=sub_len, qb=qb, kw=kw, nblk=nblk,
                  qkv_ref=qkv_ref, bias_ref=bias_ref):
            q0 = pl.multiple_of(r * sub_len + n * qb, qb)
            koff = jnp.clip(n * qb - HALF, 0, sub_len - kw)
            k0 = pl.multiple_of(r * sub_len + koff, HALF)
            q = qkv_ref[0, 0, 0, pl.ds(q0, qb), :]
            k = qkv_ref[0, 1, 0, pl.ds(k0, kw), :]
            v = qkv_ref[0, 2, 0, pl.ds(k0, kw), :]
            slot = jnp.where(n == 0, 0, jnp.where(n == nblk - 1, 2, 1))
            s = lax.dot_general(q, k, (((1,), (1,)), ((), ())), preferred_element_type=F32)
            s = s + bias_ref[0, slot]
            m = jnp.max(s, axis=-1, keepdims=True)
            p = jnp.exp(s - m).astype(BF16)
            v_ones = jnp.concatenate([v, jnp.ones((kw, HEAD_DIM), BF16)], axis=1)
            o_l = jnp.dot(p, v_ones, preferred_element_type=F32)
            start = r + n * (qb * dil)
            rows = pl.ds(start, qb) if dil == 1 else pl.ds(start, qb, stride=dil)
            og_ref[g, rows, :] = o_l[:, :HEAD_DIM]
            mg_ref[g, rows, :] = jnp.broadcast_to(m, (qb, HEAD_DIM))
            lg_ref[g, rows, :] = o_l[:, HEAD_DIM:]

        total = dil * nblk
        par = math.gcd(total, ATTN_BLOCKS_PER_STEP)

        def step(it, carry, block=block, nblk=nblk, par=par):
            for u in range(par):
                idx = it * par + u
                block(lax.div(idx, nblk), lax.rem(idx, nblk))
            return carry

        lax.fori_loop(0, total // par, step, 0)

    rc = 256

    def merge(c, carry):
        rows = pl.ds(pl.multiple_of(c * rc, rc), rc)
        ms = [mg_ref[g, rows, :] for g in range(N_GROUPS)]
        mx = jnp.maximum(jnp.maximum(ms[0], ms[1]), ms[2])
        num = den = None
        for g in range(N_GROUPS):
            w = jnp.exp(ms[g] - mx)
            num = w * og_ref[g, rows, :] if num is None else num + w * og_ref[g, rows, :]
            den = w * lg_ref[g, rows, :] if den is None else den + w * lg_ref[g, rows, :]
        o_ref[0, 0, rows, :] = (num / den).astype(BF16)
        return carry

    lax.fori_loop(0, seq_len // rc, merge, 0)


def _attention(qkvs, biases):
    b = qkvs[0].shape[0]
    s = qkvs[0].shape[3] * qkvs[0].shape[4]
    qkvs = [t.reshape(b, 3, HEADS_PER_GROUP, s, HEAD_DIM) for t in qkvs]
    qkv_spec = pl.BlockSpec((1, 3, 1, s, HEAD_DIM), lambda bi, h: (bi, 0, h, 0, 0))
    bias_specs = [pl.BlockSpec((1,) + t.shape[1:], lambda bi, h: (h, 0, 0, 0)) for t in biases]
    return pl.pallas_call(
        functools.partial(_attn_kernel, seq_len=s),
        out_shape=jax.ShapeDtypeStruct((b, HEADS_PER_GROUP, s, HEAD_DIM), BF16),
        grid=(b, HEADS_PER_GROUP),
        in_specs=[qkv_spec] * N_GROUPS + bias_specs,
        out_specs=pl.BlockSpec((1, 1, s, HEAD_DIM), lambda bi, h: (bi, h, 0, 0)),
        scratch_shapes=[pltpu.VMEM((N_GROUPS, s, HEAD_DIM), F32)] * 3,
        compiler_params=_params(2),
        name="dilated_attention",
    )(*qkvs, *biases)


def _residual_ffn(x, y, ada_ref, g2_ref, w_in_ref, w_out_ref):
    x1 = x + ada_ref[0, 2:3, :] * y
    h = _mod_rmsnorm(x1, g2_ref[...], ada_ref[0, 3:4, :], ada_ref[0, 4:5, :]).astype(BF16)
    acc = None
    for c0, cw in FF_CHUNKS:
        gate = jnp.dot(h, w_in_ref[:, c0:c0 + cw], preferred_element_type=F32)
        up = jnp.dot(h, w_in_ref[:, D_FF + c0:D_FF + c0 + cw], preferred_element_type=F32)
        a = (_silu(gate) * up).astype(BF16)
        part = jnp.dot(a, w_out_ref[c0:c0 + cw, :], preferred_element_type=F32)
        acc = part if acc is None else acc + part
    return x1 + ada_ref[0, 5:6, :] * acc


def _glu_head(x, ada_ref, g_ref, w_ref, b_ref):
    d = x.shape[-1]
    h = _mod_rmsnorm(x, g_ref[...], ada_ref[0, 0:1, :], ada_ref[0, 1:2, :]).astype(BF16)
    a = jnp.dot(h, w_ref[:, :d], preferred_element_type=F32) + b_ref[:, :d]
    gt = jnp.dot(h, w_ref[:, d:], preferred_element_type=F32) + b_ref[:, d:]
    return a * jax.nn.sigmoid(gt)


def _attn_tail_kernel(o_ref, x_ref, ada_ref, wo_ref, g2_ref, w_in_ref, w_out_ref, out_ref):
    o = jnp.concatenate([o_ref[0, h] for h in range(HEADS_PER_GROUP)], axis=-1)
    y = jnp.dot(o, wo_ref[...], preferred_element_type=F32)
    out_ref[0] = _residual_ffn(x_ref[0], y, ada_ref, g2_ref, w_in_ref, w_out_ref)


def _attn_tail_head_kernel(o_ref, x_ref, ada_ref, wo_ref, g2_ref, w_in_ref, w_out_ref,
                           ada_n_ref, g1n_ref, w1_ref, b1_ref, out_ref, u_ref):
    o = jnp.concatenate([o_ref[0, h] for h in range(HEADS_PER_GROUP)], axis=-1)
    y = jnp.dot(o, wo_ref[...], preferred_element_type=F32)
    x_new = _residual_ffn(x_ref[0], y, ada_ref, g2_ref, w_in_ref, w_out_ref)
    out_ref[0] = x_new
    u_ref[0] = _glu_head(x_new, ada_n_ref, g1n_ref, w1_ref, b1_ref)


def _attn_tail(o, x, ada, w_o, gain2, w_in, w_out, next_head=None):
    b, s, d = x.shape
    tm = min(TOKEN_TILE, s)
    tile = pl.BlockSpec((1, tm, d), lambda bi, i: (bi, i, 0))
    ada_spec = pl.BlockSpec((1, 6, d), lambda bi, i: (bi, 0, 0))
    in_specs = [pl.BlockSpec((1, HEADS_PER_GROUP, tm, HEAD_DIM), lambda bi, i: (bi, 0, i, 0)),
                tile, ada_spec, _resident(w_o.shape), _resident((1, d)),
                _resident(w_in.shape), _resident(w_out.shape)]
    args = [o, x, ada, w_o, gain2, w_in, w_out]
    x_shape = jax.ShapeDtypeStruct((b, s, d), F32)
    if next_head is None:
        return pl.pallas_call(
            _attn_tail_kernel, out_shape=x_shape, grid=(b, s // tm), in_specs=in_specs,
            out_specs=tile, compiler_params=_params(2), name="attn_out_ffn")(*args)
    ada_n, gain1_n, w_pw1, b_pw1 = next_head
    return pl.pallas_call(
        _attn_tail_head_kernel, out_shape=(x_shape, x_shape), grid=(b, s // tm),
        in_specs=in_specs + [ada_spec, _resident((1, d)), _resident(w_pw1.shape),
                             _resident((1, 2 * d))],
        out_specs=(tile, tile), compiler_params=_params(2), name="attn_out_ffn_pw1",
    )(*args, ada_n, gain1_n, w_pw1, b_pw1)


def _conv_head_kernel(x_ref, ada_ref, g_ref, w_ref, b_ref, u_ref):
    u_ref[0] = _glu_head(x_ref[0], ada_ref, g_ref, w_ref, b_ref)


def _conv_head(x, ada, gain, w_pw1, b_pw1):
    b, s, d = x.shape
    tm = min(TOKEN_TILE, s)
    return pl.pallas_call(
        _conv_head_kernel,
        out_shape=jax.ShapeDtypeStruct((b, s, d), F32),
        grid=(b, s // tm),
        in_specs=[pl.BlockSpec((1, tm, d), lambda bi, i: (bi, i, 0)),
                  pl.BlockSpec((1, 6, d), lambda bi, i: (bi, 0, 0)),
                  _resident((1, d)), _resident(w_pw1.shape), _resident((1, 2 * d))],
        out_specs=pl.BlockSpec((1, tm, d), lambda bi, i: (bi, i, 0)),
        compiler_params=_params(2),
        name="conv_pw1_glu",
    )(x, ada, gain, w_pw1, b_pw1)


def _conv_tail_kernel(prev_ref, u_ref, next_ref, x_ref, ada_ref, wdw_ref, bdw_ref, lng_ref,
                      lnb_ref, w2_ref, b2_ref, g2_ref, w_in_ref, w_out_ref, out_ref,
                      win_ref, cv_ref, *, row_chunk, unroll):
    tm, d = u_ref.shape[1], u_ref.shape[2]
    i = pl.program_id(1)
    last = pl.num_programs(1) - 1
    prev = jnp.where(i > 0, prev_ref[0], 0.0)
    nxt = jnp.where(i < last, next_ref[0], 0.0)
    for c in range(d // LANES):
        lanes = slice(c * LANES, (c + 1) * LANES)
        win_ref[c, 0:HALO_ROWS, :] = prev[:, lanes]
        win_ref[c, HALO_ROWS:HALO_ROWS + tm, :] = u_ref[0, :, lanes]
        win_ref[c, HALO_ROWS + tm:, :] = nxt[:, lanes]

    for c in range(d // LANES):
        lanes = slice(c * LANES, (c + 1) * LANES)

        def conv_rows(j, carry, c=c, lanes=lanes):
            r0 = j * (2 * row_chunk)
            for parity in range(2):
                acc = None
                for k in range(CONV_WIDTH):
                    start = r0 + (parity + HALO_ROWS - CONV_PAD + k)
                    tap = win_ref[c, pl.ds(start, row_chunk, stride=2), :] * wdw_ref[k:k + 1, lanes]
                    acc = tap if acc is None else acc + tap
                cv_ref[c, pl.ds(r0 + parity, row_chunk, stride=2), :] = acc
            return carry

        lax.fori_loop(0, tm // (2 * row_chunk), conv_rows, 0, unroll=unroll)

    cv = jnp.concatenate([cv_ref[c] for c in range(d // LANES)], axis=1) + bdw_ref[...]
    mu = jnp.mean(cv, axis=-1, keepdims=True)
    xc = cv - mu
    var = jnp.mean(xc * xc, axis=-1, keepdims=True)
    ln = xc * lax.rsqrt(var + EPS) * lng_ref[...] + lnb_ref[...]
    y = jnp.dot(_silu(ln).astype(BF16), w2_ref[...], preferred_element_type=F32) + b2_ref[...]
    out_ref[0] = _residual_ffn(x_ref[0], y, ada_ref, g2_ref, w_in_ref, w_out_ref)


def _conv_tail(u, x, ada, w_dw, b_dw, ln_g, ln_b, w_pw2, b_pw2, gain2, w_in, w_out):
    b, s, d = x.shape
    tm = min(TOKEN_TILE, s)
    row_chunk = CONV_ROW_CHUNK
    unroll = 4 if s == 4096 else 2
    assert s % tm == 0 and tm % (2 * row_chunk) == 0 and HALO_ROWS >= CONV_PAD + 1
    hb = tm // HALO_ROWS
    n_halo = s // HALO_ROWS
    return pl.pallas_call(
        functools.partial(_conv_tail_kernel, row_chunk=row_chunk, unroll=unroll),
        out_shape=jax.ShapeDtypeStruct((b, s, d), F32),
        grid=(b, s // tm),
        in_specs=[pl.BlockSpec((1, HALO_ROWS, d), lambda bi, i: (bi, jnp.maximum(i * hb - 1, 0), 0)),
                  pl.BlockSpec((1, tm, d), lambda bi, i: (bi, i, 0)),
                  pl.BlockSpec((1, HALO_ROWS, d),
                               lambda bi, i: (bi, jnp.minimum((i + 1) * hb, n_halo - 1), 0)),
                  pl.BlockSpec((1, tm, d), lambda bi, i: (bi, i, 0)),
                  pl.BlockSpec((1, 6, d), lambda bi, i: (bi, 0, 0)),
                  _resident(w_dw.shape), _resident((1, d)), _resident((1, d)), _resident((1, d)),
                  _resident(w_pw2.shape), _resident((1, d)), _resident((1, d)),
                  _resident(w_in.shape), _resident(w_out.shape)],
        out_specs=pl.BlockSpec((1, tm, d), lambda bi, i: (bi, i, 0)),
        scratch_shapes=[pltpu.VMEM((d // LANES, tm + 2 * HALO_ROWS, LANES), F32),
                        pltpu.VMEM((d // LANES, tm, LANES), F32)],
        compiler_params=_params(2),
        name="conv_dw_pw2_ffn",
    )(u, u, u, x, ada, w_dw, b_dw, ln_g, ln_b, w_pw2, b_pw2, gain2, w_in, w_out)


def _trunk(x, ada, biases, p):
    d = x.shape[-1]
    row = lambda v: v.reshape(1, -1)
    fuse_head = False
    u = None
    for i in range(DEPTH):
        j = i // N_MIXERS
        gain1, gain2 = row(p["norm1_g"][i]), row(p["norm2_g"][i])
        w_in, w_out = p["ffn_w_in"][i], p["ffn_w_out"][i]
        if i % N_MIXERS == 0:
            w4 = p["attn_w_qkv"][j].reshape(d, 3, N_GROUPS, HEADS_PER_GROUP * HEAD_DIM)
            qkvs = [_qkv_proj(x, ada[i], gain1, w4[:, :, g, :].reshape(d, -1),
                              row(p["attn_q_gain"][j]), row(p["attn_k_gain"][j]), dil)
                    for g, (_, dil) in enumerate(DILATED_GROUPS)]
            o = _attention(qkvs, biases)
            if fuse_head and i + 1 < DEPTH and (i + 1) % N_MIXERS == 1:
                jn = (i + 1) // N_MIXERS
                x, u = _attn_tail(o, x, ada[i], p["attn_w_o"][j], gain2, w_in, w_out,
                                  next_head=(ada[i + 1], row(p["norm1_g"][i + 1]),
                                             p["conv_w_pw1"][jn], row(p["conv_b_pw1"][jn])))
            else:
                x = _attn_tail(o, x, ada[i], p["attn_w_o"][j], gain2, w_in, w_out)
        else:
            if u is None:
                u = _conv_head(x, ada[i], gain1, p["conv_w_pw1"][j], row(p["conv_b_pw1"][j]))
            x = _conv_tail(u, x, ada[i], p["conv_w_dw"][j], row(p["conv_b_dw"][j]),
                           row(p["conv_ln_g"][j]), row(p["conv_ln_b"][j]), p["conv_w_pw2"][j],
                           row(p["conv_b_pw2"][j]), gain2, w_in, w_out)
            u = None
    return x


def kernel(x_prompt, x_sample, c_prompt, c_sample, rel_bias_table, norm1_g, norm2_g, ada_w, ada_b, attn_w_qkv, attn_q_gain, attn_k_gain, attn_w_o, conv_w_pw1, conv_b_pw1, conv_w_dw, conv_b_dw, conv_ln_g, conv_ln_b, conv_w_pw2, conv_b_pw2, ffn_w_in, ffn_w_out):
    p = dict(norm1_g=norm1_g, norm2_g=norm2_g, attn_q_gain=attn_q_gain, attn_k_gain=attn_k_gain,
             conv_b_pw1=conv_b_pw1, conv_w_dw=conv_w_dw, conv_b_dw=conv_b_dw, conv_ln_g=conv_ln_g,
             conv_ln_b=conv_ln_b, conv_b_pw2=conv_b_pw2,
             attn_w_qkv=attn_w_qkv.astype(BF16), attn_w_o=attn_w_o.astype(BF16),
             conv_w_pw1=conv_w_pw1.astype(BF16), conv_w_pw2=conv_w_pw2.astype(BF16),
             ffn_w_in=ffn_w_in.astype(BF16), ffn_w_out=ffn_w_out.astype(BF16))
    nb_p = c_prompt.shape[0]
    d = x_prompt.shape[-1]
    c_all = jnp.concatenate([c_prompt, c_sample], axis=0)
    ada = _ada_all(c_all, ada_w, ada_b).reshape(DEPTH, c_all.shape[0], 6, d)

    bias_cache = {}

    def biases_for(seq_len):
        out = []
        for g, (_, dil) in enumerate(DILATED_GROUPS):
            key = (g,) + _attn_geometry(seq_len, dil)
            if key not in bias_cache:
                bias_cache[key] = _expand_bias(rel_bias_table, _bias_bucket_index(seq_len, dil), g)
            out.append(bias_cache[key])
        return out

    y_prompt = _trunk(x_prompt, ada[:, :nb_p], biases_for(x_prompt.shape[1]), p)
    y_sample = _trunk(x_sample, ada[:, nb_p:], biases_for(x_sample.shape[1]), p)
    return (y_prompt, y_sample)
```

```python
import functools
import math

import jax
import jax.numpy as jnp
from jax import lax
from jax.experimental import pallas as pl
from jax.experimental.pallas import tpu as pltpu

F32 = jnp.float32
BF16 = jnp.bfloat16

D_MODEL = 1024
DEPTH = 4
N_MIXERS = 2
DILATED_GROUPS = ((128, 1), (512, 4), (2048, 16))
N_GROUPS = len(DILATED_GROUPS)
HEADS_PER_GROUP = 8
HEAD_DIM = 128
REL_BUCKETS = 32
REL_MAX_DIST = 1024
CONV_WIDTH = 31
CONV_PAD = (CONV_WIDTH - 1) // 2
D_FF = 2816
NEG_INF = -1e30
EPS = 1e-6

HALF = DILATED_GROUPS[0][0] // (2 * DILATED_GROUPS[0][1])
assert all(w // (2 * d) == HALF for w, d in DILATED_GROUPS)

TOKEN_TILE = 512
QKV_TOKEN_TILE = 1024
QUERY_BLOCK = 128
ATTN_BLOCKS_PER_STEP = 32
HALO_ROWS = 16
CONV_ROW_CHUNK = 64
SUBLANES = 8
BF16_SUBLANES = 16
LANES = 128
FF_CHUNKS = ((0, 768), (768, 768), (1536, 768), (2304, 512))
VMEM_LIMIT_BYTES = 60000 * 1024


def _params(n_axes):
    return pltpu.CompilerParams(dimension_semantics=("parallel",) * n_axes,
                                vmem_limit_bytes=VMEM_LIMIT_BYTES)


def _resident(shape):
    nd = len(shape)
    return pl.BlockSpec(shape, lambda *_: (0,) * nd, pipeline_mode=pl.Buffered(1))


def _silu(x):
    return x * jax.nn.sigmoid(x)


def _mod_rmsnorm(x, gain, shift, scale):
    ms = jnp.mean(x * x, axis=-1, keepdims=True)
    y = x * lax.rsqrt(ms + EPS) * gain
    return y * (1.0 + scale) + shift


def _ada_kernel(c_ref, w_ref, b_ref, o_ref):
    ca = _silu(c_ref[...]).astype(BF16)
    o_ref[0] = jnp.dot(ca, w_ref[0].astype(BF16), preferred_element_type=F32) + b_ref[0]


def _ada_all(c, ada_w, ada_b):
    nb, d = c.shape
    depth, _, width = ada_w.shape
    tn = 1536
    assert width % tn == 0
    return pl.pallas_call(
        _ada_kernel,
        out_shape=jax.ShapeDtypeStruct((depth, nb, width), F32),
        grid=(depth, width // tn),
        in_specs=[pl.BlockSpec((nb, d), lambda i, j: (0, 0)),
                  pl.BlockSpec((1, d, tn), lambda i, j: (i, 0, j)),
                  pl.BlockSpec((1, 1, tn), lambda i, j: (i, 0, j))],
        out_specs=pl.BlockSpec((1, nb, tn), lambda i, j: (i, 0, j)),
        compiler_params=_params(2),
        name="ada_proj",
    )(c, ada_w, ada_b.reshape(depth, 1, width))


def _t5_bucket(rel):
    half = REL_BUCKETS // 2
    max_exact = half // 2
    ret = jnp.where(rel > 0, half, 0)
    n = jnp.abs(rel)
    nf = jnp.maximum(n, 1).astype(F32)
    large = max_exact + (jnp.log(nf / max_exact) / math.log(REL_MAX_DIST / max_exact)
                         * (half - max_exact)).astype(jnp.int32)
    large = jnp.minimum(large, half - 1)
    return ret + jnp.where(n < max_exact, n, large)


def _attn_geometry(seq_len, dil):
    sub_len = seq_len // dil
    qb = min(QUERY_BLOCK, sub_len)
    kw = min(qb + 2 * HALF, sub_len)
    nblk = sub_len // qb
    assert sub_len * dil == seq_len and nblk * qb == sub_len and qb % HALF == 0
    return sub_len, qb, kw, nblk


def _key_offset(n, qb, kw, sub_len):
    return min(max(n * qb - HALF, 0), sub_len - kw)


def _bias_bucket_index(seq_len, dil):
    sub_len, qb, kw, nblk = _attn_geometry(seq_len, dil)
    slots = []
    for n in (0, min(1, nblk - 1), nblk - 1):
        delta = n * qb - _key_offset(n, qb, kw, sub_len)
        rel = jnp.arange(kw)[None, :] - jnp.arange(qb)[:, None] - delta
        slots.append(jnp.where(jnp.abs(rel) <= HALF, _t5_bucket(rel * dil), -1))
    return jnp.stack(slots).astype(jnp.int32)


def _bias_kernel(tab_ref, idx_ref, o_ref, *, head0):
    h = pl.program_id(0)
    idx = idx_ref[...]
    acc = jnp.full(idx.shape, NEG_INF, F32)
    for b in range(REL_BUCKETS):
        acc = jnp.where(idx == b, tab_ref[b, head0 + h], acc)
    o_ref[0] = acc


def _expand_bias(rel_table, idx, group):
    _, qb, kw = idx.shape
    return pl.pallas_call(
        functools.partial(_bias_kernel, head0=group * HEADS_PER_GROUP),
        out_shape=jax.ShapeDtypeStruct((HEADS_PER_GROUP, 3, qb, kw), F32),
        grid=(HEADS_PER_GROUP,),
        in_specs=[pl.BlockSpec(memory_space=pltpu.SMEM),
                  pl.BlockSpec((3, qb, kw), lambda h: (0, 0, 0))],
        out_specs=pl.BlockSpec((1, 3, qb, kw), lambda h: (h, 0, 0, 0)),
        compiler_params=_params(1),
        name="rel_bias_expand",
    )(rel_table, idx)


def _qkv_kernel(x_ref, ada_ref, g_ref, w_ref, qg_ref, kg_ref, o_ref, *scratch, dil, two_level):
    tm, d = x_ref.shape[1], x_ref.shape[2]
    sub = tm // dil
    h = _mod_rmsnorm(x_ref[0], g_ref[...], ada_ref[0, 0:1, :], ada_ref[0, 1:2, :])
    if dil == 1:
        h = h.astype(BF16)
    elif two_level:
        step = math.isqrt(dil)
        slab_ref, perm_ref, mid_ref = scratch
        part = tm // step
        for c in range(d // LANES):
            slab_ref[c] = h[:, c * LANES:(c + 1) * LANES]
        for c in range(d // LANES):
            for r0 in range(step):
                mid_ref[c, r0 * part:(r0 + 1) * part, :] = slab_ref[c, pl.ds(r0, part, stride=step), :]
        for c in range(d // LANES):
            for r in range(dil):
                r0, r1 = r % step, r // step
                perm_ref[r * sub:(r + 1) * sub, c * LANES:(c + 1) * LANES] = (
                    mid_ref[c, pl.ds(r0 * part + r1, sub, stride=step), :].astype(BF16))
        h = perm_ref[...]
    else:
        slab_ref, perm_ref = scratch
        for c in range(d // LANES):
            slab_ref[c] = h[:, c * LANES:(c + 1) * LANES]
        for c in range(d // LANES):
            for r in range(dil):
                perm_ref[r * sub:(r + 1) * sub, c * LANES:(c + 1) * LANES] = (
                    slab_ref[c, pl.ds(r, sub, stride=dil), :].astype(BF16))
        h = perm_ref[...]
    width = HEADS_PER_GROUP * HEAD_DIM
    for t in range(3):
        acc = jnp.dot(h, w_ref[:, t * width:(t + 1) * width], preferred_element_type=F32)
        for hd in range(HEADS_PER_GROUP):
            c = acc[:, hd * HEAD_DIM:(hd + 1) * HEAD_DIM]
            if t < 2:
                ms = jnp.mean(c * c, axis=-1, keepdims=True)
                c = c * lax.rsqrt(ms + EPS) * (qg_ref[...] if t == 0 else kg_ref[...])
            if t == 0:
                c = c * (HEAD_DIM ** -0.5)
            o_ref[0, t, hd] = c.astype(BF16).reshape(dil, sub, HEAD_DIM)


def _qkv_proj(x, ada, gain, w, q_gain, k_gain, dil):
    b, s, d = x.shape
    sub_len = s // dil
    tm = min(QKV_TOKEN_TILE, s)
    assert s % tm == 0 and tm % (dil * BF16_SUBLANES) == 0
    width = 3 * HEADS_PER_GROUP * HEAD_DIM
    scratch = [] if dil == 1 else [pltpu.VMEM((d // LANES, tm, LANES), F32),
                                   pltpu.VMEM((tm, d), BF16)]
    two_level = dil == 16
    if two_level:
        scratch.append(pltpu.VMEM((d // LANES, tm, LANES), F32))
    return pl.pallas_call(
        functools.partial(_qkv_kernel, dil=dil, two_level=two_level),
        out_shape=jax.ShapeDtypeStruct((b, 3, HEADS_PER_GROUP, dil, sub_len, HEAD_DIM), BF16),
        grid=(b, s // tm),
        in_specs=[pl.BlockSpec((1, tm, d), lambda bi, i: (bi, i, 0)),
                  pl.BlockSpec((1, 6, d), lambda bi, i: (bi, 0, 0)),
                  _resident((1, d)),
                  _resident((d, width)),
                  _resident((1, HEAD_DIM)),
                  _resident((1, HEAD_DIM))],
        out_specs=pl.BlockSpec((1, 3, HEADS_PER_GROUP, dil, tm // dil, HEAD_DIM),
                               lambda bi, i: (bi, 0, 0, 0, i, 0)),
        scratch_shapes=scratch,
        compiler_params=_params(2),
        name=f"qkv_proj_d{dil}",
    )(x, ada, gain, w, q_gain, k_gain)


def _attn_kernel(qkv0_ref, qkv1_ref, qkv2_ref, b0_ref, b1_ref, b2_ref, o_ref, og_ref, mg_ref,
                 lg_ref, *, seq_len):
    for g, (qkv_ref, bias_ref) in enumerate(((qkv0_ref, b0_ref), (qkv1_ref, b1_ref),
                                              (qkv2_ref, b2_ref))):
        dil = DILATED_GROUPS[g][1]
        sub_len, qb, kw, nblk = _attn_geometry(seq_len, dil)

        def block(r, n, g=g, dil=dil, sub_len=sub_len, qb=qb, kw=kw, nblk=nblk,
                  qkv_ref=qkv_ref, bias_ref=bias_ref):
            q0 = pl.multiple_of(r * sub_len + n * qb, qb)
            koff = jnp.clip(n * qb - HALF, 0, sub_len - kw)
            k0 = pl.multiple_of(r * sub_len + koff, HALF)
            q = qkv_ref[0, 0, 0, pl.ds(q0, qb), :]
            k = qkv_ref[0, 1, 0, pl.ds(k0, kw), :]
            v = qkv_ref[0, 2, 0, pl.ds(k0, kw), :]
            slot = jnp.where(n == 0, 0, jnp.where(n == nblk - 1, 2, 1))
            s = lax.dot_general(q, k, (((1,), (1,)), ((), ())), preferred_element_type=F32)
            s = s + bias_ref[0, slot]
            m = jnp.max(s, axis=-1, keepdims=True)
            p = jnp.exp(s - m).astype(BF16)
            v_ones = jnp.concatenate([v, jnp.ones((kw, HEAD_DIM), BF16)], axis=1)
            o_l = jnp.dot(p, v_ones, preferred_element_type=F32)
            start = r + n * (qb * dil)
            rows = pl.ds(start, qb) if dil == 1 else pl.ds(start, qb, stride=dil)
            og_ref[g, rows, :] = o_l[:, :HEAD_DIM]
            mg_ref[g, rows, :] = jnp.broadcast_to(m, (qb, HEAD_DIM))
            lg_ref[g, rows, :] = o_l[:, HEAD_DIM:]

        total = dil * nblk
        par = math.gcd(total, ATTN_BLOCKS_PER_STEP)

        def step(it, carry, block=block, nblk=nblk, par=par):
            for u in range(par):
                idx = it * par + u
                block(lax.div(idx, nblk), lax.rem(idx, nblk))
            return carry

        lax.fori_loop(0, total // par, step, 0)

    rc = 256

    def merge(c, carry):
        rows = pl.ds(pl.multiple_of(c * rc, rc), rc)
        ms = [mg_ref[g, rows, :] for g in range(N_GROUPS)]
        mx = jnp.maximum(jnp.maximum(ms[0], ms[1]), ms[2])
        num = den = None
        for g in range(N_GROUPS):
            w = jnp.exp(ms[g] - mx)
            num = w * og_ref[g, rows, :] if num is None else num + w * og_ref[g, rows, :]
            den = w * lg_ref[g, rows, :] if den is None else den + w * lg_ref[g, rows, :]
        o_ref[0, 0, rows, :] = (num / den).astype(BF16)
        return carry

    lax.fori_loop(0, seq_len // rc, merge, 0)


def _attention(qkvs, biases):
    b = qkvs[0].shape[0]
    s = qkvs[0].shape[3] * qkvs[0].shape[4]
    qkvs = [t.reshape(b, 3, HEADS_PER_GROUP, s, HEAD_DIM) for t in qkvs]
    qkv_spec = pl.BlockSpec((1, 3, 1, s, HEAD_DIM), lambda bi, h: (bi, 0, h, 0, 0))
    bias_specs = [pl.BlockSpec((1,) + t.shape[1:], lambda bi, h: (h, 0, 0, 0)) for t in biases]
    return pl.pallas_call(
        functools.partial(_attn_kernel, seq_len=s),
        out_shape=jax.ShapeDtypeStruct((b, HEADS_PER_GROUP, s, HEAD_DIM), BF16),
        grid=(b, HEADS_PER_GROUP),
        in_specs=[qkv_spec] * N_GROUPS + bias_specs,
        out_specs=pl.BlockSpec((1, 1, s, HEAD_DIM), lambda bi, h: (bi, h, 0, 0)),
        scratch_shapes=[pltpu.VMEM((N_GROUPS, s, HEAD_DIM), F32)] * 3,
        compiler_params=_params(2),
        name="dilated_attention",
    )(*qkvs, *biases)


def _residual_ffn(x, y, ada_ref, g2_ref, w_in_ref, w_out_ref, chunks=FF_CHUNKS):
    x1 = x + ada_ref[0, 2:3, :] * y
    h = _mod_rmsnorm(x1, g2_ref[...], ada_ref[0, 3:4, :], ada_ref[0, 4:5, :]).astype(BF16)
    acc = None
    for c0, cw in chunks:
        gate = jnp.dot(h, w_in_ref[:, c0:c0 + cw], preferred_element_type=F32)
        up = jnp.dot(h, w_in_ref[:, D_FF + c0:D_FF + c0 + cw], preferred_element_type=F32)
        a = (_silu(gate) * up).astype(BF16)
        part = jnp.dot(a, w_out_ref[c0:c0 + cw, :], preferred_element_type=F32)
        acc = part if acc is None else acc + part
    return x1 + ada_ref[0, 5:6, :] * acc


def _glu_head(x, ada_ref, g_ref, w_ref, b_ref):
    d = x.shape[-1]
    h = _mod_rmsnorm(x, g_ref[...], ada_ref[0, 0:1, :], ada_ref[0, 1:2, :]).astype(BF16)
    a = jnp.dot(h, w_ref[:, :d], preferred_element_type=F32) + b_ref[:, :d]
    gt = jnp.dot(h, w_ref[:, d:], preferred_element_type=F32) + b_ref[:, d:]
    return a * jax.nn.sigmoid(gt)


def _attn_tail_kernel(o_ref, x_ref, ada_ref, wo_ref, g2_ref, w_in_ref, w_out_ref, out_ref, *,
                      chunks):
    o = jnp.concatenate([o_ref[0, h] for h in range(HEADS_PER_GROUP)], axis=-1)
    y = jnp.dot(o, wo_ref[...], preferred_element_type=F32)
    out_ref[0] = _residual_ffn(x_ref[0], y, ada_ref, g2_ref, w_in_ref, w_out_ref, chunks)


def _attn_tail(o, x, ada, w_o, gain2, w_in, w_out):
    b, s, d = x.shape
    tm = min(TOKEN_TILE, s)
    chunks = ((0, 1536), (1536, 1280)) if s == 4096 else FF_CHUNKS
    return pl.pallas_call(
        functools.partial(_attn_tail_kernel, chunks=chunks),
        out_shape=jax.ShapeDtypeStruct((b, s, d), F32),
        grid=(b, s // tm),
        in_specs=[pl.BlockSpec((1, HEADS_PER_GROUP, tm, HEAD_DIM), lambda bi, i: (bi, 0, i, 0)),
                  pl.BlockSpec((1, tm, d), lambda bi, i: (bi, i, 0)),
                  pl.BlockSpec((1, 6, d), lambda bi, i: (bi, 0, 0)),
                  _resident(w_o.shape), _resident((1, d)),
                  _resident(w_in.shape), _resident(w_out.shape)],
        out_specs=pl.BlockSpec((1, tm, d), lambda bi, i: (bi, i, 0)),
        compiler_params=_params(2),
        name="attn_out_ffn",
    )(o, x, ada, w_o, gain2, w_in, w_out)


def _conv_head_kernel(x_ref, ada_ref, g_ref, w_ref, b_ref, u_ref):
    u_ref[0] = _glu_head(x_ref[0], ada_ref, g_ref, w_ref, b_ref)


def _conv_head(x, ada, gain, w_pw1, b_pw1):
    b, s, d = x.shape
    tm = min(2 * TOKEN_TILE if s == 4096 else TOKEN_TILE, s)
    return pl.pallas_call(
        _conv_head_kernel,
        out_shape=jax.ShapeDtypeStruct((b, s, d), F32),
        grid=(b, s // tm),
        in_specs=[pl.BlockSpec((1, tm, d), lambda bi, i: (bi, i, 0)),
                  pl.BlockSpec((1, 6, d), lambda bi, i: (bi, 0, 0)),
                  _resident((1, d)), _resident(w_pw1.shape), _resident((1, 2 * d))],
        out_specs=pl.BlockSpec((1, tm, d), lambda bi, i: (bi, i, 0)),
        compiler_params=_params(2),
        name="conv_pw1_glu",
    )(x, ada, gain, w_pw1, b_pw1)


def _conv_tail_kernel(prev_ref, u_ref, next_ref, x_ref, ada_ref, wdw_ref, bdw_ref, lng_ref,
                      lnb_ref, w2_ref, b2_ref, g2_ref, w_in_ref, w_out_ref, out_ref,
                      win_ref, cv_ref, *, row_chunk, unroll):
    tm, d = u_ref.shape[1], u_ref.shape[2]
    i = pl.program_id(1)
    last = pl.num_programs(1) - 1
    prev = jnp.where(i > 0, prev_ref[0], 0.0)
    nxt = jnp.where(i < last, next_ref[0], 0.0)
    for c in range(d // LANES):
        lanes = slice(c * LANES, (c + 1) * LANES)
        win_ref[c, 0:HALO_ROWS, :] = prev[:, lanes]
        win_ref[c, HALO_ROWS:HALO_ROWS + tm, :] = u_ref[0, :, lanes]
        win_ref[c, HALO_ROWS + tm:, :] = nxt[:, lanes]

    for c in range(d // LANES):
        lanes = slice(c * LANES, (c + 1) * LANES)

        def conv_rows(j, carry, c=c, lanes=lanes):
            r0 = j * (2 * row_chunk)
            for parity in range(2):
                acc = None
                for k in range(CONV_WIDTH):
                    start = r0 + (parity + HALO_ROWS - CONV_PAD + k)
                    tap = win_ref[c, pl.ds(start, row_chunk, stride=2), :] * wdw_ref[k:k + 1, lanes]
                    acc = tap if acc is None else acc + tap
                cv_ref[c, pl.ds(r0 + parity, row_chunk, stride=2), :] = acc
            return carry

        lax.fori_loop(0, tm // (2 * row_chunk), conv_rows, 0, unroll=unroll)

    cv = jnp.concatenate([cv_ref[c] for c in range(d // LANES)], axis=1) + bdw_ref[...]
    mu = jnp.mean(cv, axis=-1, keepdims=True)
    xc = cv - mu
    var = jnp.mean(xc * xc, axis=-1, keepdims=True)
    ln = xc * lax.rsqrt(var + EPS) * lng_ref[...] + lnb_ref[...]
    y = jnp.dot(_silu(ln).astype(BF16), w2_ref[...], preferred_element_type=F32) + b2_ref[...]
    out_ref[0] = _residual_ffn(x_ref[0], y, ada_ref, g2_ref, w_in_ref, w_out_ref)


def _conv_tail(u, x, ada, w_dw, b_dw, ln_g, ln_b, w_pw2, b_pw2, gain2, w_in, w_out):
    b, s, d = x.shape
    tm = min(TOKEN_TILE, s)
    row_chunk = CONV_ROW_CHUNK
    unroll = 2
    assert s % tm == 0 and tm % (2 * row_chunk) == 0 and HALO_ROWS >= CONV_PAD + 1
    hb = tm // HALO_ROWS
    n_halo = s // HALO_ROWS
    return pl.pallas_call(
        functools.partial(_conv_tail_kernel, row_chunk=row_chunk, unroll=unroll),
        out_shape=jax.ShapeDtypeStruct((b, s, d), F32),
        grid=(b, s // tm),
        in_specs=[pl.BlockSpec((1, HALO_ROWS, d), lambda bi, i: (bi, jnp.maximum(i * hb - 1, 0), 0)),
                  pl.BlockSpec((1, tm, d), lambda bi, i: (bi, i, 0)),
                  pl.BlockSpec((1, HALO_ROWS, d),
                               lambda bi, i: (bi, jnp.minimum((i + 1) * hb, n_halo - 1), 0)),
                  pl.BlockSpec((1, tm, d), lambda bi, i: (bi, i, 0)),
                  pl.BlockSpec((1, 6, d), lambda bi, i: (bi, 0, 0)),
                  _resident(w_dw.shape), _resident((1, d)), _resident((1, d)), _resident((1, d)),
                  _resident(w_pw2.shape), _resident((1, d)), _resident((1, d)),
                  _resident(w_in.shape), _resident(w_out.shape)],
        out_specs=pl.BlockSpec((1, tm, d), lambda bi, i: (bi, i, 0)),
        scratch_shapes=[pltpu.VMEM((d // LANES, tm + 2 * HALO_ROWS, LANES), F32),
                        pltpu.VMEM((d // LANES, tm, LANES), F32)],
        compiler_params=_params(2),
        name="conv_dw_pw2_ffn",
    )(u, u, u, x, ada, w_dw, b_dw, ln_g, ln_b, w_pw2, b_pw2, gain2, w_in, w_out)


def _trunk(x, ada, biases, p):
    d = x.shape[-1]
    row = lambda v: v.reshape(1, -1)
    for i in range(DEPTH):
        j = i // N_MIXERS
        gain1, gain2 = row(p["norm1_g"][i]), row(p["norm2_g"][i])
        w_in, w_out = p["ffn_w_in"][i], p["ffn_w_out"][i]
        if i % N_MIXERS == 0:
            w4 = p["attn_w_qkv"][j].reshape(d, 3, N_GROUPS, HEADS_PER_GROUP * HEAD_DIM)
            qkvs = [_qkv_proj(x, ada[i], gain1, w4[:, :, g, :].reshape(d, -1),
                              row(p["attn_q_gain"][j]), row(p["attn_k_gain"][j]), dil)
                    for g, (_, dil) in enumerate(DILATED_GROUPS)]
            o = _attention(qkvs, biases)
            x = _attn_tail(o, x, ada[i], p["attn_w_o"][j], gain2, w_in, w_out)
        else:
            u = _conv_head(x, ada[i], gain1, p["conv_w_pw1"][j], row(p["conv_b_pw1"][j]))
            x = _conv_tail(u, x, ada[i], p["conv_w_dw"][j], row(p["conv_b_dw"][j]),
                           row(p["conv_ln_g"][j]), row(p["conv_ln_b"][j]), p["conv_w_pw2"][j],
                           row(p["conv_b_pw2"][j]), gain2, w_in, w_out)
    return x


def kernel(x_prompt, x_sample, c_prompt, c_sample, rel_bias_table, norm1_g, norm2_g, ada_w, ada_b, attn_w_qkv, attn_q_gain, attn_k_gain, attn_w_o, conv_w_pw1, conv_b_pw1, conv_w_dw, conv_b_dw, conv_ln_g, conv_ln_b, conv_w_pw2, conv_b_pw2, ffn_w_in, ffn_w_out):
    p = dict(norm1_g=norm1_g, norm2_g=norm2_g, attn_q_gain=attn_q_gain, attn_k_gain=attn_k_gain,
             conv_b_pw1=conv_b_pw1, conv_w_dw=conv_w_dw, conv_b_dw=conv_b_dw, conv_ln_g=conv_ln_g,
             conv_ln_b=conv_ln_b, conv_b_pw2=conv_b_pw2,
             attn_w_qkv=attn_w_qkv.astype(BF16), attn_w_o=attn_w_o.astype(BF16),
             conv_w_pw1=conv_w_pw1.astype(BF16), conv_w_pw2=conv_w_pw2.astype(BF16),
             ffn_w_in=ffn_w_in.astype(BF16), ffn_w_out=ffn_w_out.astype(BF16))
    nb_p = c_prompt.shape[0]
    d = x_prompt.shape[-1]
    c_all = jnp.concatenate([c_prompt, c_sample], axis=0)
    ada = _ada_all(c_all, ada_w, ada_b).reshape(DEPTH, c_all.shape[0], 6, d)

    bias_cache = {}

    def biases_for(seq_len):
        out = []
        for g, (_, dil) in enumerate(DILATED_GROUPS):
            key = (g,) + _attn_geometry(seq_len, dil)
            if key not in bias_cache:
                bias_cache[key] = _expand_bias(rel_bias_table, _bias_bucket_index(seq_len, dil), g)
            out.append(bias_cache[key])
        return out

    y_prompt = _trunk(x_prompt, ada[:, :nb_p], biases_for(x_prompt.shape[1]), p)
    y_sample = _trunk(x_sample, ada[:, nb_p:], biases_for(x_sample.shape[1]), p)
    return (y_prompt, y_sample)
```

```python
import functools
import math

import jax
import jax.numpy as jnp
from jax import lax
from jax.experimental import pallas as pl
from jax.experimental.pallas import tpu as pltpu

F32 = jnp.float32
BF16 = jnp.bfloat16

D_MODEL = 1024
DEPTH = 4
N_MIXERS = 2
DILATED_GROUPS = ((128, 1), (512, 4), (2048, 16))
N_GROUPS = len(DILATED_GROUPS)
HEADS_PER_GROUP = 8
HEAD_DIM = 128
REL_BUCKETS = 32
REL_MAX_DIST = 1024
CONV_WIDTH = 31
CONV_PAD = (CONV_WIDTH - 1) // 2
D_FF = 2816
NEG_INF = -1e30
EPS = 1e-6

HALF = DILATED_GROUPS[0][0] // (2 * DILATED_GROUPS[0][1])
assert all(w // (2 * d) == HALF for w, d in DILATED_GROUPS)

TOKEN_TILE = 512
PROJ_TOKEN_TILE = 1024
QUERY_BLOCK = 128
ATTN_BLOCKS_PER_STEP = 32
HALO_ROWS = 16
CONV_ROW_CHUNK = 64
CONV_LOOP_UNROLL = 2
SUBLANES = 8
BF16_SUBLANES = 16
LANES = 128
FF_CHUNKS = ((0, 768), (768, 768), (1536, 768), (2304, 512))
VMEM_LIMIT_BYTES = 60000 * 1024


def _params(n_axes):
    return pltpu.CompilerParams(dimension_semantics=("parallel",) * n_axes,
                                vmem_limit_bytes=VMEM_LIMIT_BYTES)


def _resident(shape):
    nd = len(shape)
    return pl.BlockSpec(shape, lambda *_: (0,) * nd, pipeline_mode=pl.Buffered(1))


def _silu(x):
    return x * jax.nn.sigmoid(x)


def _mod_rmsnorm(x, gain, shift, scale):
    ms = jnp.mean(x * x, axis=-1, keepdims=True)
    y = x * lax.rsqrt(ms + EPS) * gain
    return y * (1.0 + scale) + shift


def _ada_kernel(c_ref, w_ref, b_ref, o_ref):
    ca = _silu(c_ref[...]).astype(BF16)
    o_ref[0] = jnp.dot(ca, w_ref[0].astype(BF16), preferred_element_type=F32) + b_ref[0]


def _ada_all(c, ada_w, ada_b):
    nb, d = c.shape
    depth, _, width = ada_w.shape
    tn = 1536
    assert width % tn == 0
    return pl.pallas_call(
        _ada_kernel,
        out_shape=jax.ShapeDtypeStruct((depth, nb, width), F32),
        grid=(depth, width // tn),
        in_specs=[pl.BlockSpec((nb, d), lambda i, j: (0, 0)),
                  pl.BlockSpec((1, d, tn), lambda i, j: (i, 0, j)),
                  pl.BlockSpec((1, 1, tn), lambda i, j: (i, 0, j))],
        out_specs=pl.BlockSpec((1, nb, tn), lambda i, j: (i, 0, j)),
        compiler_params=_params(2),
        name="ada_proj",
    )(c, ada_w, ada_b.reshape(depth, 1, width))


def _t5_bucket(rel):
    half = REL_BUCKETS // 2
    max_exact = half // 2
    ret = jnp.where(rel > 0, half, 0)
    n = jnp.abs(rel)
    nf = jnp.maximum(n, 1).astype(F32)
    large = max_exact + (jnp.log(nf / max_exact) / math.log(REL_MAX_DIST / max_exact)
                         * (half - max_exact)).astype(jnp.int32)
    large = jnp.minimum(large, half - 1)
    return ret + jnp.where(n < max_exact, n, large)


def _attn_geometry(seq_len, dil):
    sub_len = seq_len // dil
    qb = min(QUERY_BLOCK, sub_len)
    kw = min(qb + 2 * HALF, sub_len)
    nblk = sub_len // qb
    assert sub_len * dil == seq_len and nblk * qb == sub_len and qb % HALF == 0
    return sub_len, qb, kw, nblk


def _key_offset(n, qb, kw, sub_len):
    return min(max(n * qb - HALF, 0), sub_len - kw)


def _bias_bucket_index(seq_len, dil):
    sub_len, qb, kw, nblk = _attn_geometry(seq_len, dil)
    slots = []
    for n in (0, min(1, nblk - 1), nblk - 1):
        delta = n * qb - _key_offset(n, qb, kw, sub_len)
        rel = jnp.arange(kw)[None, :] - jnp.arange(qb)[:, None] - delta
        slots.append(jnp.where(jnp.abs(rel) <= HALF, _t5_bucket(rel * dil), -1))
    return jnp.stack(slots).astype(jnp.int32)


def _bias_kernel(tab_ref, idx_ref, o_ref, *, head0):
    h = pl.program_id(0)
    idx = idx_ref[...]
    acc = jnp.full(idx.shape, NEG_INF, F32)
    for b in range(REL_BUCKETS):
        acc = jnp.where(idx == b, tab_ref[b, head0 + h], acc)
    o_ref[0] = acc


def _expand_bias(rel_table, idx, group):
    _, qb, kw = idx.shape
    return pl.pallas_call(
        functools.partial(_bias_kernel, head0=group * HEADS_PER_GROUP),
        out_shape=jax.ShapeDtypeStruct((HEADS_PER_GROUP, 3, qb, kw), F32),
        grid=(HEADS_PER_GROUP,),
        in_specs=[pl.BlockSpec(memory_space=pltpu.SMEM),
                  pl.BlockSpec((3, qb, kw), lambda h: (0, 0, 0))],
        out_specs=pl.BlockSpec((1, 3, qb, kw), lambda h: (h, 0, 0, 0)),
        compiler_params=_params(1),
        name="rel_bias_expand",
    )(rel_table, idx)


def _qkv_kernel(x_ref, ada_ref, g_ref, w_ref, qg_ref, kg_ref, o_ref, *scratch, dil, two_level):
    tm, d = x_ref.shape[1], x_ref.shape[2]
    sub = tm // dil
    h = _mod_rmsnorm(x_ref[0], g_ref[...], ada_ref[0, 0:1, :], ada_ref[0, 1:2, :])
    if dil == 1:
        h = h.astype(BF16)
    elif two_level:
        step = math.isqrt(dil)
        slab_ref, perm_ref, mid_ref = scratch
        part = tm // step
        for c in range(d // LANES):
            slab_ref[c] = h[:, c * LANES:(c + 1) * LANES]
        for c in range(d // LANES):
            for r0 in range(step):
                mid_ref[c, r0 * part:(r0 + 1) * part, :] = slab_ref[c, pl.ds(r0, part, stride=step), :]
        for c in range(d // LANES):
            for r in range(dil):
                r0, r1 = r % step, r // step
                perm_ref[r * sub:(r + 1) * sub, c * LANES:(c + 1) * LANES] = (
                    mid_ref[c, pl.ds(r0 * part + r1, sub, stride=step), :].astype(BF16))
        h = perm_ref[...]
    else:
        slab_ref, perm_ref = scratch
        for c in range(d // LANES):
            slab_ref[c] = h[:, c * LANES:(c + 1) * LANES]
        for c in range(d // LANES):
            for r in range(dil):
                perm_ref[r * sub:(r + 1) * sub, c * LANES:(c + 1) * LANES] = (
                    slab_ref[c, pl.ds(r, sub, stride=dil), :].astype(BF16))
        h = perm_ref[...]
    width = HEADS_PER_GROUP * HEAD_DIM
    for t in range(3):
        acc = jnp.dot(h, w_ref[:, t * width:(t + 1) * width], preferred_element_type=F32)
        for hd in range(HEADS_PER_GROUP):
            c = acc[:, hd * HEAD_DIM:(hd + 1) * HEAD_DIM]
            if t < 2:
                ms = jnp.mean(c * c, axis=-1, keepdims=True)
                c = c * lax.rsqrt(ms + EPS) * (qg_ref[...] if t == 0 else kg_ref[...])
            if t == 0:
                c = c * (HEAD_DIM ** -0.5)
            o_ref[0, t, hd] = c.astype(BF16).reshape(dil, sub, HEAD_DIM)


def _qkv_proj(x, ada, gain, w, q_gain, k_gain, dil):
    b, s, d = x.shape
    sub_len = s // dil
    tm = min(PROJ_TOKEN_TILE, s)
    assert s % tm == 0 and tm % (dil * BF16_SUBLANES) == 0
    width = 3 * HEADS_PER_GROUP * HEAD_DIM
    scratch = [] if dil == 1 else [pltpu.VMEM((d // LANES, tm, LANES), F32),
                                   pltpu.VMEM((tm, d), BF16)]
    two_level = dil > 4 and math.isqrt(dil) ** 2 == dil
    if two_level:
        scratch.append(pltpu.VMEM((d // LANES, tm, LANES), F32))
    return pl.pallas_call(
        functools.partial(_qkv_kernel, dil=dil, two_level=two_level),
        out_shape=jax.ShapeDtypeStruct((b, 3, HEADS_PER_GROUP, dil, sub_len, HEAD_DIM), BF16),
        grid=(b, s // tm),
        in_specs=[pl.BlockSpec((1, tm, d), lambda bi, i: (bi, i, 0)),
                  pl.BlockSpec((1, 6, d), lambda bi, i: (bi, 0, 0)),
                  _resident((1, d)),
                  _resident((d, width)),
                  _resident((1, HEAD_DIM)),
                  _resident((1, HEAD_DIM))],
        out_specs=pl.BlockSpec((1, 3, HEADS_PER_GROUP, dil, tm // dil, HEAD_DIM),
                               lambda bi, i: (bi, 0, 0, 0, i, 0)),
        scratch_shapes=scratch,
        compiler_params=_params(2),
        name=f"qkv_proj_d{dil}",
    )(x, ada, gain, w, q_gain, k_gain)


def _attn_kernel(qkv0_ref, qkv1_ref, qkv2_ref, b0_ref, b1_ref, b2_ref, o_ref, og_ref, mg_ref,
                 lg_ref, *, seq_len):
    for g, (qkv_ref, bias_ref) in enumerate(((qkv0_ref, b0_ref), (qkv1_ref, b1_ref),
                                              (qkv2_ref, b2_ref))):
        dil = DILATED_GROUPS[g][1]
        sub_len, qb, kw, nblk = _attn_geometry(seq_len, dil)

        def block(r, n, g=g, dil=dil, sub_len=sub_len, qb=qb, kw=kw, nblk=nblk,
                  qkv_ref=qkv_ref, bias_ref=bias_ref):
            q0 = pl.multiple_of(r * sub_len + n * qb, qb)
            koff = jnp.clip(n * qb - HALF, 0, sub_len - kw)
            k0 = pl.multiple_of(r * sub_len + koff, HALF)
            q = qkv_ref[0, 0, 0, pl.ds(q0, qb), :]
            k = qkv_ref[0, 1, 0, pl.ds(k0, kw), :]
            v = qkv_ref[0, 2, 0, pl.ds(k0, kw), :]
            slot = jnp.where(n == 0, 0, jnp.where(n == nblk - 1, 2, 1))
            s = lax.dot_general(q, k, (((1,), (1,)), ((), ())), preferred_element_type=F32)
            s = s + bias_ref[0, slot]
            m = jnp.max(s, axis=-1, keepdims=True)
            p = jnp.exp(s - m).astype(BF16)
            v_ones = jnp.concatenate([v, jnp.ones((kw, HEAD_DIM), BF16)], axis=1)
            o_l = jnp.dot(p, v_ones, preferred_element_type=F32)
            start = r + n * (qb * dil)
            rows = pl.ds(start, qb) if dil == 1 else pl.ds(start, qb, stride=dil)
            og_ref[g, rows, :] = o_l[:, :HEAD_DIM]
            mg_ref[g, rows, :] = jnp.broadcast_to(m, (qb, HEAD_DIM))
            lg_ref[g, rows, :] = o_l[:, HEAD_DIM:]

        total = dil * nblk
        par = math.gcd(total, ATTN_BLOCKS_PER_STEP)

        def step(it, carry, block=block, nblk=nblk, par=par):
            for u in range(par):
                idx = it * par + u
                block(lax.div(idx, nblk), lax.rem(idx, nblk))
            return carry

        lax.fori_loop(0, total // par, step, 0)

    rc = 256

    def merge(c, carry):
        rows = pl.ds(pl.multiple_of(c * rc, rc), rc)
        ms = [mg_ref[g, rows, :] for g in range(N_GROUPS)]
        mx = jnp.maximum(jnp.maximum(ms[0], ms[1]), ms[2])
        num = den = None
        for g in range(N_GROUPS):
            w = jnp.exp(ms[g] - mx)
            num = w * og_ref[g, rows, :] if num is None else num + w * og_ref[g, rows, :]
            den = w * lg_ref[g, rows, :] if den is None else den + w * lg_ref[g, rows, :]
        o_ref[0, 0, rows, :] = (num / den).astype(BF16)
        return carry

    lax.fori_loop(0, seq_len // rc, merge, 0)


def _attention(qkvs, biases):
    b = qkvs[0].shape[0]
    s = qkvs[0].shape[3] * qkvs[0].shape[4]
    qkvs = [t.reshape(b, 3, HEADS_PER_GROUP, s, HEAD_DIM) for t in qkvs]
    qkv_spec = pl.BlockSpec((1, 3, 1, s, HEAD_DIM), lambda bi, h: (bi, 0, h, 0, 0))
    bias_specs = [pl.BlockSpec((1,) + t.shape[1:], lambda bi, h: (h, 0, 0, 0)) for t in biases]
    return pl.pallas_call(
        functools.partial(_attn_kernel, seq_len=s),
        out_shape=jax.ShapeDtypeStruct((b, HEADS_PER_GROUP, s, HEAD_DIM), BF16),
        grid=(b, HEADS_PER_GROUP),
        in_specs=[qkv_spec] * N_GROUPS + bias_specs,
        out_specs=pl.BlockSpec((1, 1, s, HEAD_DIM), lambda bi, h: (bi, h, 0, 0)),
        scratch_shapes=[pltpu.VMEM((N_GROUPS, s, HEAD_DIM), F32)] * 3,
        compiler_params=_params(2),
        name="dilated_attention",
    )(*qkvs, *biases)


def _residual_ffn(x, y, ada_ref, g2_ref, w_in_ref, w_out_ref):
    x1 = x + ada_ref[0, 2:3, :] * y
    h = _mod_rmsnorm(x1, g2_ref[...], ada_ref[0, 3:4, :], ada_ref[0, 4:5, :]).astype(BF16)
    acc = None
    for c0, cw in FF_CHUNKS:
        gate = jnp.dot(h, w_in_ref[:, c0:c0 + cw], preferred_element_type=F32)
        up = jnp.dot(h, w_in_ref[:, D_FF + c0:D_FF + c0 + cw], preferred_element_type=F32)
        a = (_silu(gate) * up).astype(BF16)
        part = jnp.dot(a, w_out_ref[c0:c0 + cw, :], preferred_element_type=F32)
        acc = part if acc is None else acc + part
    return x1 + ada_ref[0, 5:6, :] * acc


def _glu_head(x, ada_ref, g_ref, w_ref, b_ref):
    d = x.shape[-1]
    h = _mod_rmsnorm(x, g_ref[...], ada_ref[0, 0:1, :], ada_ref[0, 1:2, :]).astype(BF16)
    a = jnp.dot(h, w_ref[:, :d], preferred_element_type=F32) + b_ref[:, :d]
    gt = jnp.dot(h, w_ref[:, d:], preferred_element_type=F32) + b_ref[:, d:]
    return a * jax.nn.sigmoid(gt)


def _attn_tail_kernel(o_ref, x_ref, ada_ref, wo_ref, g2_ref, w_in_ref, w_out_ref, out_ref):
    o = jnp.concatenate([o_ref[0, h] for h in range(HEADS_PER_GROUP)], axis=-1)
    y = jnp.dot(o, wo_ref[...], preferred_element_type=F32)
    out_ref[0] = _residual_ffn(x_ref[0], y, ada_ref, g2_ref, w_in_ref, w_out_ref)


def _attn_tail(o, x, ada, w_o, gain2, w_in, w_out):
    b, s, d = x.shape
    tm = min(TOKEN_TILE, s)
    return pl.pallas_call(
        _attn_tail_kernel,
        out_shape=jax.ShapeDtypeStruct((b, s, d), F32),
        grid=(b, s // tm),
        in_specs=[pl.BlockSpec((1, HEADS_PER_GROUP, tm, HEAD_DIM), lambda bi, i: (bi, 0, i, 0)),
                  pl.BlockSpec((1, tm, d), lambda bi, i: (bi, i, 0)),
                  pl.BlockSpec((1, 6, d), lambda bi, i: (bi, 0, 0)),
                  _resident(w_o.shape), _resident((1, d)),
                  _resident(w_in.shape), _resident(w_out.shape)],
        out_specs=pl.BlockSpec((1, tm, d), lambda bi, i: (bi, i, 0)),
        compiler_params=_params(2),
        name="attn_out_ffn",
    )(o, x, ada, w_o, gain2, w_in, w_out)


def _conv_head_kernel(x_ref, ada_ref, g_ref, w_ref, b_ref, u_ref):
    u_ref[0] = _glu_head(x_ref[0], ada_ref, g_ref, w_ref, b_ref)


def _conv_head(x, ada, gain, w_pw1, b_pw1):
    b, s, d = x.shape
    tm = min(PROJ_TOKEN_TILE, s)
    return pl.pallas_call(
        _conv_head_kernel,
        out_shape=jax.ShapeDtypeStruct((b, s, d), F32),
        grid=(b, s // tm),
        in_specs=[pl.BlockSpec((1, tm, d), lambda bi, i: (bi, i, 0)),
                  pl.BlockSpec((1, 6, d), lambda bi, i: (bi, 0, 0)),
                  _resident((1, d)), _resident(w_pw1.shape), _resident((1, 2 * d))],
        out_specs=pl.BlockSpec((1, tm, d), lambda bi, i: (bi, i, 0)),
        compiler_params=_params(2),
        name="conv_pw1_glu",
    )(x, ada, gain, w_pw1, b_pw1)


def _conv_tail_kernel(prev_ref, u_ref, next_ref, x_ref, ada_ref, wdw_ref, bdw_ref, lng_ref,
                      lnb_ref, w2_ref, b2_ref, g2_ref, w_in_ref, w_out_ref, out_ref,
                      win_ref, cv_ref):
    tm, d = u_ref.shape[1], u_ref.shape[2]
    i = pl.program_id(1)
    last = pl.num_programs(1) - 1
    prev = jnp.where(i > 0, prev_ref[0], 0.0)
    nxt = jnp.where(i < last, next_ref[0], 0.0)
    for c in range(d // LANES):
        lanes = slice(c * LANES, (c + 1) * LANES)
        win_ref[c, 0:HALO_ROWS, :] = prev[:, lanes]
        win_ref[c, HALO_ROWS:HALO_ROWS + tm, :] = u_ref[0, :, lanes]
        win_ref[c, HALO_ROWS + tm:, :] = nxt[:, lanes]

    for c in range(d // LANES):
        lanes = slice(c * LANES, (c + 1) * LANES)

        def conv_rows(j, carry, c=c, lanes=lanes):
            r0 = j * (2 * CONV_ROW_CHUNK)
            for parity in range(2):
                acc = None
                for k in range(CONV_WIDTH):
                    start = r0 + (parity + HALO_ROWS - CONV_PAD + k)
                    tap = (win_ref[c, pl.ds(start, CONV_ROW_CHUNK, stride=2), :]
                           * wdw_ref[k:k + 1, lanes])
                    acc = tap if acc is None else acc + tap
                cv_ref[c, pl.ds(r0 + parity, CONV_ROW_CHUNK, stride=2), :] = acc
            return carry

        lax.fori_loop(0, tm // (2 * CONV_ROW_CHUNK), conv_rows, 0, unroll=CONV_LOOP_UNROLL)

    cv = jnp.concatenate([cv_ref[c] for c in range(d // LANES)], axis=1) + bdw_ref[...]
    mu = jnp.mean(cv, axis=-1, keepdims=True)
    xc = cv - mu
    var = jnp.mean(xc * xc, axis=-1, keepdims=True)
    ln = xc * lax.rsqrt(var + EPS) * lng_ref[...] + lnb_ref[...]
    y = jnp.dot(_silu(ln).astype(BF16), w2_ref[...], preferred_element_type=F32) + b2_ref[...]
    out_ref[0] = _residual_ffn(x_ref[0], y, ada_ref, g2_ref, w_in_ref, w_out_ref)


def _conv_tail(u, x, ada, w_dw, b_dw, ln_g, ln_b, w_pw2, b_pw2, gain2, w_in, w_out):
    b, s, d = x.shape
    tm = min(TOKEN_TILE, s)
    assert s % tm == 0 and tm % (2 * CONV_ROW_CHUNK * CONV_LOOP_UNROLL) == 0
    assert HALO_ROWS >= CONV_PAD + 1
    hb = tm // HALO_ROWS
    n_halo = s // HALO_ROWS
    return pl.pallas_call(
        _conv_tail_kernel,
        out_shape=jax.ShapeDtypeStruct((b, s, d), F32),
        grid=(b, s // tm),
        in_specs=[pl.BlockSpec((1, HALO_ROWS, d), lambda bi, i: (bi, jnp.maximum(i * hb - 1, 0), 0)),
                  pl.BlockSpec((1, tm, d), lambda bi, i: (bi, i, 0)),
                  pl.BlockSpec((1, HALO_ROWS, d),
                               lambda bi, i: (bi, jnp.minimum((i + 1) * hb, n_halo - 1), 0)),
                  pl.BlockSpec((1, tm, d), lambda bi, i: (bi, i, 0)),
                  pl.BlockSpec((1, 6, d), lambda bi, i: (bi, 0, 0)),
                  _resident(w_dw.shape), _resident((1, d)), _resident((1, d)), _resident((1, d)),
                  _resident(w_pw2.shape), _resident((1, d)), _resident((1, d)),
                  _resident(w_in.shape), _resident(w_out.shape)],
        out_specs=pl.BlockSpec((1, tm, d), lambda bi, i: (bi, i, 0)),
        scratch_shapes=[pltpu.VMEM((d // LANES, tm + 2 * HALO_ROWS, LANES), F32),
                        pltpu.VMEM((d // LANES, tm, LANES), F32)],
        compiler_params=_params(2),
        name="conv_dw_pw2_ffn",
    )(u, u, u, x, ada, w_dw, b_dw, ln_g, ln_b, w_pw2, b_pw2, gain2, w_in, w_out)


def _trunk(x, ada, biases, p):
    d = x.shape[-1]
    row = lambda v: v.reshape(1, -1)
    for i in range(DEPTH):
        j = i // N_MIXERS
        gain1, gain2 = row(p["norm1_g"][i]), row(p["norm2_g"][i])
        w_in, w_out = p["ffn_w_in"][i], p["ffn_w_out"][i]
        if i % N_MIXERS == 0:
            w4 = p["attn_w_qkv"][j].reshape(d, 3, N_GROUPS, HEADS_PER_GROUP * HEAD_DIM)
            qkvs = [_qkv_proj(x, ada[i], gain1, w4[:, :, g, :].reshape(d, -1),
                              row(p["attn_q_gain"][j]), row(p["attn_k_gain"][j]), dil)
                    for g, (_, dil) in enumerate(DILATED_GROUPS)]
            o = _attention(qkvs, biases)
            x = _attn_tail(o, x, ada[i], p["attn_w_o"][j], gain2, w_in, w_out)
        else:
            u = _conv_head(x, ada[i], gain1, p["conv_w_pw1"][j], row(p["conv_b_pw1"][j]))
            x = _conv_tail(u, x, ada[i], p["conv_w_dw"][j], row(p["conv_b_dw"][j]),
                           row(p["conv_ln_g"][j]), row(p["conv_ln_b"][j]), p["conv_w_pw2"][j],
                           row(p["conv_b_pw2"][j]), gain2, w_in, w_out)
    return x


def kernel(x_prompt, x_sample, c_prompt, c_sample, rel_bias_table, norm1_g, norm2_g, ada_w, ada_b, attn_w_qkv, attn_q_gain, attn_k_gain, attn_w_o, conv_w_pw1, conv_b_pw1, conv_w_dw, conv_b_dw, conv_ln_g, conv_ln_b, conv_w_pw2, conv_b_pw2, ffn_w_in, ffn_w_out):
    p = dict(norm1_g=norm1_g, norm2_g=norm2_g, attn_q_gain=attn_q_gain, attn_k_gain=attn_k_gain,
             conv_b_pw1=conv_b_pw1, conv_w_dw=conv_w_dw, conv_b_dw=conv_b_dw, conv_ln_g=conv_ln_g,
             conv_ln_b=conv_ln_b, conv_b_pw2=conv_b_pw2,
             attn_w_qkv=attn_w_qkv.astype(BF16), attn_w_o=attn_w_o.astype(BF16),
             conv_w_pw1=conv_w_pw1.astype(BF16), conv_w_pw2=conv_w_pw2.astype(BF16),
             ffn_w_in=ffn_w_in.astype(BF16), ffn_w_out=ffn_w_out.astype(BF16))
    nb_p = c_prompt.shape[0]
    d = x_prompt.shape[-1]
    c_all = jnp.concatenate([c_prompt, c_sample], axis=0)
    ada = _ada_all(c_all, ada_w, ada_b).reshape(DEPTH, c_all.shape[0], 6, d)

    bias_cache = {}

    def biases_for(seq_len):
        out = []
        for g, (_, dil) in enumerate(DILATED_GROUPS):
            key = (g,) + _attn_geometry(seq_len, dil)
            if key not in bias_cache:
                bias_cache[key] = _expand_bias(rel_bias_table, _bias_bucket_index(seq_len, dil), g)
            out.append(bias_cache[key])
        return out

    y_prompt = _trunk(x_prompt, ada[:, :nb_p], biases_for(x_prompt.shape[1]), p)
    y_sample = _trunk(x_sample, ada[:, nb_p:], biases_for(x_sample.shape[1]), p)
    return (y_prompt, y_sample)
```

```python
import functools
import math

import jax
import jax.numpy as jnp
from jax import lax
from jax.experimental import pallas as pl
from jax.experimental.pallas import tpu as pltpu

F32 = jnp.float32
BF16 = jnp.bfloat16

D_MODEL = 1024
DEPTH = 4
N_MIXERS = 2
DILATED_GROUPS = ((128, 1), (512, 4), (2048, 16))
N_GROUPS = len(DILATED_GROUPS)
HEADS_PER_GROUP = 8
HEAD_DIM = 128
REL_BUCKETS = 32
REL_MAX_DIST = 1024
CONV_WIDTH = 31
CONV_PAD = (CONV_WIDTH - 1) // 2
D_FF = 2816
NEG_INF = -1e30
EPS = 1e-6
LOG2_E = math.log2(math.e)

HALF = DILATED_GROUPS[0][0] // (2 * DILATED_GROUPS[0][1])
assert all(w // (2 * d) == HALF for w, d in DILATED_GROUPS)

TOKEN_TILE = 512
PROJ_TOKEN_TILE = 1024
QUERY_BLOCK = 128
ATTN_BLOCKS_PER_STEP = 32
HALO_ROWS = 16
CONV_ROW_CHUNK = 64
CONV_LOOP_UNROLL = 2
SUBLANES = 8
BF16_SUBLANES = 16
LANES = 128
FF_CHUNKS = ((0, 768), (768, 768), (1536, 768), (2304, 512))
VMEM_LIMIT_BYTES = 60000 * 1024


def _params(n_axes):
    return pltpu.CompilerParams(dimension_semantics=("parallel",) * n_axes,
                                vmem_limit_bytes=VMEM_LIMIT_BYTES)


def _resident(shape):
    nd = len(shape)
    return pl.BlockSpec(shape, lambda *_: (0,) * nd, pipeline_mode=pl.Buffered(1))


def _silu(x):
    return x * jax.nn.sigmoid(x)


def _mod_rmsnorm(x, gain, shift, scale):
    ms = jnp.mean(x * x, axis=-1, keepdims=True)
    y = x * lax.rsqrt(ms + EPS) * gain
    return y * (1.0 + scale) + shift


def _ada_kernel(c_ref, w_ref, b_ref, o_ref):
    ca = _silu(c_ref[...]).astype(BF16)
    o_ref[0] = jnp.dot(ca, w_ref[0].astype(BF16), preferred_element_type=F32) + b_ref[0]


def _ada_all(c, ada_w, ada_b):
    nb, d = c.shape
    depth, _, width = ada_w.shape
    tn = 1536
    assert width % tn == 0
    return pl.pallas_call(
        _ada_kernel,
        out_shape=jax.ShapeDtypeStruct((depth, nb, width), F32),
        grid=(depth, width // tn),
        in_specs=[pl.BlockSpec((nb, d), lambda i, j: (0, 0)),
                  pl.BlockSpec((1, d, tn), lambda i, j: (i, 0, j)),
                  pl.BlockSpec((1, 1, tn), lambda i, j: (i, 0, j))],
        out_specs=pl.BlockSpec((1, nb, tn), lambda i, j: (i, 0, j)),
        compiler_params=_params(2),
        name="ada_proj",
    )(c, ada_w, ada_b.reshape(depth, 1, width))


def _t5_bucket(rel):
    half = REL_BUCKETS // 2
    max_exact = half // 2
    ret = jnp.where(rel > 0, half, 0)
    n = jnp.abs(rel)
    nf = jnp.maximum(n, 1).astype(F32)
    large = max_exact + (jnp.log(nf / max_exact) / math.log(REL_MAX_DIST / max_exact)
                         * (half - max_exact)).astype(jnp.int32)
    large = jnp.minimum(large, half - 1)
    return ret + jnp.where(n < max_exact, n, large)


def _attn_geometry(seq_len, dil):
    sub_len = seq_len // dil
    qb = min(QUERY_BLOCK, sub_len)
    kw = min(qb + 2 * HALF, sub_len)
    nblk = sub_len // qb
    assert sub_len * dil == seq_len and nblk * qb == sub_len and qb % HALF == 0
    return sub_len, qb, kw, nblk


def _key_offset(n, qb, kw, sub_len):
    return min(max(n * qb - HALF, 0), sub_len - kw)


def _bias_bucket_index(seq_len, dil):
    sub_len, qb, kw, nblk = _attn_geometry(seq_len, dil)
    slots = []
    for n in (0, min(1, nblk - 1), nblk - 1):
        delta = n * qb - _key_offset(n, qb, kw, sub_len)
        rel = jnp.arange(kw)[None, :] - jnp.arange(qb)[:, None] - delta
        slots.append(jnp.where(jnp.abs(rel) <= HALF, _t5_bucket(rel * dil), -1))
    return jnp.stack(slots).astype(jnp.int32)


def _bias_kernel(tab_ref, idx_ref, o_ref, *, head0):
    h = pl.program_id(0)
    idx = idx_ref[...]
    acc = jnp.full(idx.shape, NEG_INF, F32)
    for b in range(REL_BUCKETS):
        acc = jnp.where(idx == b, tab_ref[b, head0 + h], acc)
    o_ref[0] = acc * LOG2_E


def _expand_bias(rel_table, idx, group):
    _, qb, kw = idx.shape
    return pl.pallas_call(
        functools.partial(_bias_kernel, head0=group * HEADS_PER_GROUP),
        out_shape=jax.ShapeDtypeStruct((HEADS_PER_GROUP, 3, qb, kw), F32),
        grid=(HEADS_PER_GROUP,),
        in_specs=[pl.BlockSpec(memory_space=pltpu.SMEM),
                  pl.BlockSpec((3, qb, kw), lambda h: (0, 0, 0))],
        out_specs=pl.BlockSpec((1, 3, qb, kw), lambda h: (h, 0, 0, 0)),
        compiler_params=_params(1),
        name="rel_bias_expand",
    )(rel_table, idx)


def _qkv_kernel(x_ref, ada_ref, g_ref, w_ref, qg_ref, kg_ref, o_ref, *scratch, dil, two_level):
    tm, d = x_ref.shape[1], x_ref.shape[2]
    sub = tm // dil
    h = _mod_rmsnorm(x_ref[0], g_ref[...], ada_ref[0, 0:1, :], ada_ref[0, 1:2, :])
    if dil == 1:
        h = h.astype(BF16)
    elif two_level:
        step = math.isqrt(dil)
        slab_ref, perm_ref, mid_ref = scratch
        part = tm // step
        for c in range(d // LANES):
            slab_ref[c] = h[:, c * LANES:(c + 1) * LANES]
        for c in range(d // LANES):
            for r0 in range(step):
                mid_ref[c, r0 * part:(r0 + 1) * part, :] = slab_ref[c, pl.ds(r0, part, stride=step), :]
        for c in range(d // LANES):
            for r in range(dil):
                r0, r1 = r % step, r // step
                perm_ref[r * sub:(r + 1) * sub, c * LANES:(c + 1) * LANES] = (
                    mid_ref[c, pl.ds(r0 * part + r1, sub, stride=step), :].astype(BF16))
        h = perm_ref[...]
    else:
        slab_ref, perm_ref = scratch
        for c in range(d // LANES):
            slab_ref[c] = h[:, c * LANES:(c + 1) * LANES]
        for c in range(d // LANES):
            for r in range(dil):
                perm_ref[r * sub:(r + 1) * sub, c * LANES:(c + 1) * LANES] = (
                    slab_ref[c, pl.ds(r, sub, stride=dil), :].astype(BF16))
        h = perm_ref[...]
    width = HEADS_PER_GROUP * HEAD_DIM
    for t in range(3):
        acc = jnp.dot(h, w_ref[:, t * width:(t + 1) * width], preferred_element_type=F32)
        for hd in range(HEADS_PER_GROUP):
            c = acc[:, hd * HEAD_DIM:(hd + 1) * HEAD_DIM]
            if t < 2:
                ms = jnp.mean(c * c, axis=-1, keepdims=True)
                c = c * lax.rsqrt(ms + EPS) * (qg_ref[...] if t == 0 else kg_ref[...])
            if t == 0:
                c = c * (HEAD_DIM ** -0.5 * LOG2_E)
            o_ref[0, t, hd] = c.astype(BF16).reshape(dil, sub, HEAD_DIM)


def _qkv_proj(x, ada, gain, w, q_gain, k_gain, dil):
    b, s, d = x.shape
    sub_len = s // dil
    tm = min(PROJ_TOKEN_TILE, s)
    assert s % tm == 0 and tm % (dil * BF16_SUBLANES) == 0
    width = 3 * HEADS_PER_GROUP * HEAD_DIM
    scratch = [] if dil == 1 else [pltpu.VMEM((d // LANES, tm, LANES), F32),
                                   pltpu.VMEM((tm, d), BF16)]
    two_level = dil > 4 and math.isqrt(dil) ** 2 == dil
    if two_level:
        scratch.append(pltpu.VMEM((d // LANES, tm, LANES), F32))
    return pl.pallas_call(
        functools.partial(_qkv_kernel, dil=dil, two_level=two_level),
        out_shape=jax.ShapeDtypeStruct((b, 3, HEADS_PER_GROUP, dil, sub_len, HEAD_DIM), BF16),
        grid=(b, s // tm),
        in_specs=[pl.BlockSpec((1, tm, d), lambda bi, i: (bi, i, 0)),
                  pl.BlockSpec((1, 6, d), lambda bi, i: (bi, 0, 0)),
                  _resident((1, d)),
                  _resident((d, width)),
                  _resident((1, HEAD_DIM)),
                  _resident((1, HEAD_DIM))],
        out_specs=pl.BlockSpec((1, 3, HEADS_PER_GROUP, dil, tm // dil, HEAD_DIM),
                               lambda bi, i: (bi, 0, 0, 0, i, 0)),
        scratch_shapes=scratch,
        compiler_params=_params(2),
        name=f"qkv_proj_d{dil}",
    )(x, ada, gain, w, q_gain, k_gain)


def _attn_kernel(qkv0_ref, qkv1_ref, qkv2_ref, b0_ref, b1_ref, b2_ref, o_ref, og_ref, mg_ref,
                 lg_ref, *, seq_len):
    for g, (qkv_ref, bias_ref) in enumerate(((qkv0_ref, b0_ref), (qkv1_ref, b1_ref),
                                              (qkv2_ref, b2_ref))):
        dil = DILATED_GROUPS[g][1]
        sub_len, qb, kw, nblk = _attn_geometry(seq_len, dil)

        def block(r, n, g=g, dil=dil, sub_len=sub_len, qb=qb, kw=kw, nblk=nblk,
                  qkv_ref=qkv_ref, bias_ref=bias_ref):
            q0 = pl.multiple_of(r * sub_len + n * qb, qb)
            koff = jnp.clip(n * qb - HALF, 0, sub_len - kw)
            k0 = pl.multiple_of(r * sub_len + koff, HALF)
            q = qkv_ref[0, 0, 0, pl.ds(q0, qb), :]
            k = qkv_ref[0, 1, 0, pl.ds(k0, kw), :]
            v = qkv_ref[0, 2, 0, pl.ds(k0, kw), :]
            slot = jnp.where(n == 0, 0, jnp.where(n == nblk - 1, 2, 1))
            s = lax.dot_general(q, k, (((1,), (1,)), ((), ())), preferred_element_type=F32)
            s = s + bias_ref[0, slot]
            m = jnp.max(s, axis=-1, keepdims=True)
            p = jnp.exp2(s - m).astype(BF16)
            v_ones = jnp.concatenate([v, jnp.ones((kw, HEAD_DIM), BF16)], axis=1)
            o_l = jnp.dot(p, v_ones, preferred_element_type=F32)
            start = r + n * (qb * dil)
            rows = pl.ds(start, qb) if dil == 1 else pl.ds(start, qb, stride=dil)
            og_ref[g, rows, :] = o_l[:, :HEAD_DIM]
            mg_ref[g, rows, :] = jnp.broadcast_to(m, (qb, HEAD_DIM))
            lg_ref[g, rows, :] = o_l[:, HEAD_DIM:]

        total = dil * nblk
        par = math.gcd(total, ATTN_BLOCKS_PER_STEP)

        def step(it, carry, block=block, nblk=nblk, par=par):
            for u in range(par):
                idx = it * par + u
                block(lax.div(idx, nblk), lax.rem(idx, nblk))
            return carry

        lax.fori_loop(0, total // par, step, 0)

    rc = 256

    def merge(c, carry):
        rows = pl.ds(pl.multiple_of(c * rc, rc), rc)
        ms = [mg_ref[g, rows, :] for g in range(N_GROUPS)]
        mx = jnp.maximum(jnp.maximum(ms[0], ms[1]), ms[2])
        num = den = None
        for g in range(N_GROUPS):
            w = jnp.exp2(ms[g] - mx)
            num = w * og_ref[g, rows, :] if num is None else num + w * og_ref[g, rows, :]
            den = w * lg_ref[g, rows, :] if den is None else den + w * lg_ref[g, rows, :]
        o_ref[0, 0, rows, :] = (num / den).astype(BF16)
        return carry

    lax.fori_loop(0, seq_len // rc, merge, 0)


def _attention(qkvs, biases):
    b = qkvs[0].shape[0]
    s = qkvs[0].shape[3] * qkvs[0].shape[4]
    qkvs = [t.reshape(b, 3, HEADS_PER_GROUP, s, HEAD_DIM) for t in qkvs]
    qkv_spec = pl.BlockSpec((1, 3, 1, s, HEAD_DIM), lambda bi, h: (bi, 0, h, 0, 0))
    bias_specs = [pl.BlockSpec((1,) + t.shape[1:], lambda bi, h: (h, 0, 0, 0)) for t in biases]
    return pl.pallas_call(
        functools.partial(_attn_kernel, seq_len=s),
        out_shape=jax.ShapeDtypeStruct((b, HEADS_PER_GROUP, s, HEAD_DIM), BF16),
        grid=(b, HEADS_PER_GROUP),
        in_specs=[qkv_spec] * N_GROUPS + bias_specs,
        out_specs=pl.BlockSpec((1, 1, s, HEAD_DIM), lambda bi, h: (bi, h, 0, 0)),
        scratch_shapes=[pltpu.VMEM((N_GROUPS, s, HEAD_DIM), F32)] * 3,
        compiler_params=_params(2),
        name="dilated_attention",
    )(*qkvs, *biases)


def _residual_ffn(x, y, ada_ref, g2_ref, w_in_ref, w_out_ref):
    x1 = x + ada_ref[0, 2:3, :] * y
    h = _mod_rmsnorm(x1, g2_ref[...], ada_ref[0, 3:4, :], ada_ref[0, 4:5, :]).astype(BF16)
    acc = None
    for c0, cw in FF_CHUNKS:
        gate = jnp.dot(h, w_in_ref[:, c0:c0 + cw], preferred_element_type=F32)
        up = jnp.dot(h, w_in_ref[:, D_FF + c0:D_FF + c0 + cw], preferred_element_type=F32)
        a = (_silu(gate) * up).astype(BF16)
        part = jnp.dot(a, w_out_ref[c0:c0 + cw, :], preferred_element_type=F32)
        acc = part if acc is None else acc + part
    return x1 + ada_ref[0, 5:6, :] * acc


def _glu_head(x, ada_ref, g_ref, w_ref, b_ref):
    d = x.shape[-1]
    h = _mod_rmsnorm(x, g_ref[...], ada_ref[0, 0:1, :], ada_ref[0, 1:2, :]).astype(BF16)
    a = jnp.dot(h, w_ref[:, :d], preferred_element_type=F32) + b_ref[:, :d]
    gt = jnp.dot(h, w_ref[:, d:], preferred_element_type=F32) + b_ref[:, d:]
    return a * jax.nn.sigmoid(gt)


def _attn_tail_kernel(o_ref, x_ref, ada_ref, wo_ref, g2_ref, w_in_ref, w_out_ref, out_ref):
    o = jnp.concatenate([o_ref[0, h] for h in range(HEADS_PER_GROUP)], axis=-1)
    y = jnp.dot(o, wo_ref[...], preferred_element_type=F32)
    out_ref[0] = _residual_ffn(x_ref[0], y, ada_ref, g2_ref, w_in_ref, w_out_ref)


def _attn_tail(o, x, ada, w_o, gain2, w_in, w_out):
    b, s, d = x.shape
    tm = min(TOKEN_TILE, s)
    return pl.pallas_call(
        _attn_tail_kernel,
        out_shape=jax.ShapeDtypeStruct((b, s, d), F32),
        grid=(b, s // tm),
        in_specs=[pl.BlockSpec((1, HEADS_PER_GROUP, tm, HEAD_DIM), lambda bi, i: (bi, 0, i, 0)),
                  pl.BlockSpec((1, tm, d), lambda bi, i: (bi, i, 0)),
                  pl.BlockSpec((1, 6, d), lambda bi, i: (bi, 0, 0)),
                  _resident(w_o.shape), _resident((1, d)),
                  _resident(w_in.shape), _resident(w_out.shape)],
        out_specs=pl.BlockSpec((1, tm, d), lambda bi, i: (bi, i, 0)),
        compiler_params=_params(2),
        name="attn_out_ffn",
    )(o, x, ada, w_o, gain2, w_in, w_out)


def _conv_head_kernel(x_ref, ada_ref, g_ref, w_ref, b_ref, u_ref):
    u_ref[0] = _glu_head(x_ref[0], ada_ref, g_ref, w_ref, b_ref)


def _conv_head(x, ada, gain, w_pw1, b_pw1):
    b, s, d = x.shape
    tm = min(PROJ_TOKEN_TILE, s)
    return pl.pallas_call(
        _conv_head_kernel,
        out_shape=jax.ShapeDtypeStruct((b, s, d), F32),
        grid=(b, s // tm),
        in_specs=[pl.BlockSpec((1, tm, d), lambda bi, i: (bi, i, 0)),
                  pl.BlockSpec((1, 6, d), lambda bi, i: (bi, 0, 0)),
                  _resident((1, d)), _resident(w_pw1.shape), _resident((1, 2 * d))],
        out_specs=pl.BlockSpec((1, tm, d), lambda bi, i: (bi, i, 0)),
        compiler_params=_params(2),
        name="conv_pw1_glu",
    )(x, ada, gain, w_pw1, b_pw1)


def _conv_tail_kernel(prev_ref, u_ref, next_ref, x_ref, ada_ref, wdw_ref, bdw_ref, lng_ref,
                      lnb_ref, w2_ref, b2_ref, g2_ref, w_in_ref, w_out_ref, out_ref,
                      win_ref, cv_ref):
    tm, d = u_ref.shape[1], u_ref.shape[2]
    i = pl.program_id(1)
    last = pl.num_programs(1) - 1
    prev = jnp.where(i > 0, prev_ref[0], 0.0)
    nxt = jnp.where(i < last, next_ref[0], 0.0)
    for c in range(d // LANES):
        lanes = slice(c * LANES, (c + 1) * LANES)
        win_ref[c, 0:HALO_ROWS, :] = prev[:, lanes]
        win_ref[c, HALO_ROWS:HALO_ROWS + tm, :] = u_ref[0, :, lanes]
        win_ref[c, HALO_ROWS + tm:, :] = nxt[:, lanes]

    for c in range(d // LANES):
        lanes = slice(c * LANES, (c + 1) * LANES)

        def conv_rows(j, carry, c=c, lanes=lanes):
            r0 = j * (2 * CONV_ROW_CHUNK)
            for parity in range(2):
                acc = None
                for k in range(CONV_WIDTH):
                    start = r0 + (parity + HALO_ROWS - CONV_PAD + k)
                    tap = (win_ref[c, pl.ds(start, CONV_ROW_CHUNK, stride=2), :]
                           * wdw_ref[k:k + 1, lanes])
                    acc = tap if acc is None else acc + tap
                cv_ref[c, pl.ds(r0 + parity, CONV_ROW_CHUNK, stride=2), :] = acc
            return carry

        lax.fori_loop(0, tm // (2 * CONV_ROW_CHUNK), conv_rows, 0, unroll=CONV_LOOP_UNROLL)

    cv = jnp.concatenate([cv_ref[c] for c in range(d // LANES)], axis=1) + bdw_ref[...]
    mu = jnp.mean(cv, axis=-1, keepdims=True)
    xc = cv - mu
    var = jnp.mean(xc * xc, axis=-1, keepdims=True)
    ln = xc * lax.rsqrt(var + EPS) * lng_ref[...] + lnb_ref[...]
    y = jnp.dot(_silu(ln).astype(BF16), w2_ref[...], preferred_element_type=F32) + b2_ref[...]
    out_ref[0] = _residual_ffn(x_ref[0], y, ada_ref, g2_ref, w_in_ref, w_out_ref)


def _conv_tail(u, x, ada, w_dw, b_dw, ln_g, ln_b, w_pw2, b_pw2, gain2, w_in, w_out):
    b, s, d = x.shape
    tm = min(TOKEN_TILE, s)
    assert s % tm == 0 and tm % (2 * CONV_ROW_CHUNK * CONV_LOOP_UNROLL) == 0
    assert HALO_ROWS >= CONV_PAD + 1
    hb = tm // HALO_ROWS
    n_halo = s // HALO_ROWS
    return pl.pallas_call(
        _conv_tail_kernel,
        out_shape=jax.ShapeDtypeStruct((b, s, d), F32),
        grid=(b, s // tm),
        in_specs=[pl.BlockSpec((1, HALO_ROWS, d), lambda bi, i: (bi, jnp.maximum(i * hb - 1, 0), 0)),
                  pl.BlockSpec((1, tm, d), lambda bi, i: (bi, i, 0)),
                  pl.BlockSpec((1, HALO_ROWS, d),
                               lambda bi, i: (bi, jnp.minimum((i + 1) * hb, n_halo - 1), 0)),
                  pl.BlockSpec((1, tm, d), lambda bi, i: (bi, i, 0)),
                  pl.BlockSpec((1, 6, d), lambda bi, i: (bi, 0, 0)),
                  _resident(w_dw.shape), _resident((1, d)), _resident((1, d)), _resident((1, d)),
                  _resident(w_pw2.shape), _resident((1, d)), _resident((1, d)),
                  _resident(w_in.shape), _resident(w_out.shape)],
        out_specs=pl.BlockSpec((1, tm, d), lambda bi, i: (bi, i, 0)),
        scratch_shapes=[pltpu.VMEM((d // LANES, tm + 2 * HALO_ROWS, LANES), F32),
                        pltpu.VMEM((d // LANES, tm, LANES), F32)],
        compiler_params=_params(2),
        name="conv_dw_pw2_ffn",
    )(u, u, u, x, ada, w_dw, b_dw, ln_g, ln_b, w_pw2, b_pw2, gain2, w_in, w_out)


def _trunk(x, ada, biases, p):
    d = x.shape[-1]
    row = lambda v: v.reshape(1, -1)
    for i in range(DEPTH):
        j = i // N_MIXERS
        gain1, gain2 = row(p["norm1_g"][i]), row(p["norm2_g"][i])
        w_in, w_out = p["ffn_w_in"][i], p["ffn_w_out"][i]
        if i % N_MIXERS == 0:
            w4 = p["attn_w_qkv"][j].reshape(d, 3, N_GROUPS, HEADS_PER_GROUP * HEAD_DIM)
            qkvs = [_qkv_proj(x, ada[i], gain1, w4[:, :, g, :].reshape(d, -1),
                              row(p["attn_q_gain"][j]), row(p["attn_k_gain"][j]), dil)
                    for g, (_, dil) in enumerate(DILATED_GROUPS)]
            o = _attention(qkvs, biases)
            x = _attn_tail(o, x, ada[i], p["attn_w_o"][j], gain2, w_in, w_out)
        else:
            u = _conv_head(x, ada[i], gain1, p["conv_w_pw1"][j], row(p["conv_b_pw1"][j]))
            x = _conv_tail(u, x, ada[i], p["conv_w_dw"][j], row(p["conv_b_dw"][j]),
                           row(p["conv_ln_g"][j]), row(p["conv_ln_b"][j]), p["conv_w_pw2"][j],
                           row(p["conv_b_pw2"][j]), gain2, w_in, w_out)
    return x


def kernel(x_prompt, x_sample, c_prompt, c_sample, rel_bias_table, norm1_g, norm2_g, ada_w, ada_b, attn_w_qkv, attn_q_gain, attn_k_gain, attn_w_o, conv_w_pw1, conv_b_pw1, conv_w_dw, conv_b_dw, conv_ln_g, conv_ln_b, conv_w_pw2, conv_b_pw2, ffn_w_in, ffn_w_out):
    p = dict(norm1_g=norm1_g, norm2_g=norm2_g, attn_q_gain=attn_q_gain, attn_k_gain=attn_k_gain,
             conv_b_pw1=conv_b_pw1, conv_w_dw=conv_w_dw, conv_b_dw=conv_b_dw, conv_ln_g=conv_ln_g,
             conv_ln_b=conv_ln_b, conv_b_pw2=conv_b_pw2,
             attn_w_qkv=attn_w_qkv.astype(BF16), attn_w_o=attn_w_o.astype(BF16),
             conv_w_pw1=conv_w_pw1.astype(BF16), conv_w_pw2=conv_w_pw2.astype(BF16),
             ffn_w_in=ffn_w_in.astype(BF16), ffn_w_out=ffn_w_out.astype(BF16))
    nb_p = c_prompt.shape[0]
    d = x_prompt.shape[-1]
    c_all = jnp.concatenate([c_prompt, c_sample], axis=0)
    ada = _ada_all(c_all, ada_w, ada_b).reshape(DEPTH, c_all.shape[0], 6, d)

    bias_cache = {}

    def biases_for(seq_len):
        out = []
        for g, (_, dil) in enumerate(DILATED_GROUPS):
            key = (g,) + _attn_geometry(seq_len, dil)
            if key not in bias_cache:
                bias_cache[key] = _expand_bias(rel_bias_table, _bias_bucket_index(seq_len, dil), g)
            out.append(bias_cache[key])
        return out

    y_prompt = _trunk(x_prompt, ada[:, :nb_p], biases_for(x_prompt.shape[1]), p)
    y_sample = _trunk(x_sample, ada[:, nb_p:], biases_for(x_sample.shape[1]), p)
    return (y_prompt, y_sample)
```

```python
import functools
import math

import jax
import jax.numpy as jnp
from jax import lax
from jax.experimental import pallas as pl
from jax.experimental.pallas import tpu as pltpu

F32 = jnp.float32
BF16 = jnp.bfloat16

D_MODEL = 1024
DEPTH = 4
N_MIXERS = 2
DILATED_GROUPS = ((128, 1), (512, 4), (2048, 16))
N_GROUPS = len(DILATED_GROUPS)
HEADS_PER_GROUP = 8
HEAD_DIM = 128
REL_BUCKETS = 32
REL_MAX_DIST = 1024
CONV_WIDTH = 31
CONV_PAD = (CONV_WIDTH - 1) // 2
D_FF = 2816
NEG_INF = -1e30
EPS = 1e-6
LOG2_E = math.log2(math.e)

HALF = DILATED_GROUPS[0][0] // (2 * DILATED_GROUPS[0][1])
assert all(w // (2 * d) == HALF for w, d in DILATED_GROUPS)

TOKEN_TILE = 512
PROJ_TOKEN_TILE = 1024
QUERY_BLOCK = 128
ATTN_ROWS_PER_STEP = 4096
ATTN_BLOCKS_PER_STEP = 32
HALO_ROWS = 16
CONV_ROW_CHUNK = 64
CONV_LOOP_UNROLL = 2
SUBLANES = 8
BF16_SUBLANES = 16
LANES = 128
FF_CHUNKS = ((0, 768), (768, 768), (1536, 768), (2304, 512))
VMEM_LIMIT_BYTES = 60000 * 1024


def _params(n_axes):
    return pltpu.CompilerParams(dimension_semantics=("parallel",) * n_axes,
                                vmem_limit_bytes=VMEM_LIMIT_BYTES)


def _resident(shape):
    nd = len(shape)
    return pl.BlockSpec(shape, lambda *_: (0,) * nd, pipeline_mode=pl.Buffered(1))


def _silu(x):
    return x * jax.nn.sigmoid(x)


def _mod_rmsnorm(x, gain, shift, scale):
    ms = jnp.mean(x * x, axis=-1, keepdims=True)
    y = x * lax.rsqrt(ms + EPS) * gain
    return y * (1.0 + scale) + shift


def _ada_kernel(c_ref, w_ref, b_ref, o_ref):
    ca = _silu(c_ref[...]).astype(BF16)
    o_ref[0] = jnp.dot(ca, w_ref[0].astype(BF16), preferred_element_type=F32) + b_ref[0]


def _ada_all(c, ada_w, ada_b):
    nb, d = c.shape
    depth, _, width = ada_w.shape
    tn = 1536
    assert width % tn == 0
    return pl.pallas_call(
        _ada_kernel,
        out_shape=jax.ShapeDtypeStruct((depth, nb, width), F32),
        grid=(depth, width // tn),
        in_specs=[pl.BlockSpec((nb, d), lambda i, j: (0, 0)),
                  pl.BlockSpec((1, d, tn), lambda i, j: (i, 0, j)),
                  pl.BlockSpec((1, 1, tn), lambda i, j: (i, 0, j))],
        out_specs=pl.BlockSpec((1, nb, tn), lambda i, j: (i, 0, j)),
        compiler_params=_params(2),
        name="ada_proj",
    )(c, ada_w, ada_b.reshape(depth, 1, width))


def _t5_bucket(rel):
    half = REL_BUCKETS // 2
    max_exact = half // 2
    ret = jnp.where(rel > 0, half, 0)
    n = jnp.abs(rel)
    nf = jnp.maximum(n, 1).astype(F32)
    large = max_exact + (jnp.log(nf / max_exact) / math.log(REL_MAX_DIST / max_exact)
                         * (half - max_exact)).astype(jnp.int32)
    large = jnp.minimum(large, half - 1)
    return ret + jnp.where(n < max_exact, n, large)


def _attn_geometry(seq_len, dil):
    sub_len = seq_len // dil
    qb = min(QUERY_BLOCK, sub_len)
    kw = min(qb + 2 * HALF, sub_len)
    nblk = sub_len // qb
    assert sub_len * dil == seq_len and nblk * qb == sub_len and qb % HALF == 0
    return sub_len, qb, kw, nblk


def _key_offset(n, qb, kw, sub_len):
    return min(max(n * qb - HALF, 0), sub_len - kw)


def _bias_bucket_index(seq_len, dil):
    sub_len, qb, kw, nblk = _attn_geometry(seq_len, dil)
    slots = []
    for n in (0, min(1, nblk - 1), nblk - 1):
        delta = n * qb - _key_offset(n, qb, kw, sub_len)
        rel = jnp.arange(kw)[None, :] - jnp.arange(qb)[:, None] - delta
        slots.append(jnp.where(jnp.abs(rel) <= HALF, _t5_bucket(rel * dil), -1))
    return jnp.stack(slots).astype(jnp.int32)


def _bias_kernel(tab_ref, idx_ref, o_ref, *, head0):
    h = pl.program_id(0)
    idx = idx_ref[...]
    acc = jnp.full(idx.shape, NEG_INF, F32)
    for b in range(REL_BUCKETS):
        acc = jnp.where(idx == b, tab_ref[b, head0 + h], acc)
    o_ref[0] = acc * LOG2_E


def _expand_bias(rel_table, idx, group):
    _, qb, kw = idx.shape
    return pl.pallas_call(
        functools.partial(_bias_kernel, head0=group * HEADS_PER_GROUP),
        out_shape=jax.ShapeDtypeStruct((HEADS_PER_GROUP, 3, qb, kw), F32),
        grid=(HEADS_PER_GROUP,),
        in_specs=[pl.BlockSpec(memory_space=pltpu.SMEM),
                  pl.BlockSpec((3, qb, kw), lambda h: (0, 0, 0))],
        out_specs=pl.BlockSpec((1, 3, qb, kw), lambda h: (h, 0, 0, 0)),
        compiler_params=_params(1),
        name="rel_bias_expand",
    )(rel_table, idx)


def _qkv_kernel(x_ref, ada_ref, g_ref, w_ref, qg_ref, kg_ref, o_ref, *scratch, dil, two_level):
    tm, d = x_ref.shape[1], x_ref.shape[2]
    sub = tm // dil
    h = _mod_rmsnorm(x_ref[0], g_ref[...], ada_ref[0, 0:1, :], ada_ref[0, 1:2, :])
    if dil == 1:
        h = h.astype(BF16)
    elif two_level:
        step = math.isqrt(dil)
        slab_ref, perm_ref, mid_ref = scratch
        part = tm // step
        for c in range(d // LANES):
            slab_ref[c] = h[:, c * LANES:(c + 1) * LANES]
        for c in range(d // LANES):
            for r0 in range(step):
                mid_ref[c, r0 * part:(r0 + 1) * part, :] = slab_ref[c, pl.ds(r0, part, stride=step), :]
        for c in range(d // LANES):
            for r in range(dil):
                r0, r1 = r % step, r // step
                perm_ref[r * sub:(r + 1) * sub, c * LANES:(c + 1) * LANES] = (
                    mid_ref[c, pl.ds(r0 * part + r1, sub, stride=step), :].astype(BF16))
        h = perm_ref[...]
    else:
        slab_ref, perm_ref = scratch
        for c in range(d // LANES):
            slab_ref[c] = h[:, c * LANES:(c + 1) * LANES]
        for c in range(d // LANES):
            for r in range(dil):
                perm_ref[r * sub:(r + 1) * sub, c * LANES:(c + 1) * LANES] = (
                    slab_ref[c, pl.ds(r, sub, stride=dil), :].astype(BF16))
        h = perm_ref[...]
    width = HEADS_PER_GROUP * HEAD_DIM
    for t in range(3):
        acc = jnp.dot(h, w_ref[:, t * width:(t + 1) * width], preferred_element_type=F32)
        for hd in range(HEADS_PER_GROUP):
            c = acc[:, hd * HEAD_DIM:(hd + 1) * HEAD_DIM]
            if t < 2:
                ms = jnp.mean(c * c, axis=-1, keepdims=True)
                c = c * lax.rsqrt(ms + EPS) * (qg_ref[...] if t == 0 else kg_ref[...])
            if t == 0:
                c = c * (HEAD_DIM ** -0.5 * LOG2_E)
            o_ref[0, t, hd] = c.astype(BF16).reshape(dil, sub, HEAD_DIM)


def _qkv_proj(x, ada, gain, w, q_gain, k_gain, dil):
    b, s, d = x.shape
    sub_len = s // dil
    tm = min(PROJ_TOKEN_TILE, s)
    assert s % tm == 0 and tm % (dil * BF16_SUBLANES) == 0
    width = 3 * HEADS_PER_GROUP * HEAD_DIM
    scratch = [] if dil == 1 else [pltpu.VMEM((d // LANES, tm, LANES), F32),
                                   pltpu.VMEM((tm, d), BF16)]
    two_level = dil > 4 and math.isqrt(dil) ** 2 == dil
    if two_level:
        scratch.append(pltpu.VMEM((d // LANES, tm, LANES), F32))
    return pl.pallas_call(
        functools.partial(_qkv_kernel, dil=dil, two_level=two_level),
        out_shape=jax.ShapeDtypeStruct((b, 3, HEADS_PER_GROUP, dil, sub_len, HEAD_DIM), BF16),
        grid=(b, s // tm),
        in_specs=[pl.BlockSpec((1, tm, d), lambda bi, i: (bi, i, 0)),
                  pl.BlockSpec((1, 6, d), lambda bi, i: (bi, 0, 0)),
                  _resident((1, d)),
                  _resident((d, width)),
                  _resident((1, HEAD_DIM)),
                  _resident((1, HEAD_DIM))],
        out_specs=pl.BlockSpec((1, 3, HEADS_PER_GROUP, dil, tm // dil, HEAD_DIM),
                               lambda bi, i: (bi, 0, 0, 0, i, 0)),
        scratch_shapes=scratch,
        compiler_params=_params(2),
        name=f"qkv_proj_d{dil}",
    )(x, ada, gain, w, q_gain, k_gain)


def _attn_kernel(qkv0_ref, qkv1_ref, qkv2_ref, b0_ref, b1_ref, b2_ref, o_ref, og_ref, mg_ref,
                 lg_ref, *, seq_len):
    heads = o_ref.shape[1]
    if heads == 1:
        _attn_one_head(0, qkv0_ref, qkv1_ref, qkv2_ref, b0_ref, b1_ref, b2_ref, o_ref, og_ref,
                       mg_ref, lg_ref, seq_len)
    else:
        def head(hh, carry):
            _attn_one_head(hh, qkv0_ref, qkv1_ref, qkv2_ref, b0_ref, b1_ref, b2_ref, o_ref, og_ref,
                           mg_ref, lg_ref, seq_len)
            return carry
        lax.fori_loop(0, heads, head, 0)


def _attn_one_head(hh, qkv0_ref, qkv1_ref, qkv2_ref, b0_ref, b1_ref, b2_ref, o_ref, og_ref, mg_ref,
                   lg_ref, seq_len):
    for g, (qkv_ref, bias_ref) in enumerate(((qkv0_ref, b0_ref), (qkv1_ref, b1_ref),
                                              (qkv2_ref, b2_ref))):
        dil = DILATED_GROUPS[g][1]
        sub_len, qb, kw, nblk = _attn_geometry(seq_len, dil)

        def block(r, n, g=g, dil=dil, sub_len=sub_len, qb=qb, kw=kw, nblk=nblk,
                  qkv_ref=qkv_ref, bias_ref=bias_ref):
            q0 = pl.multiple_of(r * sub_len + n * qb, qb)
            koff = jnp.clip(n * qb - HALF, 0, sub_len - kw)
            k0 = pl.multiple_of(r * sub_len + koff, HALF)
            q = qkv_ref[0, 0, hh, pl.ds(q0, qb), :]
            k = qkv_ref[0, 1, hh, pl.ds(k0, kw), :]
            v = qkv_ref[0, 2, hh, pl.ds(k0, kw), :]
            slot = jnp.where(n == 0, 0, jnp.where(n == nblk - 1, 2, 1))
            s = lax.dot_general(q, k, (((1,), (1,)), ((), ())), preferred_element_type=F32)
            s = s + bias_ref[hh, slot]
            m = jnp.max(s, axis=-1, keepdims=True)
            p = jnp.exp2(s - m).astype(BF16)
            v_ones = jnp.concatenate([v, jnp.ones((kw, HEAD_DIM), BF16)], axis=1)
            o_l = jnp.dot(p, v_ones, preferred_element_type=F32)
            start = r + n * (qb * dil)
            rows = pl.ds(start, qb) if dil == 1 else pl.ds(start, qb, stride=dil)
            og_ref[g, rows, :] = o_l[:, :HEAD_DIM]
            mg_ref[g, rows, :] = jnp.broadcast_to(m, (qb, HEAD_DIM))
            lg_ref[g, rows, :] = o_l[:, HEAD_DIM:]

        total = dil * nblk
        par = math.gcd(total, ATTN_BLOCKS_PER_STEP)

        def step(it, carry, block=block, nblk=nblk, par=par):
            for u in range(par):
                idx = it * par + u
                block(lax.div(idx, nblk), lax.rem(idx, nblk))
            return carry

        lax.fori_loop(0, total // par, step, 0)

    rc = 256

    def merge(c, carry):
        rows = pl.ds(pl.multiple_of(c * rc, rc), rc)
        ms = [mg_ref[g, rows, :] for g in range(N_GROUPS)]
        mx = jnp.maximum(jnp.maximum(ms[0], ms[1]), ms[2])
        num = den = None
        for g in range(N_GROUPS):
            w = jnp.exp2(ms[g] - mx)
            num = w * og_ref[g, rows, :] if num is None else num + w * og_ref[g, rows, :]
            den = w * lg_ref[g, rows, :] if den is None else den + w * lg_ref[g, rows, :]
        o_ref[0, hh, rows, :] = (num / den).astype(BF16)
        return carry

    lax.fori_loop(0, seq_len // rc, merge, 0)


def _attention(qkvs, biases):
    b = qkvs[0].shape[0]
    s = qkvs[0].shape[3] * qkvs[0].shape[4]
    qkvs = [t.reshape(b, 3, HEADS_PER_GROUP, s, HEAD_DIM) for t in qkvs]
    hb = max(1, min(HEADS_PER_GROUP, ATTN_ROWS_PER_STEP // s))
    assert HEADS_PER_GROUP % hb == 0
    qkv_spec = pl.BlockSpec((1, 3, hb, s, HEAD_DIM), lambda h, bi: (bi, 0, h, 0, 0))
    bias_specs = [pl.BlockSpec((hb,) + t.shape[1:], lambda h, bi: (h, 0, 0, 0)) for t in biases]
    return pl.pallas_call(
        functools.partial(_attn_kernel, seq_len=s),
        out_shape=jax.ShapeDtypeStruct((b, HEADS_PER_GROUP, s, HEAD_DIM), BF16),
        grid=(HEADS_PER_GROUP // hb, b),
        in_specs=[qkv_spec] * N_GROUPS + bias_specs,
        out_specs=pl.BlockSpec((1, hb, s, HEAD_DIM), lambda h, bi: (bi, h, 0, 0)),
        scratch_shapes=[pltpu.VMEM((N_GROUPS, s, HEAD_DIM), F32)] * 3,
        compiler_params=_params(2),
        name="dilated_attention",
    )(*qkvs, *biases)


def _residual_ffn(x, y, ada_ref, g2_ref, w_in_ref, w_out_ref):
    x1 = x + ada_ref[0, 2:3, :] * y
    h = _mod_rmsnorm(x1, g2_ref[...], ada_ref[0, 3:4, :], ada_ref[0, 4:5, :]).astype(BF16)
    acc = None
    for c0, cw in FF_CHUNKS:
        gate = jnp.dot(h, w_in_ref[:, c0:c0 + cw], preferred_element_type=F32)
        up = jnp.dot(h, w_in_ref[:, D_FF + c0:D_FF + c0 + cw], preferred_element_type=F32)
        a = (_silu(gate) * up).astype(BF16)
        part = jnp.dot(a, w_out_ref[c0:c0 + cw, :], preferred_element_type=F32)
        acc = part if acc is None else acc + part
    return x1 + ada_ref[0, 5:6, :] * acc


def _glu_head(x, ada_ref, g_ref, w_ref, b_ref):
    d = x.shape[-1]
    h = _mod_rmsnorm(x, g_ref[...], ada_ref[0, 0:1, :], ada_ref[0, 1:2, :]).astype(BF16)
    a = jnp.dot(h, w_ref[:, :d], preferred_element_type=F32) + b_ref[:, :d]
    gt = jnp.dot(h, w_ref[:, d:], preferred_element_type=F32) + b_ref[:, d:]
    return a * jax.nn.sigmoid(gt)


def _attn_tail_kernel(o_ref, x_ref, ada_ref, wo_ref, g2_ref, w_in_ref, w_out_ref, out_ref):
    o = jnp.concatenate([o_ref[0, h] for h in range(HEADS_PER_GROUP)], axis=-1)
    y = jnp.dot(o, wo_ref[...], preferred_element_type=F32)
    out_ref[0] = _residual_ffn(x_ref[0], y, ada_ref, g2_ref, w_in_ref, w_out_ref)


def _attn_tail(o, x, ada, w_o, gain2, w_in, w_out):
    b, s, d = x.shape
    tm = min(TOKEN_TILE, s)
    return pl.pallas_call(
        _attn_tail_kernel,
        out_shape=jax.ShapeDtypeStruct((b, s, d), F32),
        grid=(b, s // tm),
        in_specs=[pl.BlockSpec((1, HEADS_PER_GROUP, tm, HEAD_DIM), lambda bi, i: (bi, 0, i, 0)),
                  pl.BlockSpec((1, tm, d), lambda bi, i: (bi, i, 0)),
                  pl.BlockSpec((1, 6, d), lambda bi, i: (bi, 0, 0)),
                  _resident(w_o.shape), _resident((1, d)),
                  _resident(w_in.shape), _resident(w_out.shape)],
        out_specs=pl.BlockSpec((1, tm, d), lambda bi, i: (bi, i, 0)),
        compiler_params=_params(2),
        name="attn_out_ffn",
    )(o, x, ada, w_o, gain2, w_in, w_out)


def _conv_head_kernel(x_ref, ada_ref, g_ref, w_ref, b_ref, u_ref):
    u_ref[0] = _glu_head(x_ref[0], ada_ref, g_ref, w_ref, b_ref)


def _conv_head(x, ada, gain, w_pw1, b_pw1):
    b, s, d = x.shape
    tm = min(PROJ_TOKEN_TILE, s)
    return pl.pallas_call(
        _conv_head_kernel,
        out_shape=jax.ShapeDtypeStruct((b, s, d), F32),
        grid=(b, s // tm),
        in_specs=[pl.BlockSpec((1, tm, d), lambda bi, i: (bi, i, 0)),
                  pl.BlockSpec((1, 6, d), lambda bi, i: (bi, 0, 0)),
                  _resident((1, d)), _resident(w_pw1.shape), _resident((1, 2 * d))],
        out_specs=pl.BlockSpec((1, tm, d), lambda bi, i: (bi, i, 0)),
        compiler_params=_params(2),
        name="conv_pw1_glu",
    )(x, ada, gain, w_pw1, b_pw1)


def _conv_tail_kernel(prev_ref, u_ref, next_ref, x_ref, ada_ref, wdw_ref, bdw_ref, lng_ref,
                      lnb_ref, w2_ref, b2_ref, g2_ref, w_in_ref, w_out_ref, out_ref,
                      win_ref, cv_ref):
    tm, d = u_ref.shape[1], u_ref.shape[2]
    i = pl.program_id(1)
    last = pl.num_programs(1) - 1
    prev = jnp.where(i > 0, prev_ref[0], 0.0)
    nxt = jnp.where(i < last, next_ref[0], 0.0)
    for c in range(d // LANES):
        lanes = slice(c * LANES, (c + 1) * LANES)
        win_ref[c, 0:HALO_ROWS, :] = prev[:, lanes]
        win_ref[c, HALO_ROWS:HALO_ROWS + tm, :] = u_ref[0, :, lanes]
        win_ref[c, HALO_ROWS + tm:, :] = nxt[:, lanes]

    for c in range(d // LANES):
        lanes = slice(c * LANES, (c + 1) * LANES)

        def conv_rows(j, carry, c=c, lanes=lanes):
            r0 = j * (2 * CONV_ROW_CHUNK)
            for parity in range(2):
                acc = None
                for k in range(CONV_WIDTH):
                    start = r0 + (parity + HALO_ROWS - CONV_PAD + k)
                    tap = (win_ref[c, pl.ds(start, CONV_ROW_CHUNK, stride=2), :]
                           * wdw_ref[k:k + 1, lanes])
                    acc = tap if acc is None else acc + tap
                cv_ref[c, pl.ds(r0 + parity, CONV_ROW_CHUNK, stride=2), :] = acc
            return carry

        lax.fori_loop(0, tm // (2 * CONV_ROW_CHUNK), conv_rows, 0, unroll=CONV_LOOP_UNROLL)

    cv = jnp.concatenate([cv_ref[c] for c in range(d // LANES)], axis=1) + bdw_ref[...]
    mu = jnp.mean(cv, axis=-1, keepdims=True)
    xc = cv - mu
    var = jnp.mean(xc * xc, axis=-1, keepdims=True)
    ln = xc * lax.rsqrt(var + EPS) * lng_ref[...] + lnb_ref[...]
    y = jnp.dot(_silu(ln).astype(BF16), w2_ref[...], preferred_element_type=F32) + b2_ref[...]
    out_ref[0] = _residual_ffn(x_ref[0], y, ada_ref, g2_ref, w_in_ref, w_out_ref)


def _conv_tail(u, x, ada, w_dw, b_dw, ln_g, ln_b, w_pw2, b_pw2, gain2, w_in, w_out):
    b, s, d = x.shape
    tm = min(TOKEN_TILE, s)
    assert s % tm == 0 and tm % (2 * CONV_ROW_CHUNK * CONV_LOOP_UNROLL) == 0
    assert HALO_ROWS >= CONV_PAD + 1
    hb = tm // HALO_ROWS
    n_halo = s // HALO_ROWS
    return pl.pallas_call(
        _conv_tail_kernel,
        out_shape=jax.ShapeDtypeStruct((b, s, d), F32),
        grid=(b, s // tm),
        in_specs=[pl.BlockSpec((1, HALO_ROWS, d), lambda bi, i: (bi, jnp.maximum(i * hb - 1, 0), 0)),
                  pl.BlockSpec((1, tm, d), lambda bi, i: (bi, i, 0)),
                  pl.BlockSpec((1, HALO_ROWS, d),
                               lambda bi, i: (bi, jnp.minimum((i + 1) * hb, n_halo - 1), 0)),
                  pl.BlockSpec((1, tm, d), lambda bi, i: (bi, i, 0)),
                  pl.BlockSpec((1, 6, d), lambda bi, i: (bi, 0, 0)),
                  _resident(w_dw.shape), _resident((1, d)), _resident((1, d)), _resident((1, d)),
                  _resident(w_pw2.shape), _resident((1, d)), _resident((1, d)),
                  _resident(w_in.shape), _resident(w_out.shape)],
        out_specs=pl.BlockSpec((1, tm, d), lambda bi, i: (bi, i, 0)),
        scratch_shapes=[pltpu.VMEM((d // LANES, tm + 2 * HALO_ROWS, LANES), F32),
                        pltpu.VMEM((d // LANES, tm, LANES), F32)],
        compiler_params=_params(2),
        name="conv_dw_pw2_ffn",
    )(u, u, u, x, ada, w_dw, b_dw, ln_g, ln_b, w_pw2, b_pw2, gain2, w_in, w_out)


def _trunk(x, ada, biases, p):
    d = x.shape[-1]
    row = lambda v: v.reshape(1, -1)
    for i in range(DEPTH):
        j = i // N_MIXERS
        gain1, gain2 = row(p["norm1_g"][i]), row(p["norm2_g"][i])
        w_in, w_out = p["ffn_w_in"][i], p["ffn_w_out"][i]
        if i % N_MIXERS == 0:
            qkvs = [_qkv_proj(x, ada[i], gain1, p["attn_w_qkv"][j, g],
                              row(p["attn_q_gain"][j]), row(p["attn_k_gain"][j]), dil)
                    for g, (_, dil) in enumerate(DILATED_GROUPS)]
            o = _attention(qkvs, biases)
            x = _attn_tail(o, x, ada[i], p["attn_w_o"][j], gain2, w_in, w_out)
        else:
            u = _conv_head(x, ada[i], gain1, p["conv_w_pw1"][j], row(p["conv_b_pw1"][j]))
            x = _conv_tail(u, x, ada[i], p["conv_w_dw"][j], row(p["conv_b_dw"][j]),
                           row(p["conv_ln_g"][j]), row(p["conv_ln_b"][j]), p["conv_w_pw2"][j],
                           row(p["conv_b_pw2"][j]), gain2, w_in, w_out)
    return x


def _group_qkv_weights(w_qkv):
    n, d, _ = w_qkv.shape
    w = w_qkv.astype(BF16).reshape(n, d, 3, N_GROUPS, HEADS_PER_GROUP * HEAD_DIM)
    return w.transpose(0, 3, 1, 2, 4).reshape(n, N_GROUPS, d, 3 * HEADS_PER_GROUP * HEAD_DIM)


def kernel(x_prompt, x_sample, c_prompt, c_sample, rel_bias_table, norm1_g, norm2_g, ada_w, ada_b, attn_w_qkv, attn_q_gain, attn_k_gain, attn_w_o, conv_w_pw1, conv_b_pw1, conv_w_dw, conv_b_dw, conv_ln_g, conv_ln_b, conv_w_pw2, conv_b_pw2, ffn_w_in, ffn_w_out):
    p = dict(norm1_g=norm1_g, norm2_g=norm2_g, attn_q_gain=attn_q_gain, attn_k_gain=attn_k_gain,
             conv_b_pw1=conv_b_pw1, conv_w_dw=conv_w_dw, conv_b_dw=conv_b_dw, conv_ln_g=conv_ln_g,
             conv_ln_b=conv_ln_b, conv_b_pw2=conv_b_pw2,
             attn_w_qkv=_group_qkv_weights(attn_w_qkv), attn_w_o=attn_w_o.astype(BF16),
             conv_w_pw1=conv_w_pw1.astype(BF16), conv_w_pw2=conv_w_pw2.astype(BF16),
             ffn_w_in=ffn_w_in.astype(BF16), ffn_w_out=ffn_w_out.astype(BF16))
    nb_p = c_prompt.shape[0]
    d = x_prompt.shape[-1]
    c_all = jnp.concatenate([c_prompt, c_sample], axis=0)
    ada = _ada_all(c_all, ada_w, ada_b).reshape(DEPTH, c_all.shape[0], 6, d)

    bias_cache = {}

    def biases_for(seq_len):
        out = []
        for g, (_, dil) in enumerate(DILATED_GROUPS):
            key = (g,) + _attn_geometry(seq_len, dil)
            if key not in bias_cache:
                bias_cache[key] = _expand_bias(rel_bias_table, _bias_bucket_index(seq_len, dil), g)
            out.append(bias_cache[key])
        return out

    y_prompt = _trunk(x_prompt, ada[:, :nb_p], biases_for(x_prompt.shape[1]), p)
    y_sample = _trunk(x_sample, ada[:, nb_p:], biases_for(x_sample.shape[1]), p)
    return (y_prompt, y_sample)
```

```python
import functools
import math
from typing import NamedTuple

import jax
import jax.numpy as jnp
from jax import lax
from jax.experimental import pallas as pl
from jax.experimental.pallas import tpu as pltpu

F32 = jnp.float32
BF16 = jnp.bfloat16

DEPTH = 4
N_MIXERS = 2
DILATED_GROUPS = ((128, 1), (512, 4), (2048, 16))
N_GROUPS = len(DILATED_GROUPS)
HEADS_PER_GROUP = 8
HEAD_DIM = 128
REL_BUCKETS = 32
REL_MAX_DIST = 1024
CONV_WIDTH = 31
CONV_PAD = (CONV_WIDTH - 1) // 2
D_FF = 2816
NEG_INF = -1e30
EPS = 1e-6
LOG2_E = math.log2(math.e)

HALF = DILATED_GROUPS[0][0] // (2 * DILATED_GROUPS[0][1])
assert all(w // (2 * d) == HALF for w, d in DILATED_GROUPS)

TOKEN_TILE = 512
PROJ_TOKEN_TILE = 1024
QUERY_BLOCK = 128
ATTN_ROWS_PER_STEP = 4096
ATTN_BLOCKS_PER_STEP = 32
HALO_ROWS = 16
CONV_ROW_CHUNK = 64
CONV_LOOP_UNROLL = 2
BF16_SUBLANES = 16
LANES = 128
FF_CHUNKS = ((0, 768), (768, 768), (1536, 768), (2304, 512))
VMEM_LIMIT_BYTES = 60000 * 1024


def _params(n_axes):
    return pltpu.CompilerParams(dimension_semantics=("parallel",) * n_axes,
                                vmem_limit_bytes=VMEM_LIMIT_BYTES)


class _LayerOf(NamedTuple):
    stack: jax.Array
    index: tuple

    @property
    def shape(self):
        return self.stack.shape[len(self.index):]


def _resident(w):
    if isinstance(w, _LayerOf):
        lead, shape = tuple(w.index), w.shape
        block = (None,) * len(lead) + tuple(shape)
        return pl.BlockSpec(block, lambda *_: lead + (0,) * len(shape),
                            pipeline_mode=pl.Buffered(1)), w.stack
    nd = w.ndim
    return pl.BlockSpec(w.shape, lambda *_: (0,) * nd, pipeline_mode=pl.Buffered(1)), w


def _silu(x):
    return x * jax.nn.sigmoid(x)


def _mod_rmsnorm(x, gain, shift, scale):
    ms = jnp.mean(x * x, axis=-1, keepdims=True)
    y = x * lax.rsqrt(ms + EPS) * gain
    return y * (1.0 + scale) + shift


def _ada_kernel(c_ref, w_ref, b_ref, o_ref):
    ca = _silu(c_ref[...]).astype(BF16)
    o_ref[0] = jnp.dot(ca, w_ref[0].astype(BF16), preferred_element_type=F32) + b_ref[0]


def _ada_all(c, ada_w, ada_b):
    nb, d = c.shape
    depth, _, width = ada_w.shape
    tn = 1536
    assert width % tn == 0
    return pl.pallas_call(
        _ada_kernel,
        out_shape=jax.ShapeDtypeStruct((depth, nb, width), F32),
        grid=(depth, width // tn),
        in_specs=[pl.BlockSpec((nb, d), lambda i, j: (0, 0)),
                  pl.BlockSpec((1, d, tn), lambda i, j: (i, 0, j)),
                  pl.BlockSpec((1, 1, tn), lambda i, j: (i, 0, j))],
        out_specs=pl.BlockSpec((1, nb, tn), lambda i, j: (i, 0, j)),
        compiler_params=_params(2),
        name="ada_proj",
    )(c, ada_w, ada_b.reshape(depth, 1, width))


def _t5_bucket(rel):
    half = REL_BUCKETS // 2
    max_exact = half // 2
    ret = jnp.where(rel > 0, half, 0)
    n = jnp.abs(rel)
    nf = jnp.maximum(n, 1).astype(F32)
    large = max_exact + (jnp.log(nf / max_exact) / math.log(REL_MAX_DIST / max_exact)
                         * (half - max_exact)).astype(jnp.int32)
    large = jnp.minimum(large, half - 1)
    return ret + jnp.where(n < max_exact, n, large)


def _attn_geometry(seq_len, dil):
    sub_len = seq_len // dil
    qb = min(QUERY_BLOCK, sub_len)
    kw = min(qb + 2 * HALF, sub_len)
    nblk = sub_len // qb
    assert sub_len * dil == seq_len and nblk * qb == sub_len and qb % HALF == 0
    return sub_len, qb, kw, nblk


def _key_offset(n, qb, kw, sub_len):
    return min(max(n * qb - HALF, 0), sub_len - kw)


def _bias_bucket_index(seq_len, dil):
    sub_len, qb, kw, nblk = _attn_geometry(seq_len, dil)
    slots = []
    for n in (0, min(1, nblk - 1), nblk - 1):
        delta = n * qb - _key_offset(n, qb, kw, sub_len)
        rel = jnp.arange(kw)[None, :] - jnp.arange(qb)[:, None] - delta
        slots.append(jnp.where(jnp.abs(rel) <= HALF, _t5_bucket(rel * dil), -1))
    return jnp.stack(slots).astype(jnp.int32)


def _bias_kernel(tab_ref, idx_ref, o_ref, *, head0):
    h = pl.program_id(0)
    idx = idx_ref[...]
    acc = jnp.full(idx.shape, NEG_INF, F32)
    for b in range(REL_BUCKETS):
        acc = jnp.where(idx == b, tab_ref[b, head0 + h], acc)
    o_ref[0] = acc * LOG2_E


def _expand_bias(rel_table, idx, group):
    _, qb, kw = idx.shape
    return pl.pallas_call(
        functools.partial(_bias_kernel, head0=group * HEADS_PER_GROUP),
        out_shape=jax.ShapeDtypeStruct((HEADS_PER_GROUP, 3, qb, kw), F32),
        grid=(HEADS_PER_GROUP,),
        in_specs=[pl.BlockSpec(memory_space=pltpu.SMEM),
                  pl.BlockSpec((3, qb, kw), lambda h: (0, 0, 0))],
        out_specs=pl.BlockSpec((1, 3, qb, kw), lambda h: (h, 0, 0, 0)),
        compiler_params=_params(1),
        name="rel_bias_expand",
    )(rel_table, idx)


def _qkv_kernel(x_ref, ada_ref, g_ref, w_ref, qg_ref, kg_ref, o_ref, *scratch, dil, two_level):
    tm, d = x_ref.shape[1], x_ref.shape[2]
    sub = tm // dil
    h = _mod_rmsnorm(x_ref[0], g_ref[...], ada_ref[0, 0:1, :], ada_ref[0, 1:2, :])
    if dil == 1:
        h = h.astype(BF16)
    elif two_level:
        step = math.isqrt(dil)
        slab_ref, perm_ref, mid_ref = scratch
        part = tm // step
        for c in range(d // LANES):
            slab_ref[c] = h[:, c * LANES:(c + 1) * LANES]
        for c in range(d // LANES):
            for r0 in range(step):
                mid_ref[c, r0 * part:(r0 + 1) * part, :] = slab_ref[c, pl.ds(r0, part, stride=step), :]
        for c in range(d // LANES):
            for r in range(dil):
                r0, r1 = r % step, r // step
                perm_ref[r * sub:(r + 1) * sub, c * LANES:(c + 1) * LANES] = (
                    mid_ref[c, pl.ds(r0 * part + r1, sub, stride=step), :].astype(BF16))
        h = perm_ref[...]
    else:
        slab_ref, perm_ref = scratch
        for c in range(d // LANES):
            slab_ref[c] = h[:, c * LANES:(c + 1) * LANES]
        for c in range(d // LANES):
            for r in range(dil):
                perm_ref[r * sub:(r + 1) * sub, c * LANES:(c + 1) * LANES] = (
                    slab_ref[c, pl.ds(r, sub, stride=dil), :].astype(BF16))
        h = perm_ref[...]
    width = HEADS_PER_GROUP * HEAD_DIM
    for t in range(3):
        acc = jnp.dot(h, w_ref[:, t * width:(t + 1) * width], preferred_element_type=F32)
        for hd in range(HEADS_PER_GROUP):
            c = acc[:, hd * HEAD_DIM:(hd + 1) * HEAD_DIM]
            if t < 2:
                ms = jnp.mean(c * c, axis=-1, keepdims=True)
                c = c * lax.rsqrt(ms + EPS) * (qg_ref[...] if t == 0 else kg_ref[...])
            if t == 0:
                c = c * (HEAD_DIM ** -0.5 * LOG2_E)
            o_ref[0, t, hd] = c.astype(BF16).reshape(dil, sub, HEAD_DIM)


def _qkv_proj(x, ada, gain, w, q_gain, k_gain, dil):
    b, s, d = x.shape
    sub_len = s // dil
    tm = min(PROJ_TOKEN_TILE, s)
    assert s % tm == 0 and tm % (dil * BF16_SUBLANES) == 0
    scratch = [] if dil == 1 else [pltpu.VMEM((d // LANES, tm, LANES), F32),
                                   pltpu.VMEM((tm, d), BF16)]
    two_level = dil > 4 and math.isqrt(dil) ** 2 == dil
    if two_level:
        scratch.append(pltpu.VMEM((d // LANES, tm, LANES), F32))
    specs, params = zip(*[_resident(p) for p in (gain, w, q_gain, k_gain)])
    return pl.pallas_call(
        functools.partial(_qkv_kernel, dil=dil, two_level=two_level),
        out_shape=jax.ShapeDtypeStruct((b, 3, HEADS_PER_GROUP, dil, sub_len, HEAD_DIM), BF16),
        grid=(b, s // tm),
        in_specs=[pl.BlockSpec((1, tm, d), lambda bi, i: (bi, i, 0)),
                  pl.BlockSpec((1, 6, d), lambda bi, i: (bi, 0, 0)), *specs],
        out_specs=pl.BlockSpec((1, 3, HEADS_PER_GROUP, dil, tm // dil, HEAD_DIM),
                               lambda bi, i: (bi, 0, 0, 0, i, 0)),
        scratch_shapes=scratch,
        compiler_params=_params(2),
        name=f"qkv_proj_d{dil}",
    )(x, ada, *params)


def _attn_kernel(qkv0_ref, qkv1_ref, qkv2_ref, b0_ref, b1_ref, b2_ref, o_ref, og_ref, mg_ref,
                 lg_ref, *, seq_len):
    heads = o_ref.shape[1]
    if heads == 1:
        _attn_one_head(0, qkv0_ref, qkv1_ref, qkv2_ref, b0_ref, b1_ref, b2_ref, o_ref, og_ref,
                       mg_ref, lg_ref, seq_len)
    else:
        def head(hh, carry):
            _attn_one_head(hh, qkv0_ref, qkv1_ref, qkv2_ref, b0_ref, b1_ref, b2_ref, o_ref, og_ref,
                           mg_ref, lg_ref, seq_len)
            return carry
        lax.fori_loop(0, heads, head, 0)


def _attn_one_head(hh, qkv0_ref, qkv1_ref, qkv2_ref, b0_ref, b1_ref, b2_ref, o_ref, og_ref, mg_ref,
                   lg_ref, seq_len):
    for g, (qkv_ref, bias_ref) in enumerate(((qkv0_ref, b0_ref), (qkv1_ref, b1_ref),
                                              (qkv2_ref, b2_ref))):
        dil = DILATED_GROUPS[g][1]
        sub_len, qb, kw, nblk = _attn_geometry(seq_len, dil)

        def block(r, n, g=g, dil=dil, sub_len=sub_len, qb=qb, kw=kw, nblk=nblk,
                  qkv_ref=qkv_ref, bias_ref=bias_ref):
            q0 = pl.multiple_of(r * sub_len + n * qb, qb)
            koff = jnp.clip(n * qb - HALF, 0, sub_len - kw)
            k0 = pl.multiple_of(r * sub_len + koff, HALF)
            q = qkv_ref[0, 0, hh, pl.ds(q0, qb), :]
            k = qkv_ref[0, 1, hh, pl.ds(k0, kw), :]
            v = qkv_ref[0, 2, hh, pl.ds(k0, kw), :]
            slot = jnp.where(n == 0, 0, jnp.where(n == nblk - 1, 2, 1))
            s = lax.dot_general(q, k, (((1,), (1,)), ((), ())), preferred_element_type=F32)
            s = s + bias_ref[hh, slot]
            m = jnp.max(s, axis=-1, keepdims=True)
            p = jnp.exp2(s - m).astype(BF16)
            v_ones = jnp.concatenate([v, jnp.ones((kw, HEAD_DIM), BF16)], axis=1)
            o_l = jnp.dot(p, v_ones, preferred_element_type=F32)
            start = r + n * (qb * dil)
            rows = pl.ds(start, qb) if dil == 1 else pl.ds(start, qb, stride=dil)
            og_ref[g, rows, :] = o_l[:, :HEAD_DIM]
            mg_ref[g, rows, :] = jnp.broadcast_to(m, (qb, HEAD_DIM))
            lg_ref[g, rows, :] = o_l[:, HEAD_DIM:]

        total = dil * nblk
        par = math.gcd(total, ATTN_BLOCKS_PER_STEP)

        def step(it, carry, block=block, nblk=nblk, par=par):
            for u in range(par):
                idx = it * par + u
                block(lax.div(idx, nblk), lax.rem(idx, nblk))
            return carry

        lax.fori_loop(0, total // par, step, 0)

    rc = 256

    def merge(c, carry):
        rows = pl.ds(pl.multiple_of(c * rc, rc), rc)
        ms = [mg_ref[g, rows, :] for g in range(N_GROUPS)]
        mx = jnp.maximum(jnp.maximum(ms[0], ms[1]), ms[2])
        num = den = None
        for g in range(N_GROUPS):
            w = jnp.exp2(ms[g] - mx)
            num = w * og_ref[g, rows, :] if num is None else num + w * og_ref[g, rows, :]
            den = w * lg_ref[g, rows, :] if den is None else den + w * lg_ref[g, rows, :]
        o_ref[0, hh, rows, :] = (num / den).astype(BF16)
        return carry

    lax.fori_loop(0, seq_len // rc, merge, 0)


def _attention(qkvs, biases):
    b = qkvs[0].shape[0]
    s = qkvs[0].shape[3] * qkvs[0].shape[4]
    qkvs = [t.reshape(b, 3, HEADS_PER_GROUP, s, HEAD_DIM) for t in qkvs]
    hb = max(1, min(HEADS_PER_GROUP, ATTN_ROWS_PER_STEP // s))
    assert HEADS_PER_GROUP % hb == 0
    qkv_spec = pl.BlockSpec((1, 3, hb, s, HEAD_DIM), lambda h, bi: (bi, 0, h, 0, 0))
    bias_specs = [pl.BlockSpec((hb,) + t.shape[1:], lambda h, bi: (h, 0, 0, 0)) for t in biases]
    return pl.pallas_call(
        functools.partial(_attn_kernel, seq_len=s),
        out_shape=jax.ShapeDtypeStruct((b, HEADS_PER_GROUP, s, HEAD_DIM), BF16),
        grid=(HEADS_PER_GROUP // hb, b),
        in_specs=[qkv_spec] * N_GROUPS + bias_specs,
        out_specs=pl.BlockSpec((1, hb, s, HEAD_DIM), lambda h, bi: (bi, h, 0, 0)),
        scratch_shapes=[pltpu.VMEM((N_GROUPS, s, HEAD_DIM), F32)] * 3,
        compiler_params=_params(2),
        name="dilated_attention",
    )(*qkvs, *biases)


def _residual_ffn(x, y, ada_ref, g2_ref, w_in_ref, w_out_ref):
    x1 = x + ada_ref[0, 2:3, :] * y
    h = _mod_rmsnorm(x1, g2_ref[...], ada_ref[0, 3:4, :], ada_ref[0, 4:5, :]).astype(BF16)
    acc = None
    for c0, cw in FF_CHUNKS:
        gate = jnp.dot(h, w_in_ref[:, c0:c0 + cw], preferred_element_type=F32)
        up = jnp.dot(h, w_in_ref[:, D_FF + c0:D_FF + c0 + cw], preferred_element_type=F32)
        a = (_silu(gate) * up).astype(BF16)
        part = jnp.dot(a, w_out_ref[c0:c0 + cw, :], preferred_element_type=F32)
        acc = part if acc is None else acc + part
    return x1 + ada_ref[0, 5:6, :] * acc


def _attn_tail_kernel(o_ref, x_ref, ada_ref, wo_ref, g2_ref, w_in_ref, w_out_ref, out_ref):
    o = jnp.concatenate([o_ref[0, h] for h in range(HEADS_PER_GROUP)], axis=-1)
    y = jnp.dot(o, wo_ref[...], preferred_element_type=F32)
    out_ref[0] = _residual_ffn(x_ref[0], y, ada_ref, g2_ref, w_in_ref, w_out_ref)


def _attn_tail(o, x, ada, w_o, gain2, w_in, w_out):
    b, s, d = x.shape
    tm = min(TOKEN_TILE, s)
    specs, params = zip(*[_resident(p) for p in (w_o, gain2, w_in, w_out)])
    return pl.pallas_call(
        _attn_tail_kernel,
        out_shape=jax.ShapeDtypeStruct((b, s, d), F32),
        grid=(b, s // tm),
        in_specs=[pl.BlockSpec((1, HEADS_PER_GROUP, tm, HEAD_DIM), lambda bi, i: (bi, 0, i, 0)),
                  pl.BlockSpec((1, tm, d), lambda bi, i: (bi, i, 0)),
                  pl.BlockSpec((1, 6, d), lambda bi, i: (bi, 0, 0)), *specs],
        out_specs=pl.BlockSpec((1, tm, d), lambda bi, i: (bi, i, 0)),
        compiler_params=_params(2),
        name="attn_out_ffn",
    )(o, x, ada, *params)


def _conv_head_kernel(x_ref, ada_ref, g_ref, w_ref, b_ref, u_ref):
    d = x_ref.shape[-1]
    h = _mod_rmsnorm(x_ref[0], g_ref[...], ada_ref[0, 0:1, :], ada_ref[0, 1:2, :]).astype(BF16)
    a = jnp.dot(h, w_ref[:, :d], preferred_element_type=F32) + b_ref[:, :d]
    gt = jnp.dot(h, w_ref[:, d:], preferred_element_type=F32) + b_ref[:, d:]
    u_ref[0] = a * jax.nn.sigmoid(gt)


def _conv_head(x, ada, gain, w_pw1, b_pw1):
    b, s, d = x.shape
    tm = min(PROJ_TOKEN_TILE, s)
    specs, params = zip(*[_resident(p) for p in (gain, w_pw1, b_pw1)])
    return pl.pallas_call(
        _conv_head_kernel,
        out_shape=jax.ShapeDtypeStruct((b, s, d), F32),
        grid=(b, s // tm),
        in_specs=[pl.BlockSpec((1, tm, d), lambda bi, i: (bi, i, 0)),
                  pl.BlockSpec((1, 6, d), lambda bi, i: (bi, 0, 0)), *specs],
        out_specs=pl.BlockSpec((1, tm, d), lambda bi, i: (bi, i, 0)),
        compiler_params=_params(2),
        name="conv_pw1_glu",
    )(x, ada, *params)


def _conv_tail_kernel(prev_ref, u_ref, next_ref, x_ref, ada_ref, wdw_ref, bdw_ref, lng_ref,
                      lnb_ref, w2_ref, b2_ref, g2_ref, w_in_ref, w_out_ref, out_ref,
                      win_ref, cv_ref):
    tm, d = u_ref.shape[1], u_ref.shape[2]
    i = pl.program_id(1)
    last = pl.num_programs(1) - 1
    prev = jnp.where(i > 0, prev_ref[0], 0.0)
    nxt = jnp.where(i < last, next_ref[0], 0.0)
    for c in range(d // LANES):
        lanes = slice(c * LANES, (c + 1) * LANES)
        win_ref[c, 0:HALO_ROWS, :] = prev[:, lanes]
        win_ref[c, HALO_ROWS:HALO_ROWS + tm, :] = u_ref[0, :, lanes]
        win_ref[c, HALO_ROWS + tm:, :] = nxt[:, lanes]

    for c in range(d // LANES):
        lanes = slice(c * LANES, (c + 1) * LANES)

        def conv_rows(j, carry, c=c, lanes=lanes):
            r0 = j * (2 * CONV_ROW_CHUNK)
            for parity in range(2):
                acc = None
                for k in range(CONV_WIDTH):
                    start = r0 + (parity + HALO_ROWS - CONV_PAD + k)
                    tap = (win_ref[c, pl.ds(start, CONV_ROW_CHUNK, stride=2), :]
                           * wdw_ref[k:k + 1, lanes])
                    acc = tap if acc is None else acc + tap
                cv_ref[c, pl.ds(r0 + parity, CONV_ROW_CHUNK, stride=2), :] = acc
            return carry

        lax.fori_loop(0, tm // (2 * CONV_ROW_CHUNK), conv_rows, 0, unroll=CONV_LOOP_UNROLL)

    cv = jnp.concatenate([cv_ref[c] for c in range(d // LANES)], axis=1) + bdw_ref[...]
    mu = jnp.mean(cv, axis=-1, keepdims=True)
    xc = cv - mu
    var = jnp.mean(xc * xc, axis=-1, keepdims=True)
    ln = xc * lax.rsqrt(var + EPS) * lng_ref[...] + lnb_ref[...]
    y = jnp.dot(_silu(ln).astype(BF16), w2_ref[...], preferred_element_type=F32) + b2_ref[...]
    out_ref[0] = _residual_ffn(x_ref[0], y, ada_ref, g2_ref, w_in_ref, w_out_ref)


def _conv_tail(u, x, ada, w_dw, b_dw, ln_g, ln_b, w_pw2, b_pw2, gain2, w_in, w_out):
    b, s, d = x.shape
    tm = min(TOKEN_TILE, s)
    assert s % tm == 0 and tm % (2 * CONV_ROW_CHUNK * CONV_LOOP_UNROLL) == 0
    assert HALO_ROWS >= CONV_PAD + 1
    hb = tm // HALO_ROWS
    n_halo = s // HALO_ROWS
    specs, params = zip(*[_resident(p) for p in (w_dw, b_dw, ln_g, ln_b, w_pw2, b_pw2, gain2,
                                                 w_in, w_out)])
    return pl.pallas_call(
        _conv_tail_kernel,
        out_shape=jax.ShapeDtypeStruct((b, s, d), F32),
        grid=(b, s // tm),
        in_specs=[pl.BlockSpec((1, HALO_ROWS, d), lambda bi, i: (bi, jnp.maximum(i * hb - 1, 0), 0)),
                  pl.BlockSpec((1, tm, d), lambda bi, i: (bi, i, 0)),
                  pl.BlockSpec((1, HALO_ROWS, d),
                               lambda bi, i: (bi, jnp.minimum((i + 1) * hb, n_halo - 1), 0)),
                  pl.BlockSpec((1, tm, d), lambda bi, i: (bi, i, 0)),
                  pl.BlockSpec((1, 6, d), lambda bi, i: (bi, 0, 0)), *specs],
        out_specs=pl.BlockSpec((1, tm, d), lambda bi, i: (bi, i, 0)),
        scratch_shapes=[pltpu.VMEM((d // LANES, tm + 2 * HALO_ROWS, LANES), F32),
                        pltpu.VMEM((d // LANES, tm, LANES), F32)],
        compiler_params=_params(2),
        name="conv_dw_pw2_ffn",
    )(u, u, u, x, ada, *params)


def _trunk(x, ada, biases, p):
    row = lambda v: v.reshape(1, -1)
    for i in range(DEPTH):
        j = i // N_MIXERS
        gain1, gain2 = row(p["norm1_g"][i]), row(p["norm2_g"][i])
        w_in, w_out = _LayerOf(p["ffn_w_in"], (i,)), _LayerOf(p["ffn_w_out"], (i,))
        if i % N_MIXERS == 0:
            qkvs = [_qkv_proj(x, ada[i], gain1, _LayerOf(p["attn_w_qkv"], (j, g)),
                              row(p["attn_q_gain"][j]), row(p["attn_k_gain"][j]), dil)
                    for g, (_, dil) in enumerate(DILATED_GROUPS)]
            o = _attention(qkvs, biases)
            x = _attn_tail(o, x, ada[i], _LayerOf(p["attn_w_o"], (j,)), gain2, w_in, w_out)
        else:
            u = _conv_head(x, ada[i], gain1, _LayerOf(p["conv_w_pw1"], (j,)),
                           row(p["conv_b_pw1"][j]))
            x = _conv_tail(u, x, ada[i], p["conv_w_dw"][j], row(p["conv_b_dw"][j]),
                           row(p["conv_ln_g"][j]), row(p["conv_ln_b"][j]),
                           _LayerOf(p["conv_w_pw2"], (j,)), row(p["conv_b_pw2"][j]), gain2,
                           w_in, w_out)
    return x


def _group_qkv_weights(w_qkv):
    n, d, _ = w_qkv.shape
    w = w_qkv.astype(BF16).reshape(n, d, 3, N_GROUPS, HEADS_PER_GROUP * HEAD_DIM)
    return w.transpose(0, 3, 1, 2, 4).reshape(n, N_GROUPS, d, 3 * HEADS_PER_GROUP * HEAD_DIM)


def kernel(x_prompt, x_sample, c_prompt, c_sample, rel_bias_table, norm1_g, norm2_g, ada_w, ada_b, attn_w_qkv, attn_q_gain, attn_k_gain, attn_w_o, conv_w_pw1, conv_b_pw1, conv_w_dw, conv_b_dw, conv_ln_g, conv_ln_b, conv_w_pw2, conv_b_pw2, ffn_w_in, ffn_w_out):
    p = dict(norm1_g=norm1_g, norm2_g=norm2_g, attn_q_gain=attn_q_gain, attn_k_gain=attn_k_gain,
             conv_b_pw1=conv_b_pw1, conv_w_dw=conv_w_dw, conv_b_dw=conv_b_dw, conv_ln_g=conv_ln_g,
             conv_ln_b=conv_ln_b, conv_b_pw2=conv_b_pw2,
             attn_w_qkv=_group_qkv_weights(attn_w_qkv), attn_w_o=attn_w_o.astype(BF16),
             conv_w_pw1=conv_w_pw1.astype(BF16), conv_w_pw2=conv_w_pw2.astype(BF16),
             ffn_w_in=ffn_w_in.astype(BF16), ffn_w_out=ffn_w_out.astype(BF16))
    nb_p = c_prompt.shape[0]
    d = x_prompt.shape[-1]
    c_all = jnp.concatenate([c_prompt, c_sample], axis=0)
    ada = _ada_all(c_all, ada_w, ada_b).reshape(DEPTH, c_all.shape[0], 6, d)

    bias_cache = {}

    def biases_for(seq_len):
        out = []
        for g, (_, dil) in enumerate(DILATED_GROUPS):
            key = (g,) + _attn_geometry(seq_len, dil)
            if key not in bias_cache:
                bias_cache[key] = _expand_bias(rel_bias_table, _bias_bucket_index(seq_len, dil), g)
            out.append(bias_cache[key])
        return out

    y_prompt = _trunk(x_prompt, ada[:, :nb_p], biases_for(x_prompt.shape[1]), p)
    y_sample = _trunk(x_sample, ada[:, nb_p:], biases_for(x_sample.shape[1]), p)
    return (y_prompt, y_sample)
```

```python
import functools
import math
from typing import NamedTuple

import jax
import jax.numpy as jnp
from jax import lax
from jax.experimental import pallas as pl
from jax.experimental.pallas import tpu as pltpu

F32 = jnp.float32
BF16 = jnp.bfloat16

DEPTH = 4
N_MIXERS = 2
DILATED_GROUPS = ((128, 1), (512, 4), (2048, 16))
N_GROUPS = len(DILATED_GROUPS)
HEADS_PER_GROUP = 8
HEAD_DIM = 128
REL_BUCKETS = 32
REL_MAX_DIST = 1024
CONV_WIDTH = 31
CONV_PAD = (CONV_WIDTH - 1) // 2
D_FF = 2816
NEG_INF = -1e30
EPS = 1e-6
LOG2_E = math.log2(math.e)

HALF = DILATED_GROUPS[0][0] // (2 * DILATED_GROUPS[0][1])
assert all(w // (2 * d) == HALF for w, d in DILATED_GROUPS)

TOKEN_TILE = 512
PROJ_TOKEN_TILE = 1024
QUERY_BLOCK = 128
ATTN_ROWS_PER_STEP = 4096
ATTN_BLOCKS_PER_STEP = 32
HALO_ROWS = 16
CONV_ROW_CHUNK = 64
CONV_LOOP_UNROLL = 2
BF16_SUBLANES = 16
LANES = 128
FF_CHUNKS = ((0, 768), (768, 768), (1536, 768), (2304, 512))
VMEM_LIMIT_BYTES = 60000 * 1024


def _params(n_axes):
    return pltpu.CompilerParams(dimension_semantics=("parallel",) * n_axes,
                                vmem_limit_bytes=VMEM_LIMIT_BYTES)


class _LayerOf(NamedTuple):
    stack: jax.Array
    layer: int
    cols: int = 0
    col_block: int = 0

    @property
    def shape(self):
        return (self.stack.shape[1], self.cols or self.stack.shape[2])


def _resident(w):
    if isinstance(w, _LayerOf):
        index = (w.layer, 0, w.col_block)
        return pl.BlockSpec((None,) + w.shape, lambda *_: index,
                            pipeline_mode=pl.Buffered(1)), w.stack
    nd = w.ndim
    return pl.BlockSpec(w.shape, lambda *_: (0,) * nd, pipeline_mode=pl.Buffered(1)), w


def _silu(x):
    return x * jax.nn.sigmoid(x)


def _mod_rmsnorm(x, gain, shift, scale):
    ms = jnp.mean(x * x, axis=-1, keepdims=True)
    y = x * lax.rsqrt(ms + EPS) * gain
    return y * (1.0 + scale) + shift


def _ada_kernel(c_ref, w_ref, b_ref, o_ref):
    ca = _silu(c_ref[...]).astype(BF16)
    o_ref[0] = jnp.dot(ca, w_ref[0].astype(BF16), preferred_element_type=F32) + b_ref[0]


def _ada_all(c, ada_w, ada_b):
    nb, d = c.shape
    depth, _, width = ada_w.shape
    tn = 1536
    assert width % tn == 0
    return pl.pallas_call(
        _ada_kernel,
        out_shape=jax.ShapeDtypeStruct((depth, nb, width), F32),
        grid=(depth, width // tn),
        in_specs=[pl.BlockSpec((nb, d), lambda i, j: (0, 0)),
                  pl.BlockSpec((1, d, tn), lambda i, j: (i, 0, j)),
                  pl.BlockSpec((1, 1, tn), lambda i, j: (i, 0, j))],
        out_specs=pl.BlockSpec((1, nb, tn), lambda i, j: (i, 0, j)),
        compiler_params=_params(2),
        name="ada_proj",
    )(c, ada_w, ada_b.reshape(depth, 1, width))


def _t5_bucket(rel):
    half = REL_BUCKETS // 2
    max_exact = half // 2
    ret = jnp.where(rel > 0, half, 0)
    n = jnp.abs(rel)
    nf = jnp.maximum(n, 1).astype(F32)
    large = max_exact + (jnp.log(nf / max_exact) / math.log(REL_MAX_DIST / max_exact)
                         * (half - max_exact)).astype(jnp.int32)
    large = jnp.minimum(large, half - 1)
    return ret + jnp.where(n < max_exact, n, large)


def _attn_geometry(seq_len, dil):
    sub_len = seq_len // dil
    qb = min(QUERY_BLOCK, sub_len)
    kw = min(qb + 2 * HALF, sub_len)
    nblk = sub_len // qb
    assert sub_len * dil == seq_len and nblk * qb == sub_len and qb % HALF == 0
    return sub_len, qb, kw, nblk


def _key_offset(n, qb, kw, sub_len):
    return min(max(n * qb - HALF, 0), sub_len - kw)


def _bias_bucket_index(seq_len, dil):
    sub_len, qb, kw, nblk = _attn_geometry(seq_len, dil)
    slots = []
    for n in (0, min(1, nblk - 1), nblk - 1):
        delta = n * qb - _key_offset(n, qb, kw, sub_len)
        rel = jnp.arange(kw)[None, :] - jnp.arange(qb)[:, None] - delta
        slots.append(jnp.where(jnp.abs(rel) <= HALF, _t5_bucket(rel * dil), -1))
    return jnp.stack(slots).astype(jnp.int32)


def _bias_kernel(tab_ref, idx_ref, o_ref, *, head0):
    h = pl.program_id(0)
    idx = idx_ref[...]
    acc = jnp.full(idx.shape, NEG_INF, F32)
    for b in range(REL_BUCKETS):
        acc = jnp.where(idx == b, tab_ref[b, head0 + h], acc)
    o_ref[0] = acc * LOG2_E


def _expand_bias(rel_table, idx, group):
    _, qb, kw = idx.shape
    return pl.pallas_call(
        functools.partial(_bias_kernel, head0=group * HEADS_PER_GROUP),
        out_shape=jax.ShapeDtypeStruct((HEADS_PER_GROUP, 3, qb, kw), F32),
        grid=(HEADS_PER_GROUP,),
        in_specs=[pl.BlockSpec(memory_space=pltpu.SMEM),
                  pl.BlockSpec((3, qb, kw), lambda h: (0, 0, 0))],
        out_specs=pl.BlockSpec((1, 3, qb, kw), lambda h: (h, 0, 0, 0)),
        compiler_params=_params(1),
        name="rel_bias_expand",
    )(rel_table, idx)


def _qkv_kernel(x_ref, ada_ref, g_ref, wq_ref, wk_ref, wv_ref, qg_ref, kg_ref, o_ref, *scratch,
                dil, two_level):
    tm, d = x_ref.shape[1], x_ref.shape[2]
    sub = tm // dil
    h = _mod_rmsnorm(x_ref[0], g_ref[...], ada_ref[0, 0:1, :], ada_ref[0, 1:2, :])
    if dil == 1:
        h = h.astype(BF16)
    elif two_level:
        step = math.isqrt(dil)
        slab_ref, perm_ref, mid_ref = scratch
        part = tm // step
        for c in range(d // LANES):
            slab_ref[c] = h[:, c * LANES:(c + 1) * LANES]
        for c in range(d // LANES):
            for r0 in range(step):
                mid_ref[c, r0 * part:(r0 + 1) * part, :] = slab_ref[c, pl.ds(r0, part, stride=step), :]
        for c in range(d // LANES):
            for r in range(dil):
                r0, r1 = r % step, r // step
                perm_ref[r * sub:(r + 1) * sub, c * LANES:(c + 1) * LANES] = (
                    mid_ref[c, pl.ds(r0 * part + r1, sub, stride=step), :].astype(BF16))
        h = perm_ref[...]
    else:
        slab_ref, perm_ref = scratch
        for c in range(d // LANES):
            slab_ref[c] = h[:, c * LANES:(c + 1) * LANES]
        for c in range(d // LANES):
            for r in range(dil):
                perm_ref[r * sub:(r + 1) * sub, c * LANES:(c + 1) * LANES] = (
                    slab_ref[c, pl.ds(r, sub, stride=dil), :].astype(BF16))
        h = perm_ref[...]
    for t, w_ref in enumerate((wq_ref, wk_ref, wv_ref)):
        acc = jnp.dot(h, w_ref[...], preferred_element_type=F32)
        for hd in range(HEADS_PER_GROUP):
            c = acc[:, hd * HEAD_DIM:(hd + 1) * HEAD_DIM]
            if t < 2:
                ms = jnp.mean(c * c, axis=-1, keepdims=True)
                c = c * lax.rsqrt(ms + EPS) * (qg_ref[...] if t == 0 else kg_ref[...])
            if t == 0:
                c = c * (HEAD_DIM ** -0.5 * LOG2_E)
            o_ref[0, t, hd] = c.astype(BF16).reshape(dil, sub, HEAD_DIM)


def _qkv_proj(x, ada, gain, w_qkv, layer, group, q_gain, k_gain, dil):
    b, s, d = x.shape
    sub_len = s // dil
    tm = min(PROJ_TOKEN_TILE, s)
    assert s % tm == 0 and tm % (dil * BF16_SUBLANES) == 0
    scratch = [] if dil == 1 else [pltpu.VMEM((d // LANES, tm, LANES), F32),
                                   pltpu.VMEM((tm, d), BF16)]
    two_level = dil > 4 and math.isqrt(dil) ** 2 == dil
    if two_level:
        scratch.append(pltpu.VMEM((d // LANES, tm, LANES), F32))
    width = HEADS_PER_GROUP * HEAD_DIM
    w_q, w_k, w_v = (_LayerOf(w_qkv, layer, width, t * N_GROUPS + group) for t in range(3))
    specs, params = zip(*[_resident(p) for p in (gain, w_q, w_k, w_v, q_gain, k_gain)])
    return pl.pallas_call(
        functools.partial(_qkv_kernel, dil=dil, two_level=two_level),
        out_shape=jax.ShapeDtypeStruct((b, 3, HEADS_PER_GROUP, dil, sub_len, HEAD_DIM), BF16),
        grid=(b, s // tm),
        in_specs=[pl.BlockSpec((1, tm, d), lambda bi, i: (bi, i, 0)),
                  pl.BlockSpec((1, 6, d), lambda bi, i: (bi, 0, 0)), *specs],
        out_specs=pl.BlockSpec((1, 3, HEADS_PER_GROUP, dil, tm // dil, HEAD_DIM),
                               lambda bi, i: (bi, 0, 0, 0, i, 0)),
        scratch_shapes=scratch,
        compiler_params=_params(2),
        name=f"qkv_proj_d{dil}",
    )(x, ada, *params)


def _attn_kernel(qkv0_ref, qkv1_ref, qkv2_ref, b0_ref, b1_ref, b2_ref, o_ref, og_ref, mg_ref,
                 lg_ref, *, seq_len):
    heads = o_ref.shape[1]
    if heads == 1:
        _attn_one_head(0, qkv0_ref, qkv1_ref, qkv2_ref, b0_ref, b1_ref, b2_ref, o_ref, og_ref,
                       mg_ref, lg_ref, seq_len)
    else:
        def head(hh, carry):
            _attn_one_head(hh, qkv0_ref, qkv1_ref, qkv2_ref, b0_ref, b1_ref, b2_ref, o_ref, og_ref,
                           mg_ref, lg_ref, seq_len)
            return carry
        lax.fori_loop(0, heads, head, 0)


def _attn_one_head(hh, qkv0_ref, qkv1_ref, qkv2_ref, b0_ref, b1_ref, b2_ref, o_ref, og_ref, mg_ref,
                   lg_ref, seq_len):
    for g, (qkv_ref, bias_ref) in enumerate(((qkv0_ref, b0_ref), (qkv1_ref, b1_ref),
                                              (qkv2_ref, b2_ref))):
        dil = DILATED_GROUPS[g][1]
        sub_len, qb, kw, nblk = _attn_geometry(seq_len, dil)

        def block(r, n, g=g, dil=dil, sub_len=sub_len, qb=qb, kw=kw, nblk=nblk,
                  qkv_ref=qkv_ref, bias_ref=bias_ref):
            q0 = pl.multiple_of(r * sub_len + n * qb, qb)
            koff = jnp.clip(n * qb - HALF, 0, sub_len - kw)
            k0 = pl.multiple_of(r * sub_len + koff, HALF)
            q = qkv_ref[0, 0, hh, pl.ds(q0, qb), :]
            k = qkv_ref[0, 1, hh, pl.ds(k0, kw), :]
            v = qkv_ref[0, 2, hh, pl.ds(k0, kw), :]
            slot = jnp.where(n == 0, 0, jnp.where(n == nblk - 1, 2, 1))
            s = lax.dot_general(q, k, (((1,), (1,)), ((), ())), preferred_element_type=F32)
            s = s + bias_ref[hh, slot]
            m = jnp.max(s, axis=-1, keepdims=True)
            p = jnp.exp2(s - m).astype(BF16)
            v_ones = jnp.concatenate([v, jnp.ones((kw, HEAD_DIM), BF16)], axis=1)
            o_l = jnp.dot(p, v_ones, preferred_element_type=F32)
            start = r + n * (qb * dil)
            rows = pl.ds(start, qb) if dil == 1 else pl.ds(start, qb, stride=dil)
            og_ref[g, rows, :] = o_l[:, :HEAD_DIM]
            mg_ref[g, rows, :] = jnp.broadcast_to(m, (qb, HEAD_DIM))
            lg_ref[g, rows, :] = o_l[:, HEAD_DIM:]

        total = dil * nblk
        par = math.gcd(total, ATTN_BLOCKS_PER_STEP)

        def step(it, carry, block=block, nblk=nblk, par=par):
            for u in range(par):
                idx = it * par + u
                block(lax.div(idx, nblk), lax.rem(idx, nblk))
            return carry

        lax.fori_loop(0, total // par, step, 0)

    rc = 256

    def merge(c, carry):
        rows = pl.ds(pl.multiple_of(c * rc, rc), rc)
        ms = [mg_ref[g, rows, :] for g in range(N_GROUPS)]
        mx = jnp.maximum(jnp.maximum(ms[0], ms[1]), ms[2])
        num = den = None
        for g in range(N_GROUPS):
            w = jnp.exp2(ms[g] - mx)
            num = w * og_ref[g, rows, :] if num is None else num + w * og_ref[g, rows, :]
            den = w * lg_ref[g, rows, :] if den is None else den + w * lg_ref[g, rows, :]
        o_ref[0, hh, rows, :] = (num / den).astype(BF16)
        return carry

    lax.fori_loop(0, seq_len // rc, merge, 0, unroll=4)


def _attention(qkvs, biases):
    b = qkvs[0].shape[0]
    s = qkvs[0].shape[3] * qkvs[0].shape[4]
    qkvs = [t.reshape(b, 3, HEADS_PER_GROUP, s, HEAD_DIM) for t in qkvs]
    hb = max(1, min(HEADS_PER_GROUP, ATTN_ROWS_PER_STEP // s))
    assert HEADS_PER_GROUP % hb == 0
    qkv_spec = pl.BlockSpec((1, 3, hb, s, HEAD_DIM), lambda h, bi: (bi, 0, h, 0, 0))
    bias_specs = [pl.BlockSpec((hb,) + t.shape[1:], lambda h, bi: (h, 0, 0, 0)) for t in biases]
    return pl.pallas_call(
        functools.partial(_attn_kernel, seq_len=s),
        out_shape=jax.ShapeDtypeStruct((b, HEADS_PER_GROUP, s, HEAD_DIM), BF16),
        grid=(HEADS_PER_GROUP // hb, b),
        in_specs=[qkv_spec] * N_GROUPS + bias_specs,
        out_specs=pl.BlockSpec((1, hb, s, HEAD_DIM), lambda h, bi: (bi, h, 0, 0)),
        scratch_shapes=[pltpu.VMEM((N_GROUPS, s, HEAD_DIM), F32)] * 3,
        compiler_params=_params(2),
        name="dilated_attention",
    )(*qkvs, *biases)


def _residual_ffn(x, y, ada_ref, g2_ref, w_in_ref, w_out_ref):
    x1 = x + ada_ref[0, 2:3, :] * y
    h = _mod_rmsnorm(x1, g2_ref[...], ada_ref[0, 3:4, :], ada_ref[0, 4:5, :]).astype(BF16)
    acc = None
    for c0, cw in FF_CHUNKS:
        gate = jnp.dot(h, w_in_ref[:, c0:c0 + cw], preferred_element_type=F32)
        up = jnp.dot(h, w_in_ref[:, D_FF + c0:D_FF + c0 + cw], preferred_element_type=F32)
        a = (_silu(gate) * up).astype(BF16)
        part = jnp.dot(a, w_out_ref[c0:c0 + cw, :], preferred_element_type=F32)
        acc = part if acc is None else acc + part
    return x1 + ada_ref[0, 5:6, :] * acc


def _attn_tail_kernel(o_ref, x_ref, ada_ref, wo_ref, g2_ref, w_in_ref, w_out_ref, out_ref):
    o = jnp.concatenate([o_ref[0, h] for h in range(HEADS_PER_GROUP)], axis=-1)
    y = jnp.dot(o, wo_ref[...], preferred_element_type=F32)
    out_ref[0] = _residual_ffn(x_ref[0], y, ada_ref, g2_ref, w_in_ref, w_out_ref)


def _attn_tail(o, x, ada, w_o, gain2, w_in, w_out):
    b, s, d = x.shape
    tm = min(TOKEN_TILE, s)
    specs, params = zip(*[_resident(p) for p in (w_o, gain2, w_in, w_out)])
    return pl.pallas_call(
        _attn_tail_kernel,
        out_shape=jax.ShapeDtypeStruct((b, s, d), F32),
        grid=(b, s // tm),
        in_specs=[pl.BlockSpec((1, HEADS_PER_GROUP, tm, HEAD_DIM), lambda bi, i: (bi, 0, i, 0)),
                  pl.BlockSpec((1, tm, d), lambda bi, i: (bi, i, 0)),
                  pl.BlockSpec((1, 6, d), lambda bi, i: (bi, 0, 0)), *specs],
        out_specs=pl.BlockSpec((1, tm, d), lambda bi, i: (bi, i, 0)),
        compiler_params=_params(2),
        name="attn_out_ffn",
    )(o, x, ada, *params)


def _conv_head_kernel(x_ref, ada_ref, g_ref, w_ref, b_ref, u_ref):
    d = x_ref.shape[-1]
    h = _mod_rmsnorm(x_ref[0], g_ref[...], ada_ref[0, 0:1, :], ada_ref[0, 1:2, :]).astype(BF16)
    a = jnp.dot(h, w_ref[:, :d], preferred_element_type=F32) + b_ref[:, :d]
    gt = jnp.dot(h, w_ref[:, d:], preferred_element_type=F32) + b_ref[:, d:]
    u_ref[0] = a * jax.nn.sigmoid(gt)


def _conv_head(x, ada, gain, w_pw1, b_pw1):
    b, s, d = x.shape
    tm = min(PROJ_TOKEN_TILE, s)
    specs, params = zip(*[_resident(p) for p in (gain, w_pw1, b_pw1)])
    return pl.pallas_call(
        _conv_head_kernel,
        out_shape=jax.ShapeDtypeStruct((b, s, d), F32),
        grid=(b, s // tm),
        in_specs=[pl.BlockSpec((1, tm, d), lambda bi, i: (bi, i, 0)),
                  pl.BlockSpec((1, 6, d), lambda bi, i: (bi, 0, 0)), *specs],
        out_specs=pl.BlockSpec((1, tm, d), lambda bi, i: (bi, i, 0)),
        compiler_params=_params(2),
        name="conv_pw1_glu",
    )(x, ada, *params)


def _conv_tail_kernel(prev_ref, u_ref, next_ref, x_ref, ada_ref, wdw_ref, bdw_ref, lng_ref,
                      lnb_ref, w2_ref, b2_ref, g2_ref, w_in_ref, w_out_ref, out_ref,
                      win_ref, cv_ref):
    tm, d = u_ref.shape[1], u_ref.shape[2]
    i = pl.program_id(1)
    last = pl.num_programs(1) - 1
    prev = jnp.where(i > 0, prev_ref[0], 0.0)
    nxt = jnp.where(i < last, next_ref[0], 0.0)
    for c in range(d // LANES):
        lanes = slice(c * LANES, (c + 1) * LANES)
        win_ref[c, 0:HALO_ROWS, :] = prev[:, lanes]
        win_ref[c, HALO_ROWS:HALO_ROWS + tm, :] = u_ref[0, :, lanes]
        win_ref[c, HALO_ROWS + tm:, :] = nxt[:, lanes]

    for c in range(d // LANES):
        lanes = slice(c * LANES, (c + 1) * LANES)

        def conv_rows(j, carry, c=c, lanes=lanes):
            r0 = j * (2 * CONV_ROW_CHUNK)
            for parity in range(2):
                acc = None
                for k in range(CONV_WIDTH):
                    start = r0 + (parity + HALO_ROWS - CONV_PAD + k)
                    tap = (win_ref[c, pl.ds(start, CONV_ROW_CHUNK, stride=2), :]
                           * wdw_ref[k:k + 1, lanes])
                    acc = tap if acc is None else acc + tap
                cv_ref[c, pl.ds(r0 + parity, CONV_ROW_CHUNK, stride=2), :] = acc
            return carry

        lax.fori_loop(0, tm // (2 * CONV_ROW_CHUNK), conv_rows, 0, unroll=CONV_LOOP_UNROLL)

    cv = jnp.concatenate([cv_ref[c] for c in range(d // LANES)], axis=1) + bdw_ref[...]
    mu = jnp.mean(cv, axis=-1, keepdims=True)
    xc = cv - mu
    var = jnp.mean(xc * xc, axis=-1, keepdims=True)
    ln = xc * lax.rsqrt(var + EPS) * lng_ref[...] + lnb_ref[...]
    y = jnp.dot(_silu(ln).astype(BF16), w2_ref[...], preferred_element_type=F32) + b2_ref[...]
    out_ref[0] = _residual_ffn(x_ref[0], y, ada_ref, g2_ref, w_in_ref, w_out_ref)


def _conv_tail(u, x, ada, w_dw, b_dw, ln_g, ln_b, w_pw2, b_pw2, gain2, w_in, w_out):
    b, s, d = x.shape
    tm = min(TOKEN_TILE, s)
    assert s % tm == 0 and tm % (2 * CONV_ROW_CHUNK * CONV_LOOP_UNROLL) == 0
    assert HALO_ROWS >= CONV_PAD + 1
    hb = tm // HALO_ROWS
    n_halo = s // HALO_ROWS
    specs, params = zip(*[_resident(p) for p in (w_dw, b_dw, ln_g, ln_b, w_pw2, b_pw2, gain2,
                                                 w_in, w_out)])
    return pl.pallas_call(
        _conv_tail_kernel,
        out_shape=jax.ShapeDtypeStruct((b, s, d), F32),
        grid=(b, s // tm),
        in_specs=[pl.BlockSpec((1, HALO_ROWS, d), lambda bi, i: (bi, jnp.maximum(i * hb - 1, 0), 0)),
                  pl.BlockSpec((1, tm, d), lambda bi, i: (bi, i, 0)),
                  pl.BlockSpec((1, HALO_ROWS, d),
                               lambda bi, i: (bi, jnp.minimum((i + 1) * hb, n_halo - 1), 0)),
                  pl.BlockSpec((1, tm, d), lambda bi, i: (bi, i, 0)),
                  pl.BlockSpec((1, 6, d), lambda bi, i: (bi, 0, 0)), *specs],
        out_specs=pl.BlockSpec((1, tm, d), lambda bi, i: (bi, i, 0)),
        scratch_shapes=[pltpu.VMEM((d // LANES, tm + 2 * HALO_ROWS, LANES), F32),
                        pltpu.VMEM((d // LANES, tm, LANES), F32)],
        compiler_params=_params(2),
        name="conv_dw_pw2_ffn",
    )(u, u, u, x, ada, *params)


def _trunk(x, ada, biases, p):
    row = lambda v: v.reshape(1, -1)
    for i in range(DEPTH):
        j = i // N_MIXERS
        gain1, gain2 = row(p["norm1_g"][i]), row(p["norm2_g"][i])
        w_in, w_out = _LayerOf(p["ffn_w_in"], i), _LayerOf(p["ffn_w_out"], i)
        if i % N_MIXERS == 0:
            qkvs = [_qkv_proj(x, ada[i], gain1, p["attn_w_qkv"], j, g,
                              row(p["attn_q_gain"][j]), row(p["attn_k_gain"][j]), dil)
                    for g, (_, dil) in enumerate(DILATED_GROUPS)]
            o = _attention(qkvs, biases)
            x = _attn_tail(o, x, ada[i], _LayerOf(p["attn_w_o"], j), gain2, w_in, w_out)
        else:
            u = _conv_head(x, ada[i], gain1, _LayerOf(p["conv_w_pw1"], j),
                           row(p["conv_b_pw1"][j]))
            x = _conv_tail(u, x, ada[i], p["conv_w_dw"][j], row(p["conv_b_dw"][j]),
                           row(p["conv_ln_g"][j]), row(p["conv_ln_b"][j]),
                           _LayerOf(p["conv_w_pw2"], j), row(p["conv_b_pw2"][j]), gain2,
                           w_in, w_out)
    return x


def kernel(x_prompt, x_sample, c_prompt, c_sample, rel_bias_table, norm1_g, norm2_g, ada_w, ada_b, attn_w_qkv, attn_q_gain, attn_k_gain, attn_w_o, conv_w_pw1, conv_b_pw1, conv_w_dw, conv_b_dw, conv_ln_g, conv_ln_b, conv_w_pw2, conv_b_pw2, ffn_w_in, ffn_w_out):
    p = dict(norm1_g=norm1_g, norm2_g=norm2_g, attn_q_gain=attn_q_gain, attn_k_gain=attn_k_gain,
             conv_b_pw1=conv_b_pw1, conv_w_dw=conv_w_dw, conv_b_dw=conv_b_dw, conv_ln_g=conv_ln_g,
             conv_ln_b=conv_ln_b, conv_b_pw2=conv_b_pw2,
             attn_w_qkv=attn_w_qkv.astype(BF16), attn_w_o=attn_w_o.astype(BF16),
             conv_w_pw1=conv_w_pw1.astype(BF16), conv_w_pw2=conv_w_pw2.astype(BF16),
             ffn_w_in=ffn_w_in.astype(BF16), ffn_w_out=ffn_w_out.astype(BF16))
    nb_p = c_prompt.shape[0]
    d = x_prompt.shape[-1]
    c_all = jnp.concatenate([c_prompt, c_sample], axis=0)
    ada = _ada_all(c_all, ada_w, ada_b).reshape(DEPTH, c_all.shape[0], 6, d)

    bias_cache = {}

    def biases_for(seq_len):
        out = []
        for g, (_, dil) in enumerate(DILATED_GROUPS):
            key = (g,) + _attn_geometry(seq_len, dil)
            if key not in bias_cache:
                bias_cache[key] = _expand_bias(rel_bias_table, _bias_bucket_index(seq_len, dil), g)
            out.append(bias_cache[key])
        return out

    y_prompt = _trunk(x_prompt, ada[:, :nb_p], biases_for(x_prompt.shape[1]), p)
    y_sample = _trunk(x_sample, ada[:, nb_p:], biases_for(x_sample.shape[1]), p)
    return (y_prompt, y_sample)
```

```python
import functools
import math
from typing import NamedTuple

import jax
import jax.numpy as jnp
from jax import lax
from jax.experimental import pallas as pl
from jax.experimental.pallas import tpu as pltpu

F32 = jnp.float32
BF16 = jnp.bfloat16

DEPTH = 4
N_MIXERS = 2
DILATED_GROUPS = ((128, 1), (512, 4), (2048, 16))
N_GROUPS = len(DILATED_GROUPS)
HEADS_PER_GROUP = 8
HEAD_DIM = 128
REL_BUCKETS = 32
REL_MAX_DIST = 1024
CONV_WIDTH = 31
CONV_PAD = (CONV_WIDTH - 1) // 2
D_FF = 2816
NEG_INF = -1e30
EPS = 1e-6
LOG2_E = math.log2(math.e)

HALF = DILATED_GROUPS[0][0] // (2 * DILATED_GROUPS[0][1])
assert all(w // (2 * d) == HALF for w, d in DILATED_GROUPS)

TOKEN_TILE = 512
PROJ_TOKEN_TILE = 1024
QUERY_BLOCK = 128
ATTN_ROWS_PER_STEP = 4096
ATTN_BLOCKS_PER_STEP = 32
QUARTERS = 4
HALO_ROWS = 16
CONV_ROW_CHUNK = 64
CONV_LOOP_UNROLL = 2
BF16_SUBLANES = 16
LANES = 128
FF_CHUNKS = ((0, 768), (768, 768), (1536, 768), (2304, 512))
VMEM_LIMIT_BYTES = 60000 * 1024


def _params(n_axes):
    return pltpu.CompilerParams(dimension_semantics=("parallel",) * n_axes,
                                vmem_limit_bytes=VMEM_LIMIT_BYTES)


class _LayerOf(NamedTuple):
    stack: jax.Array
    layer: int
    cols: int = 0
    col_block: int = 0

    @property
    def shape(self):
        return (self.stack.shape[1], self.cols or self.stack.shape[2])


def _resident(w):
    if isinstance(w, _LayerOf):
        index = (w.layer, 0, w.col_block)
        return pl.BlockSpec((None,) + w.shape, lambda *_: index,
                            pipeline_mode=pl.Buffered(1)), w.stack
    nd = w.ndim
    return pl.BlockSpec(w.shape, lambda *_: (0,) * nd, pipeline_mode=pl.Buffered(1)), w


def _silu(x):
    return x * jax.nn.sigmoid(x)


def _mod_rmsnorm(x, gain, shift, scale):
    ms = jnp.mean(x * x, axis=-1, keepdims=True)
    y = x * lax.rsqrt(ms + EPS) * gain
    return y * (1.0 + scale) + shift


def _ada_kernel(c_ref, w_ref, b_ref, o_ref):
    ca = _silu(c_ref[...]).astype(BF16)
    o_ref[0] = jnp.dot(ca, w_ref[0].astype(BF16), preferred_element_type=F32) + b_ref[0]


def _ada_all(c, ada_w, ada_b):
    nb, d = c.shape
    depth, _, width = ada_w.shape
    tn = 1536
    assert width % tn == 0
    return pl.pallas_call(
        _ada_kernel,
        out_shape=jax.ShapeDtypeStruct((depth, nb, width), F32),
        grid=(depth, width // tn),
        in_specs=[pl.BlockSpec((nb, d), lambda i, j: (0, 0)),
                  pl.BlockSpec((1, d, tn), lambda i, j: (i, 0, j)),
                  pl.BlockSpec((1, 1, tn), lambda i, j: (i, 0, j))],
        out_specs=pl.BlockSpec((1, nb, tn), lambda i, j: (i, 0, j)),
        compiler_params=_params(2),
        name="ada_proj",
    )(c, ada_w, ada_b.reshape(depth, 1, width))


def _t5_bucket(rel):
    half = REL_BUCKETS // 2
    max_exact = half // 2
    ret = jnp.where(rel > 0, half, 0)
    n = jnp.abs(rel)
    nf = jnp.maximum(n, 1).astype(F32)
    large = max_exact + (jnp.log(nf / max_exact) / math.log(REL_MAX_DIST / max_exact)
                         * (half - max_exact)).astype(jnp.int32)
    large = jnp.minimum(large, half - 1)
    return ret + jnp.where(n < max_exact, n, large)


def _attn_geometry(seq_len, dil):
    sub_len = seq_len // dil
    qb = min(QUERY_BLOCK, sub_len)
    kw = min(qb + 2 * HALF, sub_len)
    nblk = sub_len // qb
    assert sub_len * dil == seq_len and nblk * qb == sub_len and qb % HALF == 0
    return sub_len, qb, kw, nblk


def _key_offset(n, qb, kw, sub_len):
    return min(max(n * qb - HALF, 0), sub_len - kw)


def _bias_bucket_index(seq_len, dil):
    sub_len, qb, kw, nblk = _attn_geometry(seq_len, dil)
    slots = []
    for n in (0, min(1, nblk - 1), nblk - 1):
        delta = n * qb - _key_offset(n, qb, kw, sub_len)
        rel = jnp.arange(kw)[None, :] - jnp.arange(qb)[:, None] - delta
        slots.append(jnp.where(jnp.abs(rel) <= HALF, _t5_bucket(rel * dil), -1))
    return jnp.stack(slots).astype(jnp.int32)


def _bias_kernel(tab_ref, idx_ref, o_ref, *, head0):
    h = pl.program_id(0)
    idx = idx_ref[...]
    acc = jnp.full(idx.shape, NEG_INF, F32)
    for b in range(REL_BUCKETS):
        acc = jnp.where(idx == b, tab_ref[b, head0 + h], acc)
    o_ref[0] = acc * LOG2_E


def _expand_bias(rel_table, idx, group):
    _, qb, kw = idx.shape
    return pl.pallas_call(
        functools.partial(_bias_kernel, head0=group * HEADS_PER_GROUP),
        out_shape=jax.ShapeDtypeStruct((HEADS_PER_GROUP, 3, qb, kw), F32),
        grid=(HEADS_PER_GROUP,),
        in_specs=[pl.BlockSpec(memory_space=pltpu.SMEM),
                  pl.BlockSpec((3, qb, kw), lambda h: (0, 0, 0))],
        out_specs=pl.BlockSpec((1, 3, qb, kw), lambda h: (h, 0, 0, 0)),
        compiler_params=_params(1),
        name="rel_bias_expand",
    )(rel_table, idx)


def _qkv_kernel(x_ref, ada_ref, g_ref, wq_ref, wk_ref, wv_ref, qg_ref, kg_ref, o_ref, *scratch,
                dil, two_level):
    tm, d = x_ref.shape[1], x_ref.shape[2]
    sub = tm // dil
    h = _mod_rmsnorm(x_ref[0], g_ref[...], ada_ref[0, 0:1, :], ada_ref[0, 1:2, :])
    if dil == 1:
        h = h.astype(BF16)
    elif two_level:
        step = math.isqrt(dil)
        slab_ref, perm_ref, mid_ref = scratch
        part = tm // step
        for c in range(d // LANES):
            slab_ref[c] = h[:, c * LANES:(c + 1) * LANES]
        for c in range(d // LANES):
            for r0 in range(step):
                mid_ref[c, r0 * part:(r0 + 1) * part, :] = slab_ref[c, pl.ds(r0, part, stride=step), :]
        for c in range(d // LANES):
            for r in range(dil):
                r0, r1 = r % step, r // step
                perm_ref[r * sub:(r + 1) * sub, c * LANES:(c + 1) * LANES] = (
                    mid_ref[c, pl.ds(r0 * part + r1, sub, stride=step), :].astype(BF16))
        h = perm_ref[...]
    else:
        slab_ref, perm_ref = scratch
        for c in range(d // LANES):
            slab_ref[c] = h[:, c * LANES:(c + 1) * LANES]
        for c in range(d // LANES):
            for r in range(dil):
                perm_ref[r * sub:(r + 1) * sub, c * LANES:(c + 1) * LANES] = (
                    slab_ref[c, pl.ds(r, sub, stride=dil), :].astype(BF16))
        h = perm_ref[...]
    for t, w_ref in enumerate((wq_ref, wk_ref, wv_ref)):
        acc = jnp.dot(h, w_ref[...], preferred_element_type=F32)
        for hd in range(HEADS_PER_GROUP):
            c = acc[:, hd * HEAD_DIM:(hd + 1) * HEAD_DIM]
            if t < 2:
                ms = jnp.mean(c * c, axis=-1, keepdims=True)
                c = c * lax.rsqrt(ms + EPS) * (qg_ref[...] if t == 0 else kg_ref[...])
            if t == 0:
                c = c * (HEAD_DIM ** -0.5 * LOG2_E)
            o_ref[0, t, hd] = c.astype(BF16).reshape(dil, sub, HEAD_DIM)


def _qkv_proj(x, ada, gain, w_qkv, layer, group, q_gain, k_gain, dil):
    b, s, d = x.shape
    sub_len = s // dil
    tm = min(PROJ_TOKEN_TILE, s)
    assert s % tm == 0 and tm % (dil * BF16_SUBLANES) == 0
    scratch = [] if dil == 1 else [pltpu.VMEM((d // LANES, tm, LANES), F32),
                                   pltpu.VMEM((tm, d), BF16)]
    two_level = dil > 4 and math.isqrt(dil) ** 2 == dil
    if two_level:
        scratch.append(pltpu.VMEM((d // LANES, tm, LANES), F32))
    width = HEADS_PER_GROUP * HEAD_DIM
    w_q, w_k, w_v = (_LayerOf(w_qkv, layer, width, t * N_GROUPS + group) for t in range(3))
    specs, params = zip(*[_resident(p) for p in (gain, w_q, w_k, w_v, q_gain, k_gain)])
    return pl.pallas_call(
        functools.partial(_qkv_kernel, dil=dil, two_level=two_level),
        out_shape=jax.ShapeDtypeStruct((b, 3, HEADS_PER_GROUP, dil, sub_len, HEAD_DIM), BF16),
        grid=(b, s // tm),
        in_specs=[pl.BlockSpec((1, tm, d), lambda bi, i: (bi, i, 0)),
                  pl.BlockSpec((1, 6, d), lambda bi, i: (bi, 0, 0)), *specs],
        out_specs=pl.BlockSpec((1, 3, HEADS_PER_GROUP, dil, tm // dil, HEAD_DIM),
                               lambda bi, i: (bi, 0, 0, 0, i, 0)),
        scratch_shapes=scratch,
        compiler_params=_params(2),
        name=f"qkv_proj_d{dil}",
    )(x, ada, *params)


def _attn_kernel(qkv0_ref, qkv1_ref, qkv2_ref, b0_ref, b1_ref, b2_ref, o_ref, og_ref, mg_ref,
                 lg_ref, fo_ref, *, seq_len):
    heads = o_ref.shape[1]
    if heads == 1:
        _attn_one_head(0, qkv0_ref, qkv1_ref, qkv2_ref, b0_ref, b1_ref, b2_ref, o_ref, og_ref,
                       mg_ref, lg_ref, fo_ref, seq_len)
    else:
        def head(hh, carry):
            _attn_one_head(hh, qkv0_ref, qkv1_ref, qkv2_ref, b0_ref, b1_ref, b2_ref, o_ref, og_ref,
                           mg_ref, lg_ref, fo_ref, seq_len)
            return carry
        lax.fori_loop(0, heads, head, 0)


def _quarter_major(dil):
    return dil > QUARTERS and dil % QUARTERS == 0


def _attn_one_head(hh, qkv0_ref, qkv1_ref, qkv2_ref, b0_ref, b1_ref, b2_ref, o_ref, og_ref, mg_ref,
                   lg_ref, fo_ref, seq_len):
    for g, (qkv_ref, bias_ref) in enumerate(((qkv0_ref, b0_ref), (qkv1_ref, b1_ref),
                                              (qkv2_ref, b2_ref))):
        dil = DILATED_GROUPS[g][1]
        sub_len, qb, kw, nblk = _attn_geometry(seq_len, dil)

        def block(r, n, g=g, dil=dil, sub_len=sub_len, qb=qb, kw=kw, nblk=nblk,
                  qkv_ref=qkv_ref, bias_ref=bias_ref):
            q0 = pl.multiple_of(r * sub_len + n * qb, qb)
            koff = jnp.clip(n * qb - HALF, 0, sub_len - kw)
            k0 = pl.multiple_of(r * sub_len + koff, HALF)
            q = qkv_ref[0, 0, hh, pl.ds(q0, qb), :]
            k = qkv_ref[0, 1, hh, pl.ds(k0, kw), :]
            v = qkv_ref[0, 2, hh, pl.ds(k0, kw), :]
            slot = jnp.where(n == 0, 0, jnp.where(n == nblk - 1, 2, 1))
            s = lax.dot_general(q, k, (((1,), (1,)), ((), ())), preferred_element_type=F32)
            s = s + bias_ref[hh, slot]
            m = jnp.max(s, axis=-1, keepdims=True)
            p = jnp.exp2(s - m).astype(BF16)
            v_ones = jnp.concatenate([v, jnp.ones((kw, HEAD_DIM), BF16)], axis=1)
            o_l = jnp.dot(p, v_ones, preferred_element_type=F32)
            if _quarter_major(dil):
                step4 = dil // QUARTERS
                start = (lax.rem(r, QUARTERS) * (seq_len // QUARTERS) + lax.div(r, QUARTERS)
                         + n * (qb * step4))
                rows = pl.ds(start, qb, stride=step4)
            else:
                start = r + n * (qb * dil)
                rows = pl.ds(start, qb) if dil == 1 else pl.ds(start, qb, stride=dil)
            og_ref[g, rows, :] = o_l[:, :HEAD_DIM]
            mg_ref[g, rows, :] = jnp.broadcast_to(m, (qb, HEAD_DIM))
            lg_ref[g, rows, :] = o_l[:, HEAD_DIM:]

        total = dil * nblk
        par = math.gcd(total, ATTN_BLOCKS_PER_STEP)

        def step(it, carry, block=block, nblk=nblk, par=par):
            for u in range(par):
                idx = it * par + u
                block(lax.div(idx, nblk), lax.rem(idx, nblk))
            return carry

        lax.fori_loop(0, total // par, step, 0)

    rc = 256

    def merge(c, carry):
        sub_rows = rc // QUARTERS
        for q in range(QUARTERS):
            token_rows = pl.ds(c * rc + q, sub_rows, stride=QUARTERS)
            quarter_rows = pl.ds(pl.multiple_of(c * sub_rows, sub_rows) + q * (seq_len // QUARTERS),
                                 sub_rows)
            rows = [quarter_rows if _quarter_major(dil) else token_rows
                    for _, dil in DILATED_GROUPS]
            ms = [mg_ref[g, rows[g], :] for g in range(N_GROUPS)]
            mx = jnp.maximum(jnp.maximum(ms[0], ms[1]), ms[2])
            num = den = None
            for g in range(N_GROUPS):
                w = jnp.exp2(ms[g] - mx)
                num = w * og_ref[g, rows[g], :] if num is None else num + w * og_ref[g, rows[g], :]
                den = w * lg_ref[g, rows[g], :] if den is None else den + w * lg_ref[g, rows[g], :]
            fo_ref[token_rows, :] = num / den
        rows = pl.ds(pl.multiple_of(c * rc, rc), rc)
        o_ref[0, hh, rows, :] = fo_ref[rows, :].astype(BF16)
        return carry

    lax.fori_loop(0, seq_len // rc, merge, 0, unroll=4)


def _attention(qkvs, biases):
    b = qkvs[0].shape[0]
    s = qkvs[0].shape[3] * qkvs[0].shape[4]
    qkvs = [t.reshape(b, 3, HEADS_PER_GROUP, s, HEAD_DIM) for t in qkvs]
    hb = max(1, min(HEADS_PER_GROUP, ATTN_ROWS_PER_STEP // s))
    assert HEADS_PER_GROUP % hb == 0
    qkv_spec = pl.BlockSpec((1, 3, hb, s, HEAD_DIM), lambda h, bi: (bi, 0, h, 0, 0))
    bias_specs = [pl.BlockSpec((hb,) + t.shape[1:], lambda h, bi: (h, 0, 0, 0)) for t in biases]
    return pl.pallas_call(
        functools.partial(_attn_kernel, seq_len=s),
        out_shape=jax.ShapeDtypeStruct((b, HEADS_PER_GROUP, s, HEAD_DIM), BF16),
        grid=(HEADS_PER_GROUP // hb, b),
        in_specs=[qkv_spec] * N_GROUPS + bias_specs,
        out_specs=pl.BlockSpec((1, hb, s, HEAD_DIM), lambda h, bi: (bi, h, 0, 0)),
        scratch_shapes=[pltpu.VMEM((N_GROUPS, s, HEAD_DIM), F32)] * 3
                       + [pltpu.VMEM((s, HEAD_DIM), F32)],
        compiler_params=_params(2),
        name="dilated_attention",
    )(*qkvs, *biases)


def _residual_ffn(x, y, ada_ref, g2_ref, w_in_ref, w_out_ref):
    x1 = x + ada_ref[0, 2:3, :] * y
    h = _mod_rmsnorm(x1, g2_ref[...], ada_ref[0, 3:4, :], ada_ref[0, 4:5, :]).astype(BF16)
    acc = None
    for c0, cw in FF_CHUNKS:
        gate = jnp.dot(h, w_in_ref[:, c0:c0 + cw], preferred_element_type=F32)
        up = jnp.dot(h, w_in_ref[:, D_FF + c0:D_FF + c0 + cw], preferred_element_type=F32)
        a = (_silu(gate) * up).astype(BF16)
        part = jnp.dot(a, w_out_ref[c0:c0 + cw, :], preferred_element_type=F32)
        acc = part if acc is None else acc + part
    return x1 + ada_ref[0, 5:6, :] * acc


def _attn_tail_kernel(o_ref, x_ref, ada_ref, wo_ref, g2_ref, w_in_ref, w_out_ref, out_ref):
    o = jnp.concatenate([o_ref[0, h] for h in range(HEADS_PER_GROUP)], axis=-1)
    y = jnp.dot(o, wo_ref[...], preferred_element_type=F32)
    out_ref[0] = _residual_ffn(x_ref[0], y, ada_ref, g2_ref, w_in_ref, w_out_ref)


def _attn_tail(o, x, ada, w_o, gain2, w_in, w_out):
    b, s, d = x.shape
    tm = min(TOKEN_TILE, s)
    specs, params = zip(*[_resident(p) for p in (w_o, gain2, w_in, w_out)])
    return pl.pallas_call(
        _attn_tail_kernel,
        out_shape=jax.ShapeDtypeStruct((b, s, d), F32),
        grid=(b, s // tm),
        in_specs=[pl.BlockSpec((1, HEADS_PER_GROUP, tm, HEAD_DIM), lambda bi, i: (bi, 0, i, 0)),
                  pl.BlockSpec((1, tm, d), lambda bi, i: (bi, i, 0)),
                  pl.BlockSpec((1, 6, d), lambda bi, i: (bi, 0, 0)), *specs],
        out_specs=pl.BlockSpec((1, tm, d), lambda bi, i: (bi, i, 0)),
        compiler_params=_params(2),
        name="attn_out_ffn",
    )(o, x, ada, *params)


def _conv_head_kernel(x_ref, ada_ref, g_ref, w_ref, b_ref, u_ref):
    d = x_ref.shape[-1]
    h = _mod_rmsnorm(x_ref[0], g_ref[...], ada_ref[0, 0:1, :], ada_ref[0, 1:2, :]).astype(BF16)
    a = jnp.dot(h, w_ref[:, :d], preferred_element_type=F32) + b_ref[:, :d]
    gt = jnp.dot(h, w_ref[:, d:], preferred_element_type=F32) + b_ref[:, d:]
    u_ref[0] = a * jax.nn.sigmoid(gt)


def _conv_head(x, ada, gain, w_pw1, b_pw1):
    b, s, d = x.shape
    tm = min(PROJ_TOKEN_TILE, s)
    specs, params = zip(*[_resident(p) for p in (gain, w_pw1, b_pw1)])
    return pl.pallas_call(
        _conv_head_kernel,
        out_shape=jax.ShapeDtypeStruct((b, s, d), F32),
        grid=(b, s // tm),
        in_specs=[pl.BlockSpec((1, tm, d), lambda bi, i: (bi, i, 0)),
                  pl.BlockSpec((1, 6, d), lambda bi, i: (bi, 0, 0)), *specs],
        out_specs=pl.BlockSpec((1, tm, d), lambda bi, i: (bi, i, 0)),
        compiler_params=_params(2),
        name="conv_pw1_glu",
    )(x, ada, *params)


def _conv_tail_kernel(prev_ref, u_ref, next_ref, x_ref, ada_ref, wdw_ref, bdw_ref, lng_ref,
                      lnb_ref, w2_ref, b2_ref, g2_ref, w_in_ref, w_out_ref, out_ref,
                      win_ref, cv_ref):
    tm, d = u_ref.shape[1], u_ref.shape[2]
    i = pl.program_id(1)
    last = pl.num_programs(1) - 1
    prev = jnp.where(i > 0, prev_ref[0], 0.0)
    nxt = jnp.where(i < last, next_ref[0], 0.0)
    for c in range(d // LANES):
        lanes = slice(c * LANES, (c + 1) * LANES)
        win_ref[c, 0:HALO_ROWS, :] = prev[:, lanes]
        win_ref[c, HALO_ROWS:HALO_ROWS + tm, :] = u_ref[0, :, lanes]
        win_ref[c, HALO_ROWS + tm:, :] = nxt[:, lanes]

    for c in range(d // LANES):
        lanes = slice(c * LANES, (c + 1) * LANES)

        def conv_rows(j, carry, c=c, lanes=lanes):
            r0 = j * (2 * CONV_ROW_CHUNK)
            for parity in range(2):
                acc = None
                for k in range(CONV_WIDTH):
                    start = r0 + (parity + HALO_ROWS - CONV_PAD + k)
                    tap = (win_ref[c, pl.ds(start, CONV_ROW_CHUNK, stride=2), :]
                           * wdw_ref[k:k + 1, lanes])
                    acc = tap if acc is None else acc + tap
                cv_ref[c, pl.ds(r0 + parity, CONV_ROW_CHUNK, stride=2), :] = acc
            return carry

        lax.fori_loop(0, tm // (2 * CONV_ROW_CHUNK), conv_rows, 0, unroll=CONV_LOOP_UNROLL)

    cv = jnp.concatenate([cv_ref[c] for c in range(d // LANES)], axis=1) + bdw_ref[...]
    mu = jnp.mean(cv, axis=-1, keepdims=True)
    xc = cv - mu
    var = jnp.mean(xc * xc, axis=-1, keepdims=True)
    ln = xc * lax.rsqrt(var + EPS) * lng_ref[...] + lnb_ref[...]
    y = jnp.dot(_silu(ln).astype(BF16), w2_ref[...], preferred_element_type=F32) + b2_ref[...]
    out_ref[0] = _residual_ffn(x_ref[0], y, ada_ref, g2_ref, w_in_ref, w_out_ref)


def _conv_tail(u, x, ada, w_dw, b_dw, ln_g, ln_b, w_pw2, b_pw2, gain2, w_in, w_out):
    b, s, d = x.shape
    tm = min(TOKEN_TILE, s)
    assert s % tm == 0 and tm % (2 * CONV_ROW_CHUNK * CONV_LOOP_UNROLL) == 0
    assert HALO_ROWS >= CONV_PAD + 1
    hb = tm // HALO_ROWS
    n_halo = s // HALO_ROWS
    specs, params = zip(*[_resident(p) for p in (w_dw, b_dw, ln_g, ln_b, w_pw2, b_pw2, gain2,
                                                 w_in, w_out)])
    return pl.pallas_call(
        _conv_tail_kernel,
        out_shape=jax.ShapeDtypeStruct((b, s, d), F32),
        grid=(b, s // tm),
        in_specs=[pl.BlockSpec((1, HALO_ROWS, d), lambda bi, i: (bi, jnp.maximum(i * hb - 1, 0), 0)),
                  pl.BlockSpec((1, tm, d), lambda bi, i: (bi, i, 0)),
                  pl.BlockSpec((1, HALO_ROWS, d),
                               lambda bi, i: (bi, jnp.minimum((i + 1) * hb, n_halo - 1), 0)),
                  pl.BlockSpec((1, tm, d), lambda bi, i: (bi, i, 0)),
                  pl.BlockSpec((1, 6, d), lambda bi, i: (bi, 0, 0)), *specs],
        out_specs=pl.BlockSpec((1, tm, d), lambda bi, i: (bi, i, 0)),
        scratch_shapes=[pltpu.VMEM((d // LANES, tm + 2 * HALO_ROWS, LANES), F32),
                        pltpu.VMEM((d // LANES, tm, LANES), F32)],
        compiler_params=_params(2),
        name="conv_dw_pw2_ffn",
    )(u, u, u, x, ada, *params)


def _trunk(x, ada, biases, p):
    row = lambda v: v.reshape(1, -1)
    for i in range(DEPTH):
        j = i // N_MIXERS
        gain1, gain2 = row(p["norm1_g"][i]), row(p["norm2_g"][i])
        w_in, w_out = _LayerOf(p["ffn_w_in"], i), _LayerOf(p["ffn_w_out"], i)
        if i % N_MIXERS == 0:
            qkvs = [_qkv_proj(x, ada[i], gain1, p["attn_w_qkv"], j, g,
                              row(p["attn_q_gain"][j]), row(p["attn_k_gain"][j]), dil)
                    for g, (_, dil) in enumerate(DILATED_GROUPS)]
            o = _attention(qkvs, biases)
            x = _attn_tail(o, x, ada[i], _LayerOf(p["attn_w_o"], j), gain2, w_in, w_out)
        else:
            u = _conv_head(x, ada[i], gain1, _LayerOf(p["conv_w_pw1"], j),
                           row(p["conv_b_pw1"][j]))
            x = _conv_tail(u, x, ada[i], p["conv_w_dw"][j], row(p["conv_b_dw"][j]),
                           row(p["conv_ln_g"][j]), row(p["conv_ln_b"][j]),
                           _LayerOf(p["conv_w_pw2"], j), row(p["conv_b_pw2"][j]), gain2,
                           w_in, w_out)
    return x


def kernel(x_prompt, x_sample, c_prompt, c_sample, rel_bias_table, norm1_g, norm2_g, ada_w, ada_b, attn_w_qkv, attn_q_gain, attn_k_gain, attn_w_o, conv_w_pw1, conv_b_pw1, conv_w_dw, conv_b_dw, conv_ln_g, conv_ln_b, conv_w_pw2, conv_b_pw2, ffn_w_in, ffn_w_out):
    p = dict(norm1_g=norm1_g, norm2_g=norm2_g, attn_q_gain=attn_q_gain, attn_k_gain=attn_k_gain,
             conv_b_pw1=conv_b_pw1, conv_w_dw=conv_w_dw, conv_b_dw=conv_b_dw, conv_ln_g=conv_ln_g,
             conv_ln_b=conv_ln_b, conv_b_pw2=conv_b_pw2,
             attn_w_qkv=attn_w_qkv.astype(BF16), attn_w_o=attn_w_o.astype(BF16),
             conv_w_pw1=conv_w_pw1.astype(BF16), conv_w_pw2=conv_w_pw2.astype(BF16),
             ffn_w_in=ffn_w_in.astype(BF16), ffn_w_out=ffn_w_out.astype(BF16))
    nb_p = c_prompt.shape[0]
    d = x_prompt.shape[-1]
    c_all = jnp.concatenate([c_prompt, c_sample], axis=0)
    ada = _ada_all(c_all, ada_w, ada_b).reshape(DEPTH, c_all.shape[0], 6, d)

    bias_cache = {}

    def biases_for(seq_len):
        out = []
        for g, (_, dil) in enumerate(DILATED_GROUPS):
            key = (g,) + _attn_geometry(seq_len, dil)
            if key not in bias_cache:
                bias_cache[key] = _expand_bias(rel_bias_table, _bias_bucket_index(seq_len, dil), g)
            out.append(bias_cache[key])
        return out

    y_prompt = _trunk(x_prompt, ada[:, :nb_p], biases_for(x_prompt.shape[1]), p)
    y_sample = _trunk(x_sample, ada[:, nb_p:], biases_for(x_sample.shape[1]), p)
    return (y_prompt, y_sample)
```

```python
import functools
import math
from typing import NamedTuple

import jax
import jax.numpy as jnp
from jax import lax
from jax.experimental import pallas as pl
from jax.experimental.pallas import tpu as pltpu

F32 = jnp.float32
BF16 = jnp.bfloat16

DEPTH = 4
N_MIXERS = 2
DILATED_GROUPS = ((128, 1), (512, 4), (2048, 16))
N_GROUPS = len(DILATED_GROUPS)
HEADS_PER_GROUP = 8
HEAD_DIM = 128
REL_BUCKETS = 32
REL_MAX_DIST = 1024
CONV_WIDTH = 31
CONV_PAD = (CONV_WIDTH - 1) // 2
D_FF = 2816
NEG_INF = -1e30
EPS = 1e-6
LOG2_E = math.log2(math.e)

HALF = DILATED_GROUPS[0][0] // (2 * DILATED_GROUPS[0][1])
assert all(w // (2 * d) == HALF for w, d in DILATED_GROUPS)

TOKEN_TILE = 512
PROJ_TOKEN_TILE = 1024
QUERY_BLOCK = 128
ATTN_ROWS_PER_STEP = 4096
ATTN_BLOCKS_PER_STEP = 32
QUARTERS = 4
HALO_ROWS = 16
CONV_ROW_CHUNK = 64
CONV_LOOP_UNROLL = 2
BF16_SUBLANES = 16
LANES = 128
FF_CHUNKS = ((0, 768), (768, 768), (1536, 768), (2304, 512))
VMEM_LIMIT_BYTES = 60000 * 1024


def _params(n_axes):
    return pltpu.CompilerParams(dimension_semantics=("parallel",) * n_axes,
                                vmem_limit_bytes=VMEM_LIMIT_BYTES)


class _LayerOf(NamedTuple):
    stack: jax.Array
    layer: int
    cols: int = 0
    col_block: int = 0

    @property
    def shape(self):
        return (self.stack.shape[1], self.cols or self.stack.shape[2])


def _resident(w):
    if isinstance(w, _LayerOf):
        index = (w.layer, 0, w.col_block)
        return pl.BlockSpec((None,) + w.shape, lambda *_: index,
                            pipeline_mode=pl.Buffered(1)), w.stack
    nd = w.ndim
    return pl.BlockSpec(w.shape, lambda *_: (0,) * nd, pipeline_mode=pl.Buffered(1)), w


def _silu(x):
    return x * jax.nn.sigmoid(x)


def _mod_rmsnorm(x, gain, shift, scale):
    ms = jnp.mean(x * x, axis=-1, keepdims=True)
    y = x * lax.rsqrt(ms + EPS) * gain
    return y * (1.0 + scale) + shift


def _ada_kernel(c_ref, w_ref, b_ref, o_ref):
    ca = _silu(c_ref[...]).astype(BF16)
    o_ref[0] = jnp.dot(ca, w_ref[0].astype(BF16), preferred_element_type=F32) + b_ref[0]


def _ada_all(c, ada_w, ada_b):
    nb, d = c.shape
    depth, _, width = ada_w.shape
    tn = 1536
    assert width % tn == 0
    return pl.pallas_call(
        _ada_kernel,
        out_shape=jax.ShapeDtypeStruct((depth, nb, width), F32),
        grid=(depth, width // tn),
        in_specs=[pl.BlockSpec((nb, d), lambda i, j: (0, 0)),
                  pl.BlockSpec((1, d, tn), lambda i, j: (i, 0, j)),
                  pl.BlockSpec((1, 1, tn), lambda i, j: (i, 0, j))],
        out_specs=pl.BlockSpec((1, nb, tn), lambda i, j: (i, 0, j)),
        compiler_params=_params(2),
        name="ada_proj",
    )(c, ada_w, ada_b.reshape(depth, 1, width))


def _t5_bucket(rel):
    half = REL_BUCKETS // 2
    max_exact = half // 2
    ret = jnp.where(rel > 0, half, 0)
    n = jnp.abs(rel)
    nf = jnp.maximum(n, 1).astype(F32)
    large = max_exact + (jnp.log(nf / max_exact) / math.log(REL_MAX_DIST / max_exact)
                         * (half - max_exact)).astype(jnp.int32)
    large = jnp.minimum(large, half - 1)
    return ret + jnp.where(n < max_exact, n, large)


def _attn_geometry(seq_len, dil):
    sub_len = seq_len // dil
    qb = min(QUERY_BLOCK, sub_len)
    kw = min(qb + 2 * HALF, sub_len)
    nblk = sub_len // qb
    assert sub_len * dil == seq_len and nblk * qb == sub_len and qb % HALF == 0
    return sub_len, qb, kw, nblk


def _key_offset(n, qb, kw, sub_len):
    return min(max(n * qb - HALF, 0), sub_len - kw)


def _bias_bucket_index(seq_len, dil):
    sub_len, qb, kw, nblk = _attn_geometry(seq_len, dil)
    slots = []
    for n in (0, min(1, nblk - 1), nblk - 1):
        delta = n * qb - _key_offset(n, qb, kw, sub_len)
        rel = jnp.arange(kw)[None, :] - jnp.arange(qb)[:, None] - delta
        slots.append(jnp.where(jnp.abs(rel) <= HALF, _t5_bucket(rel * dil), -1))
    return jnp.stack(slots).astype(jnp.int32)


def _bias_kernel(tab_ref, idx_ref, o_ref, *, head0):
    h = pl.program_id(0)
    idx = idx_ref[...]
    acc = jnp.full(idx.shape, NEG_INF, F32)
    for b in range(REL_BUCKETS):
        acc = jnp.where(idx == b, tab_ref[b, head0 + h], acc)
    o_ref[0] = acc * LOG2_E


def _expand_bias(rel_table, idx, group):
    _, qb, kw = idx.shape
    return pl.pallas_call(
        functools.partial(_bias_kernel, head0=group * HEADS_PER_GROUP),
        out_shape=jax.ShapeDtypeStruct((HEADS_PER_GROUP, 3, qb, kw), F32),
        grid=(HEADS_PER_GROUP,),
        in_specs=[pl.BlockSpec(memory_space=pltpu.SMEM),
                  pl.BlockSpec((3, qb, kw), lambda h: (0, 0, 0))],
        out_specs=pl.BlockSpec((1, 3, qb, kw), lambda h: (h, 0, 0, 0)),
        compiler_params=_params(1),
        name="rel_bias_expand",
    )(rel_table, idx)


def _qkv_kernel(x_ref, ada_ref, g_ref, wq_ref, wk_ref, wv_ref, qg_ref, kg_ref, o_ref, *scratch,
                dil, two_level):
    tm, d = x_ref.shape[1], x_ref.shape[2]
    sub = tm // dil
    h = _mod_rmsnorm(x_ref[0], g_ref[...], ada_ref[0, 0:1, :], ada_ref[0, 1:2, :])
    if dil == 1:
        h = h.astype(BF16)
    elif two_level:
        step = math.isqrt(dil)
        slab_ref, perm_ref, mid_ref = scratch
        part = tm // step
        for c in range(d // LANES):
            slab_ref[c] = h[:, c * LANES:(c + 1) * LANES]
        for c in range(d // LANES):
            for r0 in range(step):
                mid_ref[c, r0 * part:(r0 + 1) * part, :] = slab_ref[c, pl.ds(r0, part, stride=step), :]
        for c in range(d // LANES):
            for r in range(dil):
                r0, r1 = r % step, r // step
                perm_ref[r * sub:(r + 1) * sub, c * LANES:(c + 1) * LANES] = (
                    mid_ref[c, pl.ds(r0 * part + r1, sub, stride=step), :].astype(BF16))
        h = perm_ref[...]
    else:
        slab_ref, perm_ref = scratch
        for c in range(d // LANES):
            slab_ref[c] = h[:, c * LANES:(c + 1) * LANES]
        for c in range(d // LANES):
            for r in range(dil):
                perm_ref[r * sub:(r + 1) * sub, c * LANES:(c + 1) * LANES] = (
                    slab_ref[c, pl.ds(r, sub, stride=dil), :].astype(BF16))
        h = perm_ref[...]
    for t, w_ref in enumerate((wq_ref, wk_ref, wv_ref)):
        acc = jnp.dot(h, w_ref[...], preferred_element_type=F32)
        for hd in range(HEADS_PER_GROUP):
            c = acc[:, hd * HEAD_DIM:(hd + 1) * HEAD_DIM]
            if t < 2:
                ms = jnp.mean(c * c, axis=-1, keepdims=True)
                c = c * lax.rsqrt(ms + EPS) * (qg_ref[...] if t == 0 else kg_ref[...])
            if t == 0:
                c = c * (HEAD_DIM ** -0.5 * LOG2_E)
            o_ref[0, t, hd] = c.astype(BF16).reshape(dil, sub, HEAD_DIM)


def _qkv_proj(x, ada, gain, w_qkv, layer, group, q_gain, k_gain, dil):
    b, s, d = x.shape
    sub_len = s // dil
    tm = min(PROJ_TOKEN_TILE, s)
    assert s % tm == 0 and tm % (dil * BF16_SUBLANES) == 0
    scratch = [] if dil == 1 else [pltpu.VMEM((d // LANES, tm, LANES), F32),
                                   pltpu.VMEM((tm, d), BF16)]
    two_level = dil > 4 and math.isqrt(dil) ** 2 == dil
    if two_level:
        scratch.append(pltpu.VMEM((d // LANES, tm, LANES), F32))
    width = HEADS_PER_GROUP * HEAD_DIM
    w_q, w_k, w_v = (_LayerOf(w_qkv, layer, width, t * N_GROUPS + group) for t in range(3))
    specs, params = zip(*[_resident(p) for p in (gain, w_q, w_k, w_v, q_gain, k_gain)])
    return pl.pallas_call(
        functools.partial(_qkv_kernel, dil=dil, two_level=two_level),
        out_shape=jax.ShapeDtypeStruct((b, 3, HEADS_PER_GROUP, dil, sub_len, HEAD_DIM), BF16),
        grid=(b, s // tm),
        in_specs=[pl.BlockSpec((1, tm, d), lambda bi, i: (bi, i, 0)),
                  pl.BlockSpec((1, 6, d), lambda bi, i: (bi, 0, 0)), *specs],
        out_specs=pl.BlockSpec((1, 3, HEADS_PER_GROUP, dil, tm // dil, HEAD_DIM),
                               lambda bi, i: (bi, 0, 0, 0, i, 0)),
        scratch_shapes=scratch,
        compiler_params=_params(2),
        name=f"qkv_proj_d{dil}",
    )(x, ada, *params)


def _attn_kernel(qkv0_ref, qkv1_ref, qkv2_ref, b0_ref, b1_ref, b2_ref, o_ref, og_ref, mg_ref,
                 lg_ref, fo_ref, *, seq_len):
    heads = o_ref.shape[1]
    if heads == 1:
        _attn_one_head(0, qkv0_ref, qkv1_ref, qkv2_ref, b0_ref, b1_ref, b2_ref, o_ref, og_ref,
                       mg_ref, lg_ref, fo_ref, seq_len)
    else:
        def head(hh, carry):
            _attn_one_head(hh, qkv0_ref, qkv1_ref, qkv2_ref, b0_ref, b1_ref, b2_ref, o_ref, og_ref,
                           mg_ref, lg_ref, fo_ref, seq_len)
            return carry
        lax.fori_loop(0, heads, head, 0)


def _quarter_major(dil):
    return dil % QUARTERS == 0


def _attn_one_head(hh, qkv0_ref, qkv1_ref, qkv2_ref, b0_ref, b1_ref, b2_ref, o_ref, og_ref, mg_ref,
                   lg_ref, fo_ref, seq_len):
    for g, (qkv_ref, bias_ref) in enumerate(((qkv0_ref, b0_ref), (qkv1_ref, b1_ref),
                                              (qkv2_ref, b2_ref))):
        dil = DILATED_GROUPS[g][1]
        sub_len, qb, kw, nblk = _attn_geometry(seq_len, dil)

        def block(r, n, g=g, dil=dil, sub_len=sub_len, qb=qb, kw=kw, nblk=nblk,
                  qkv_ref=qkv_ref, bias_ref=bias_ref):
            q0 = pl.multiple_of(r * sub_len + n * qb, qb)
            koff = jnp.clip(n * qb - HALF, 0, sub_len - kw)
            k0 = pl.multiple_of(r * sub_len + koff, HALF)
            q = qkv_ref[0, 0, hh, pl.ds(q0, qb), :]
            k = qkv_ref[0, 1, hh, pl.ds(k0, kw), :]
            v = qkv_ref[0, 2, hh, pl.ds(k0, kw), :]
            slot = jnp.where(n == 0, 0, jnp.where(n == nblk - 1, 2, 1))
            s = lax.dot_general(q, k, (((1,), (1,)), ((), ())), preferred_element_type=F32)
            s = s + bias_ref[hh, slot]
            m = jnp.max(s, axis=-1, keepdims=True)
            p = jnp.exp2(s - m).astype(BF16)
            v_ones = jnp.concatenate([v, jnp.ones((kw, HEAD_DIM), BF16)], axis=1)
            o_l = jnp.dot(p, v_ones, preferred_element_type=F32)
            if _quarter_major(dil):
                step4 = dil // QUARTERS
                if step4 == 1:
                    rows = pl.ds(pl.multiple_of(r * (seq_len // QUARTERS) + n * qb, qb), qb)
                else:
                    start = (lax.rem(r, QUARTERS) * (seq_len // QUARTERS) + lax.div(r, QUARTERS)
                             + n * (qb * step4))
                    rows = pl.ds(start, qb, stride=step4)
            else:
                start = r + n * (qb * dil)
                rows = pl.ds(start, qb) if dil == 1 else pl.ds(start, qb, stride=dil)
            og_ref[g, rows, :] = o_l[:, :HEAD_DIM]
            mg_ref[g, rows, :] = jnp.broadcast_to(m, (qb, HEAD_DIM))
            lg_ref[g, rows, :] = o_l[:, HEAD_DIM:]

        total = dil * nblk
        par = math.gcd(total, ATTN_BLOCKS_PER_STEP)

        def step(it, carry, block=block, nblk=nblk, par=par):
            for u in range(par):
                idx = it * par + u
                block(lax.div(idx, nblk), lax.rem(idx, nblk))
            return carry

        lax.fori_loop(0, total // par, step, 0)

    rc = 256

    def merge(c, carry):
        sub_rows = rc // QUARTERS
        for q in range(QUARTERS):
            token_rows = pl.ds(c * rc + q, sub_rows, stride=QUARTERS)
            quarter_rows = pl.ds(pl.multiple_of(c * sub_rows, sub_rows) + q * (seq_len // QUARTERS),
                                 sub_rows)
            rows = [quarter_rows if _quarter_major(dil) else token_rows
                    for _, dil in DILATED_GROUPS]
            ms = [mg_ref[g, rows[g], :] for g in range(N_GROUPS)]
            mx = jnp.maximum(jnp.maximum(ms[0], ms[1]), ms[2])
            num = den = None
            for g in range(N_GROUPS):
                w = jnp.exp2(ms[g] - mx)
                num = w * og_ref[g, rows[g], :] if num is None else num + w * og_ref[g, rows[g], :]
                den = w * lg_ref[g, rows[g], :] if den is None else den + w * lg_ref[g, rows[g], :]
            fo_ref[token_rows, :] = num / den
        rows = pl.ds(pl.multiple_of(c * rc, rc), rc)
        o_ref[0, hh, rows, :] = fo_ref[rows, :].astype(BF16)
        return carry

    lax.fori_loop(0, seq_len // rc, merge, 0, unroll=4)


def _attention(qkvs, biases):
    b = qkvs[0].shape[0]
    s = qkvs[0].shape[3] * qkvs[0].shape[4]
    qkvs = [t.reshape(b, 3, HEADS_PER_GROUP, s, HEAD_DIM) for t in qkvs]
    hb = max(1, min(HEADS_PER_GROUP, ATTN_ROWS_PER_STEP // s))
    assert HEADS_PER_GROUP % hb == 0
    qkv_spec = pl.BlockSpec((1, 3, hb, s, HEAD_DIM), lambda h, bi: (bi, 0, h, 0, 0))
    bias_specs = [pl.BlockSpec((hb,) + t.shape[1:], lambda h, bi: (h, 0, 0, 0)) for t in biases]
    return pl.pallas_call(
        functools.partial(_attn_kernel, seq_len=s),
        out_shape=jax.ShapeDtypeStruct((b, HEADS_PER_GROUP, s, HEAD_DIM), BF16),
        grid=(HEADS_PER_GROUP // hb, b),
        in_specs=[qkv_spec] * N_GROUPS + bias_specs,
        out_specs=pl.BlockSpec((1, hb, s, HEAD_DIM), lambda h, bi: (bi, h, 0, 0)),
        scratch_shapes=[pltpu.VMEM((N_GROUPS, s, HEAD_DIM), F32)] * 3
                       + [pltpu.VMEM((s, HEAD_DIM), F32)],
        compiler_params=_params(2),
        name="dilated_attention",
    )(*qkvs, *biases)


def _residual_ffn(x, y, ada_ref, g2_ref, w_in_ref, w_out_ref):
    x1 = x + ada_ref[0, 2:3, :] * y
    h = _mod_rmsnorm(x1, g2_ref[...], ada_ref[0, 3:4, :], ada_ref[0, 4:5, :]).astype(BF16)
    acc = None
    for c0, cw in FF_CHUNKS:
        gate = jnp.dot(h, w_in_ref[:, c0:c0 + cw], preferred_element_type=F32)
        up = jnp.dot(h, w_in_ref[:, D_FF + c0:D_FF + c0 + cw], preferred_element_type=F32)
        a = (_silu(gate) * up).astype(BF16)
        part = jnp.dot(a, w_out_ref[c0:c0 + cw, :], preferred_element_type=F32)
        acc = part if acc is None else acc + part
    return x1 + ada_ref[0, 5:6, :] * acc


def _attn_tail_kernel(o_ref, x_ref, ada_ref, wo_ref, g2_ref, w_in_ref, w_out_ref, out_ref):
    o = jnp.concatenate([o_ref[0, h] for h in range(HEADS_PER_GROUP)], axis=-1)
    y = jnp.dot(o, wo_ref[...], preferred_element_type=F32)
    out_ref[0] = _residual_ffn(x_ref[0], y, ada_ref, g2_ref, w_in_ref, w_out_ref)


def _attn_tail(o, x, ada, w_o, gain2, w_in, w_out):
    b, s, d = x.shape
    tm = min(TOKEN_TILE, s)
    specs, params = zip(*[_resident(p) for p in (w_o, gain2, w_in, w_out)])
    return pl.pallas_call(
        _attn_tail_kernel,
        out_shape=jax.ShapeDtypeStruct((b, s, d), F32),
        grid=(b, s // tm),
        in_specs=[pl.BlockSpec((1, HEADS_PER_GROUP, tm, HEAD_DIM), lambda bi, i: (bi, 0, i, 0)),
                  pl.BlockSpec((1, tm, d), lambda bi, i: (bi, i, 0)),
                  pl.BlockSpec((1, 6, d), lambda bi, i: (bi, 0, 0)), *specs],
        out_specs=pl.BlockSpec((1, tm, d), lambda bi, i: (bi, i, 0)),
        compiler_params=_params(2),
        name="attn_out_ffn",
    )(o, x, ada, *params)


def _conv_head_kernel(x_ref, ada_ref, g_ref, w_ref, b_ref, u_ref):
    d = x_ref.shape[-1]
    h = _mod_rmsnorm(x_ref[0], g_ref[...], ada_ref[0, 0:1, :], ada_ref[0, 1:2, :]).astype(BF16)
    a = jnp.dot(h, w_ref[:, :d], preferred_element_type=F32) + b_ref[:, :d]
    gt = jnp.dot(h, w_ref[:, d:], preferred_element_type=F32) + b_ref[:, d:]
    u_ref[0] = a * jax.nn.sigmoid(gt)


def _conv_head(x, ada, gain, w_pw1, b_pw1):
    b, s, d = x.shape
    tm = min(PROJ_TOKEN_TILE, s)
    specs, params = zip(*[_resident(p) for p in (gain, w_pw1, b_pw1)])
    return pl.pallas_call(
        _conv_head_kernel,
        out_shape=jax.ShapeDtypeStruct((b, s, d), F32),
        grid=(b, s // tm),
        in_specs=[pl.BlockSpec((1, tm, d), lambda bi, i: (bi, i, 0)),
                  pl.BlockSpec((1, 6, d), lambda bi, i: (bi, 0, 0)), *specs],
        out_specs=pl.BlockSpec((1, tm, d), lambda bi, i: (bi, i, 0)),
        compiler_params=_params(2),
        name="conv_pw1_glu",
    )(x, ada, *params)


def _conv_tail_kernel(prev_ref, u_ref, next_ref, x_ref, ada_ref, wdw_ref, bdw_ref, lng_ref,
                      lnb_ref, w2_ref, b2_ref, g2_ref, w_in_ref, w_out_ref, out_ref,
                      win_ref, cv_ref):
    tm, d = u_ref.shape[1], u_ref.shape[2]
    i = pl.program_id(1)
    last = pl.num_programs(1) - 1
    prev = jnp.where(i > 0, prev_ref[0], 0.0)
    nxt = jnp.where(i < last, next_ref[0], 0.0)
    for c in range(d // LANES):
        lanes = slice(c * LANES, (c + 1) * LANES)
        win_ref[c, 0:HALO_ROWS, :] = prev[:, lanes]
        win_ref[c, HALO_ROWS:HALO_ROWS + tm, :] = u_ref[0, :, lanes]
        win_ref[c, HALO_ROWS + tm:, :] = nxt[:, lanes]

    for c in range(d // LANES):
        lanes = slice(c * LANES, (c + 1) * LANES)

        def conv_rows(j, carry, c=c, lanes=lanes):
            r0 = j * (2 * CONV_ROW_CHUNK)
            for parity in range(2):
                acc = None
                for k in range(CONV_WIDTH):
                    start = r0 + (parity + HALO_ROWS - CONV_PAD + k)
                    tap = (win_ref[c, pl.ds(start, CONV_ROW_CHUNK, stride=2), :]
                           * wdw_ref[k:k + 1, lanes])
                    acc = tap if acc is None else acc + tap
                cv_ref[c, pl.ds(r0 + parity, CONV_ROW_CHUNK, stride=2), :] = acc
            return carry

        lax.fori_loop(0, tm // (2 * CONV_ROW_CHUNK), conv_rows, 0, unroll=CONV_LOOP_UNROLL)

    cv = jnp.concatenate([cv_ref[c] for c in range(d // LANES)], axis=1) + bdw_ref[...]
    mu = jnp.mean(cv, axis=-1, keepdims=True)
    xc = cv - mu
    var = jnp.mean(xc * xc, axis=-1, keepdims=True)
    ln = xc * lax.rsqrt(var + EPS) * lng_ref[...] + lnb_ref[...]
    y = jnp.dot(_silu(ln).astype(BF16), w2_ref[...], preferred_element_type=F32) + b2_ref[...]
    out_ref[0] = _residual_ffn(x_ref[0], y, ada_ref, g2_ref, w_in_ref, w_out_ref)


def _conv_tail(u, x, ada, w_dw, b_dw, ln_g, ln_b, w_pw2, b_pw2, gain2, w_in, w_out):
    b, s, d = x.shape
    tm = min(TOKEN_TILE, s)
    assert s % tm == 0 and tm % (2 * CONV_ROW_CHUNK * CONV_LOOP_UNROLL) == 0
    assert HALO_ROWS >= CONV_PAD + 1
    hb = tm // HALO_ROWS
    n_halo = s // HALO_ROWS
    specs, params = zip(*[_resident(p) for p in (w_dw, b_dw, ln_g, ln_b, w_pw2, b_pw2, gain2,
                                                 w_in, w_out)])
    return pl.pallas_call(
        _conv_tail_kernel,
        out_shape=jax.ShapeDtypeStruct((b, s, d), F32),
        grid=(b, s // tm),
        in_specs=[pl.BlockSpec((1, HALO_ROWS, d), lambda bi, i: (bi, jnp.maximum(i * hb - 1, 0), 0)),
                  pl.BlockSpec((1, tm, d), lambda bi, i: (bi, i, 0)),
                  pl.BlockSpec((1, HALO_ROWS, d),
                               lambda bi, i: (bi, jnp.minimum((i + 1) * hb, n_halo - 1), 0)),
                  pl.BlockSpec((1, tm, d), lambda bi, i: (bi, i, 0)),
                  pl.BlockSpec((1, 6, d), lambda bi, i: (bi, 0, 0)), *specs],
        out_specs=pl.BlockSpec((1, tm, d), lambda bi, i: (bi, i, 0)),
        scratch_shapes=[pltpu.VMEM((d // LANES, tm + 2 * HALO_ROWS, LANES), F32),
                        pltpu.VMEM((d // LANES, tm, LANES), F32)],
        compiler_params=_params(2),
        name="conv_dw_pw2_ffn",
    )(u, u, u, x, ada, *params)


def _trunk(x, ada, biases, p):
    row = lambda v: v.reshape(1, -1)
    for i in range(DEPTH):
        j = i // N_MIXERS
        gain1, gain2 = row(p["norm1_g"][i]), row(p["norm2_g"][i])
        w_in, w_out = _LayerOf(p["ffn_w_in"], i), _LayerOf(p["ffn_w_out"], i)
        if i % N_MIXERS == 0:
            qkvs = [_qkv_proj(x, ada[i], gain1, p["attn_w_qkv"], j, g,
                              row(p["attn_q_gain"][j]), row(p["attn_k_gain"][j]), dil)
                    for g, (_, dil) in enumerate(DILATED_GROUPS)]
            o = _attention(qkvs, biases)
            x = _attn_tail(o, x, ada[i], _LayerOf(p["attn_w_o"], j), gain2, w_in, w_out)
        else:
            u = _conv_head(x, ada[i], gain1, _LayerOf(p["conv_w_pw1"], j),
                           row(p["conv_b_pw1"][j]))
            x = _conv_tail(u, x, ada[i], p["conv_w_dw"][j], row(p["conv_b_dw"][j]),
                           row(p["conv_ln_g"][j]), row(p["conv_ln_b"][j]),
                           _LayerOf(p["conv_w_pw2"], j), row(p["conv_b_pw2"][j]), gain2,
                           w_in, w_out)
    return x


def kernel(x_prompt, x_sample, c_prompt, c_sample, rel_bias_table, norm1_g, norm2_g, ada_w, ada_b, attn_w_qkv, attn_q_gain, attn_k_gain, attn_w_o, conv_w_pw1, conv_b_pw1, conv_w_dw, conv_b_dw, conv_ln_g, conv_ln_b, conv_w_pw2, conv_b_pw2, ffn_w_in, ffn_w_out):
    p = dict(norm1_g=norm1_g, norm2_g=norm2_g, attn_q_gain=attn_q_gain, attn_k_gain=attn_k_gain,
             conv_b_pw1=conv_b_pw1, conv_w_dw=conv_w_dw, conv_b_dw=conv_b_dw, conv_ln_g=conv_ln_g,
             conv_ln_b=conv_ln_b, conv_b_pw2=conv_b_pw2,
             attn_w_qkv=attn_w_qkv.astype(BF16), attn_w_o=attn_w_o.astype(BF16),
             conv_w_pw1=conv_w_pw1.astype(BF16), conv_w_pw2=conv_w_pw2.astype(BF16),
             ffn_w_in=ffn_w_in.astype(BF16), ffn_w_out=ffn_w_out.astype(BF16))
    nb_p = c_prompt.shape[0]
    d = x_prompt.shape[-1]
    c_all = jnp.concatenate([c_prompt, c_sample], axis=0)
    ada = _ada_all(c_all, ada_w, ada_b).reshape(DEPTH, c_all.shape[0], 6, d)

    bias_cache = {}

    def biases_for(seq_len):
        out = []
        for g, (_, dil) in enumerate(DILATED_GROUPS):
            key = (g,) + _attn_geometry(seq_len, dil)
            if key not in bias_cache:
                bias_cache[key] = _expand_bias(rel_bias_table, _bias_bucket_index(seq_len, dil), g)
            out.append(bias_cache[key])
        return out

    y_prompt = _trunk(x_prompt, ada[:, :nb_p], biases_for(x_prompt.shape[1]), p)
    y_sample = _trunk(x_sample, ada[:, nb_p:], biases_for(x_sample.shape[1]), p)
    return (y_prompt, y_sample)
```

```python
import functools
import math
from typing import NamedTuple

import jax
import jax.numpy as jnp
from jax import lax
from jax.experimental import pallas as pl
from jax.experimental.pallas import tpu as pltpu

F32 = jnp.float32
BF16 = jnp.bfloat16

DEPTH = 4
N_MIXERS = 2
DILATED_GROUPS = ((128, 1), (512, 4), (2048, 16))
N_GROUPS = len(DILATED_GROUPS)
HEADS_PER_GROUP = 8
HEAD_DIM = 128
REL_BUCKETS = 32
REL_MAX_DIST = 1024
CONV_WIDTH = 31
CONV_PAD = (CONV_WIDTH - 1) // 2
D_FF = 2816
NEG_INF = -1e30
EPS = 1e-6
LOG2_E = math.log2(math.e)

HALF = DILATED_GROUPS[0][0] // (2 * DILATED_GROUPS[0][1])
assert all(w // (2 * d) == HALF for w, d in DILATED_GROUPS)

TOKEN_TILE = 512
PROJ_TOKEN_TILE = 1024
QUERY_BLOCK = 128
ATTN_ROWS_PER_STEP = 4096
ATTN_BLOCKS_PER_STEP = 32
QUARTERS = 4
HALO_ROWS = 16
CONV_ROW_CHUNK = 64
CONV_LOOP_UNROLL = 2
BF16_SUBLANES = 16
LANES = 128
FF_CHUNKS = ((0, 768), (768, 768), (1536, 768), (2304, 512))
VMEM_LIMIT_BYTES = 60000 * 1024


def _params(n_axes, operands=None):
    fusion = None if operands is None else [isinstance(w, _LayerOf) for w in operands]
    return pltpu.CompilerParams(dimension_semantics=("parallel",) * n_axes,
                                vmem_limit_bytes=VMEM_LIMIT_BYTES, allow_input_fusion=fusion)


class _LayerOf(NamedTuple):
    stack: jax.Array
    layer: int
    cols: int = 0
    col_block: int = 0

    @property
    def shape(self):
        return (self.stack.shape[1], self.cols or self.stack.shape[2])


def _resident(w):
    if isinstance(w, _LayerOf):
        index = (w.layer, 0, w.col_block)
        return pl.BlockSpec((None,) + w.shape, lambda *_: index,
                            pipeline_mode=pl.Buffered(1)), w.stack
    nd = w.ndim
    return pl.BlockSpec(w.shape, lambda *_: (0,) * nd, pipeline_mode=pl.Buffered(1)), w


def _silu(x):
    return x * jax.nn.sigmoid(x)


def _mod_rmsnorm(x, gain, shift, scale):
    ms = jnp.mean(x * x, axis=-1, keepdims=True)
    y = x * lax.rsqrt(ms + EPS) * gain
    return y * (1.0 + scale) + shift


def _ada_kernel(c_ref, w_ref, b_ref, o_ref):
    ca = _silu(c_ref[...]).astype(BF16)
    o_ref[0] = jnp.dot(ca, w_ref[0].astype(BF16), preferred_element_type=F32) + b_ref[0]


def _ada_all(c, ada_w, ada_b):
    nb, d = c.shape
    depth, _, width = ada_w.shape
    tn = 1536
    assert width % tn == 0
    return pl.pallas_call(
        _ada_kernel,
        out_shape=jax.ShapeDtypeStruct((depth, nb, width), F32),
        grid=(depth, width // tn),
        in_specs=[pl.BlockSpec((nb, d), lambda i, j: (0, 0)),
                  pl.BlockSpec((1, d, tn), lambda i, j: (i, 0, j)),
                  pl.BlockSpec((1, 1, tn), lambda i, j: (i, 0, j))],
        out_specs=pl.BlockSpec((1, nb, tn), lambda i, j: (i, 0, j)),
        compiler_params=_params(2),
        name="ada_proj",
    )(c, ada_w, ada_b.reshape(depth, 1, width))


def _t5_bucket(rel):
    half = REL_BUCKETS // 2
    max_exact = half // 2
    ret = jnp.where(rel > 0, half, 0)
    n = jnp.abs(rel)
    nf = jnp.maximum(n, 1).astype(F32)
    large = max_exact + (jnp.log(nf / max_exact) / math.log(REL_MAX_DIST / max_exact)
                         * (half - max_exact)).astype(jnp.int32)
    large = jnp.minimum(large, half - 1)
    return ret + jnp.where(n < max_exact, n, large)


def _attn_geometry(seq_len, dil):
    sub_len = seq_len // dil
    qb = min(QUERY_BLOCK, sub_len)
    kw = min(qb + 2 * HALF, sub_len)
    nblk = sub_len // qb
    assert sub_len * dil == seq_len and nblk * qb == sub_len and qb % HALF == 0
    return sub_len, qb, kw, nblk


def _key_offset(n, qb, kw, sub_len):
    return min(max(n * qb - HALF, 0), sub_len - kw)


def _bias_bucket_index(seq_len, dil):
    sub_len, qb, kw, nblk = _attn_geometry(seq_len, dil)
    slots = []
    for n in (0, min(1, nblk - 1), nblk - 1):
        delta = n * qb - _key_offset(n, qb, kw, sub_len)
        rel = jnp.arange(kw)[None, :] - jnp.arange(qb)[:, None] - delta
        slots.append(jnp.where(jnp.abs(rel) <= HALF, _t5_bucket(rel * dil), -1))
    return jnp.stack(slots).astype(jnp.int32)


def _bias_kernel(tab_ref, idx_ref, o_ref, *, head0):
    h = pl.program_id(0)
    idx = idx_ref[...]
    acc = jnp.full(idx.shape, NEG_INF, F32)
    for b in range(REL_BUCKETS):
        acc = jnp.where(idx == b, tab_ref[b, head0 + h], acc)
    o_ref[0] = acc * LOG2_E


def _expand_bias(rel_table, idx, group):
    _, qb, kw = idx.shape
    return pl.pallas_call(
        functools.partial(_bias_kernel, head0=group * HEADS_PER_GROUP),
        out_shape=jax.ShapeDtypeStruct((HEADS_PER_GROUP, 3, qb, kw), F32),
        grid=(HEADS_PER_GROUP,),
        in_specs=[pl.BlockSpec(memory_space=pltpu.SMEM),
                  pl.BlockSpec((3, qb, kw), lambda h: (0, 0, 0))],
        out_specs=pl.BlockSpec((1, 3, qb, kw), lambda h: (h, 0, 0, 0)),
        compiler_params=_params(1),
        name="rel_bias_expand",
    )(rel_table, idx)


def _qkv_kernel(x_ref, ada_ref, g_ref, wq_ref, wk_ref, wv_ref, qg_ref, kg_ref, o_ref, *scratch,
                dil, two_level):
    tm, d = x_ref.shape[1], x_ref.shape[2]
    sub = tm // dil
    h = _mod_rmsnorm(x_ref[0], g_ref[...], ada_ref[0, 0:1, :], ada_ref[0, 1:2, :])
    if dil == 1:
        h = h.astype(BF16)
    elif two_level:
        step = math.isqrt(dil)
        slab_ref, perm_ref, mid_ref = scratch
        part = tm // step
        for c in range(d // LANES):
            slab_ref[c] = h[:, c * LANES:(c + 1) * LANES]
        for c in range(d // LANES):
            for r0 in range(step):
                mid_ref[c, r0 * part:(r0 + 1) * part, :] = slab_ref[c, pl.ds(r0, part, stride=step), :]
        for c in range(d // LANES):
            for r in range(dil):
                r0, r1 = r % step, r // step
                perm_ref[r * sub:(r + 1) * sub, c * LANES:(c + 1) * LANES] = (
                    mid_ref[c, pl.ds(r0 * part + r1, sub, stride=step), :].astype(BF16))
        h = perm_ref[...]
    else:
        slab_ref, perm_ref = scratch
        for c in range(d // LANES):
            slab_ref[c] = h[:, c * LANES:(c + 1) * LANES]
        for c in range(d // LANES):
            for r in range(dil):
                perm_ref[r * sub:(r + 1) * sub, c * LANES:(c + 1) * LANES] = (
                    slab_ref[c, pl.ds(r, sub, stride=dil), :].astype(BF16))
        h = perm_ref[...]
    for t, w_ref in enumerate((wq_ref, wk_ref, wv_ref)):
        acc = jnp.dot(h, w_ref[...], preferred_element_type=F32)
        for hd in range(HEADS_PER_GROUP):
            c = acc[:, hd * HEAD_DIM:(hd + 1) * HEAD_DIM]
            if t < 2:
                ms = jnp.mean(c * c, axis=-1, keepdims=True)
                c = c * lax.rsqrt(ms + EPS) * (qg_ref[...] if t == 0 else kg_ref[...])
            if t == 0:
                c = c * (HEAD_DIM ** -0.5 * LOG2_E)
            o_ref[0, t, hd] = c.astype(BF16).reshape(dil, sub, HEAD_DIM)


def _qkv_proj(x, ada, gain, w_qkv, layer, group, q_gain, k_gain, dil):
    b, s, d = x.shape
    sub_len = s // dil
    tm = min(PROJ_TOKEN_TILE, s)
    assert s % tm == 0 and tm % (dil * BF16_SUBLANES) == 0
    scratch = [] if dil == 1 else [pltpu.VMEM((d // LANES, tm, LANES), F32),
                                   pltpu.VMEM((tm, d), BF16)]
    two_level = dil > 4 and math.isqrt(dil) ** 2 == dil
    if two_level:
        scratch.append(pltpu.VMEM((d // LANES, tm, LANES), F32))
    width = HEADS_PER_GROUP * HEAD_DIM
    w_q, w_k, w_v = (_LayerOf(w_qkv, layer, width, t * N_GROUPS + group) for t in range(3))
    specs, params = zip(*[_resident(p) for p in (gain, w_q, w_k, w_v, q_gain, k_gain)])
    return pl.pallas_call(
        functools.partial(_qkv_kernel, dil=dil, two_level=two_level),
        out_shape=jax.ShapeDtypeStruct((b, 3, HEADS_PER_GROUP, dil, sub_len, HEAD_DIM), BF16),
        grid=(b, s // tm),
        in_specs=[pl.BlockSpec((1, tm, d), lambda bi, i: (bi, i, 0)),
                  pl.BlockSpec((1, 6, d), lambda bi, i: (bi, 0, 0)), *specs],
        out_specs=pl.BlockSpec((1, 3, HEADS_PER_GROUP, dil, tm // dil, HEAD_DIM),
                               lambda bi, i: (bi, 0, 0, 0, i, 0)),
        scratch_shapes=scratch,
        compiler_params=_params(2, (x, ada, gain, w_q, w_k, w_v, q_gain, k_gain)),
        name=f"qkv_proj_d{dil}",
    )(x, ada, *params)


def _attn_kernel(qkv0_ref, qkv1_ref, qkv2_ref, b0_ref, b1_ref, b2_ref, o_ref, og_ref, mg_ref,
                 lg_ref, fo_ref, *, seq_len):
    heads = o_ref.shape[1]
    if heads == 1:
        _attn_one_head(0, qkv0_ref, qkv1_ref, qkv2_ref, b0_ref, b1_ref, b2_ref, o_ref, og_ref,
                       mg_ref, lg_ref, fo_ref, seq_len)
    else:
        def head(hh, carry):
            _attn_one_head(hh, qkv0_ref, qkv1_ref, qkv2_ref, b0_ref, b1_ref, b2_ref, o_ref, og_ref,
                           mg_ref, lg_ref, fo_ref, seq_len)
            return carry
        lax.fori_loop(0, heads, head, 0)


def _quarter_major(dil):
    return dil % QUARTERS == 0


def _attn_one_head(hh, qkv0_ref, qkv1_ref, qkv2_ref, b0_ref, b1_ref, b2_ref, o_ref, og_ref, mg_ref,
                   lg_ref, fo_ref, seq_len):
    for g, (qkv_ref, bias_ref) in enumerate(((qkv0_ref, b0_ref), (qkv1_ref, b1_ref),
                                              (qkv2_ref, b2_ref))):
        dil = DILATED_GROUPS[g][1]
        sub_len, qb, kw, nblk = _attn_geometry(seq_len, dil)

        def block(r, n, g=g, dil=dil, sub_len=sub_len, qb=qb, kw=kw, nblk=nblk,
                  qkv_ref=qkv_ref, bias_ref=bias_ref):
            q0 = pl.multiple_of(r * sub_len + n * qb, qb)
            koff = jnp.clip(n * qb - HALF, 0, sub_len - kw)
            k0 = pl.multiple_of(r * sub_len + koff, HALF)
            q = qkv_ref[0, 0, hh, pl.ds(q0, qb), :]
            k = qkv_ref[0, 1, hh, pl.ds(k0, kw), :]
            v = qkv_ref[0, 2, hh, pl.ds(k0, kw), :]
            slot = jnp.where(n == 0, 0, jnp.where(n == nblk - 1, 2, 1))
            s = lax.dot_general(q, k, (((1,), (1,)), ((), ())), preferred_element_type=F32)
            s = s + bias_ref[hh, slot]
            m = jnp.max(s, axis=-1, keepdims=True)
            p = jnp.exp2(s - m).astype(BF16)
            v_ones = jnp.concatenate([v, jnp.ones((kw, HEAD_DIM), BF16)], axis=1)
            o_l = jnp.dot(p, v_ones, preferred_element_type=F32)
            if _quarter_major(dil):
                step4 = dil // QUARTERS
                if step4 == 1:
                    rows = pl.ds(pl.multiple_of(r * (seq_len // QUARTERS) + n * qb, qb), qb)
                else:
                    start = (lax.rem(r, QUARTERS) * (seq_len // QUARTERS) + lax.div(r, QUARTERS)
                             + n * (qb * step4))
                    rows = pl.ds(start, qb, stride=step4)
            else:
                start = r + n * (qb * dil)
                rows = pl.ds(start, qb) if dil == 1 else pl.ds(start, qb, stride=dil)
            og_ref[g, rows, :] = o_l[:, :HEAD_DIM]
            mg_ref[g, rows, :] = jnp.broadcast_to(m, (qb, HEAD_DIM))
            lg_ref[g, rows, :] = o_l[:, HEAD_DIM:]

        total = dil * nblk
        par = math.gcd(total, ATTN_BLOCKS_PER_STEP)

        def step(it, carry, block=block, nblk=nblk, par=par):
            for u in range(par):
                idx = it * par + u
                block(lax.div(idx, nblk), lax.rem(idx, nblk))
            return carry

        lax.fori_loop(0, total // par, step, 0)

    rc = 256

    def merge(c, carry):
        sub_rows = rc // QUARTERS
        for q in range(QUARTERS):
            token_rows = pl.ds(c * rc + q, sub_rows, stride=QUARTERS)
            quarter_rows = pl.ds(pl.multiple_of(c * sub_rows, sub_rows) + q * (seq_len // QUARTERS),
                                 sub_rows)
            rows = [quarter_rows if _quarter_major(dil) else token_rows
                    for _, dil in DILATED_GROUPS]
            ms = [mg_ref[g, rows[g], :] for g in range(N_GROUPS)]
            mx = jnp.maximum(jnp.maximum(ms[0], ms[1]), ms[2])
            num = den = None
            for g in range(N_GROUPS):
                w = jnp.exp2(ms[g] - mx)
                num = w * og_ref[g, rows[g], :] if num is None else num + w * og_ref[g, rows[g], :]
                den = w * lg_ref[g, rows[g], :] if den is None else den + w * lg_ref[g, rows[g], :]
            fo_ref[token_rows, :] = num / den
        rows = pl.ds(pl.multiple_of(c * rc, rc), rc)
        o_ref[0, hh, rows, :] = fo_ref[rows, :].astype(BF16)
        return carry

    lax.fori_loop(0, seq_len // rc, merge, 0, unroll=4)


def _attention(qkvs, biases):
    b = qkvs[0].shape[0]
    s = qkvs[0].shape[3] * qkvs[0].shape[4]
    qkvs = [t.reshape(b, 3, HEADS_PER_GROUP, s, HEAD_DIM) for t in qkvs]
    hb = max(1, min(HEADS_PER_GROUP, ATTN_ROWS_PER_STEP // s))
    assert HEADS_PER_GROUP % hb == 0
    qkv_spec = pl.BlockSpec((1, 3, hb, s, HEAD_DIM), lambda h, bi: (bi, 0, h, 0, 0))
    bias_specs = [pl.BlockSpec((hb,) + t.shape[1:], lambda h, bi: (h, 0, 0, 0)) for t in biases]
    return pl.pallas_call(
        functools.partial(_attn_kernel, seq_len=s),
        out_shape=jax.ShapeDtypeStruct((b, HEADS_PER_GROUP, s, HEAD_DIM), BF16),
        grid=(HEADS_PER_GROUP // hb, b),
        in_specs=[qkv_spec] * N_GROUPS + bias_specs,
        out_specs=pl.BlockSpec((1, hb, s, HEAD_DIM), lambda h, bi: (bi, h, 0, 0)),
        scratch_shapes=[pltpu.VMEM((N_GROUPS, s, HEAD_DIM), F32)] * 3
                       + [pltpu.VMEM((s, HEAD_DIM), F32)],
        compiler_params=_params(2),
        name="dilated_attention",
    )(*qkvs, *biases)


def _residual_ffn(x, y, ada_ref, g2_ref, w_in_ref, w_out_ref):
    x1 = x + ada_ref[0, 2:3, :] * y
    h = _mod_rmsnorm(x1, g2_ref[...], ada_ref[0, 3:4, :], ada_ref[0, 4:5, :]).astype(BF16)
    acc = None
    for c0, cw in FF_CHUNKS:
        gate = jnp.dot(h, w_in_ref[:, c0:c0 + cw], preferred_element_type=F32)
        up = jnp.dot(h, w_in_ref[:, D_FF + c0:D_FF + c0 + cw], preferred_element_type=F32)
        a = (_silu(gate) * up).astype(BF16)
        part = jnp.dot(a, w_out_ref[c0:c0 + cw, :], preferred_element_type=F32)
        acc = part if acc is None else acc + part
    return x1 + ada_ref[0, 5:6, :] * acc


def _attn_tail_kernel(o_ref, x_ref, ada_ref, wo_ref, g2_ref, w_in_ref, w_out_ref, out_ref):
    o = jnp.concatenate([o_ref[0, h] for h in range(HEADS_PER_GROUP)], axis=-1)
    y = jnp.dot(o, wo_ref[...], preferred_element_type=F32)
    out_ref[0] = _residual_ffn(x_ref[0], y, ada_ref, g2_ref, w_in_ref, w_out_ref)


def _attn_tail(o, x, ada, w_o, gain2, w_in, w_out):
    b, s, d = x.shape
    tm = min(TOKEN_TILE, s)
    specs, params = zip(*[_resident(p) for p in (w_o, gain2, w_in, w_out)])
    return pl.pallas_call(
        _attn_tail_kernel,
        out_shape=jax.ShapeDtypeStruct((b, s, d), F32),
        grid=(b, s // tm),
        in_specs=[pl.BlockSpec((1, HEADS_PER_GROUP, tm, HEAD_DIM), lambda bi, i: (bi, 0, i, 0)),
                  pl.BlockSpec((1, tm, d), lambda bi, i: (bi, i, 0)),
                  pl.BlockSpec((1, 6, d), lambda bi, i: (bi, 0, 0)), *specs],
        out_specs=pl.BlockSpec((1, tm, d), lambda bi, i: (bi, i, 0)),
        compiler_params=_params(2, (o, x, ada, w_o, gain2, w_in, w_out)),
        name="attn_out_ffn",
    )(o, x, ada, *params)


def _conv_head_kernel(x_ref, ada_ref, g_ref, w_ref, b_ref, u_ref):
    d = x_ref.shape[-1]
    h = _mod_rmsnorm(x_ref[0], g_ref[...], ada_ref[0, 0:1, :], ada_ref[0, 1:2, :]).astype(BF16)
    a = jnp.dot(h, w_ref[:, :d], preferred_element_type=F32) + b_ref[:, :d]
    gt = jnp.dot(h, w_ref[:, d:], preferred_element_type=F32) + b_ref[:, d:]
    u_ref[0] = a * jax.nn.sigmoid(gt)


def _conv_head(x, ada, gain, w_pw1, b_pw1):
    b, s, d = x.shape
    tm = min(PROJ_TOKEN_TILE, s)
    specs, params = zip(*[_resident(p) for p in (gain, w_pw1, b_pw1)])
    return pl.pallas_call(
        _conv_head_kernel,
        out_shape=jax.ShapeDtypeStruct((b, s, d), F32),
        grid=(b, s // tm),
        in_specs=[pl.BlockSpec((1, tm, d), lambda bi, i: (bi, i, 0)),
                  pl.BlockSpec((1, 6, d), lambda bi, i: (bi, 0, 0)), *specs],
        out_specs=pl.BlockSpec((1, tm, d), lambda bi, i: (bi, i, 0)),
        compiler_params=_params(2, (x, ada, gain, w_pw1, b_pw1)),
        name="conv_pw1_glu",
    )(x, ada, *params)


def _conv_tail_kernel(prev_ref, u_ref, next_ref, x_ref, ada_ref, wdw_ref, bdw_ref, lng_ref,
                      lnb_ref, w2_ref, b2_ref, g2_ref, w_in_ref, w_out_ref, out_ref,
                      win_ref, cv_ref):
    tm, d = u_ref.shape[1], u_ref.shape[2]
    i = pl.program_id(1)
    last = pl.num_programs(1) - 1
    prev = jnp.where(i > 0, prev_ref[0], 0.0)
    nxt = jnp.where(i < last, next_ref[0], 0.0)
    for c in range(d // LANES):
        lanes = slice(c * LANES, (c + 1) * LANES)
        win_ref[c, 0:HALO_ROWS, :] = prev[:, lanes]
        win_ref[c, HALO_ROWS:HALO_ROWS + tm, :] = u_ref[0, :, lanes]
        win_ref[c, HALO_ROWS + tm:, :] = nxt[:, lanes]

    for c in range(d // LANES):
        lanes = slice(c * LANES, (c + 1) * LANES)

        def conv_rows(j, carry, c=c, lanes=lanes):
            r0 = j * (2 * CONV_ROW_CHUNK)
            for parity in range(2):
                acc = None
                for k in range(CONV_WIDTH):
                    start = r0 + (parity + HALO_ROWS - CONV_PAD + k)
                    tap = (win_ref[c, pl.ds(start, CONV_ROW_CHUNK, stride=2), :]
                           * wdw_ref[k:k + 1, lanes])
                    acc = tap if acc is None else acc + tap
                cv_ref[c, pl.ds(r0 + parity, CONV_ROW_CHUNK, stride=2), :] = acc
            return carry

        lax.fori_loop(0, tm // (2 * CONV_ROW_CHUNK), conv_rows, 0, unroll=CONV_LOOP_UNROLL)

    cv = jnp.concatenate([cv_ref[c] for c in range(d // LANES)], axis=1) + bdw_ref[...]
    mu = jnp.mean(cv, axis=-1, keepdims=True)
    xc = cv - mu
    var = jnp.mean(xc * xc, axis=-1, keepdims=True)
    ln = xc * lax.rsqrt(var + EPS) * lng_ref[...] + lnb_ref[...]
    y = jnp.dot(_silu(ln).astype(BF16), w2_ref[...], preferred_element_type=F32) + b2_ref[...]
    out_ref[0] = _residual_ffn(x_ref[0], y, ada_ref, g2_ref, w_in_ref, w_out_ref)


def _conv_tail(u, x, ada, w_dw, b_dw, ln_g, ln_b, w_pw2, b_pw2, gain2, w_in, w_out):
    b, s, d = x.shape
    tm = min(TOKEN_TILE, s)
    assert s % tm == 0 and tm % (2 * CONV_ROW_CHUNK * CONV_LOOP_UNROLL) == 0
    assert HALO_ROWS >= CONV_PAD + 1
    hb = tm // HALO_ROWS
    n_halo = s // HALO_ROWS
    specs, params = zip(*[_resident(p) for p in (w_dw, b_dw, ln_g, ln_b, w_pw2, b_pw2, gain2,
                                                 w_in, w_out)])
    return pl.pallas_call(
        _conv_tail_kernel,
        out_shape=jax.ShapeDtypeStruct((b, s, d), F32),
        grid=(b, s // tm),
        in_specs=[pl.BlockSpec((1, HALO_ROWS, d), lambda bi, i: (bi, jnp.maximum(i * hb - 1, 0), 0)),
                  pl.BlockSpec((1, tm, d), lambda bi, i: (bi, i, 0)),
                  pl.BlockSpec((1, HALO_ROWS, d),
                               lambda bi, i: (bi, jnp.minimum((i + 1) * hb, n_halo - 1), 0)),
                  pl.BlockSpec((1, tm, d), lambda bi, i: (bi, i, 0)),
                  pl.BlockSpec((1, 6, d), lambda bi, i: (bi, 0, 0)), *specs],
        out_specs=pl.BlockSpec((1, tm, d), lambda bi, i: (bi, i, 0)),
        scratch_shapes=[pltpu.VMEM((d // LANES, tm + 2 * HALO_ROWS, LANES), F32),
                        pltpu.VMEM((d // LANES, tm, LANES), F32)],
        compiler_params=_params(2, (u, u, u, x, ada, w_dw, b_dw, ln_g, ln_b, w_pw2, b_pw2, gain2,
                                    w_in, w_out)),
        name="conv_dw_pw2_ffn",
    )(u, u, u, x, ada, *params)


def _trunk(x, ada, biases, p):
    row = lambda v: v.reshape(1, -1)
    for i in range(DEPTH):
        j = i // N_MIXERS
        gain1, gain2 = row(p["norm1_g"][i]), row(p["norm2_g"][i])
        w_in, w_out = _LayerOf(p["ffn_w_in"], i), _LayerOf(p["ffn_w_out"], i)
        if i % N_MIXERS == 0:
            qkvs = [_qkv_proj(x, ada[i], gain1, p["attn_w_qkv"], j, g,
                              row(p["attn_q_gain"][j]), row(p["attn_k_gain"][j]), dil)
                    for g, (_, dil) in enumerate(DILATED_GROUPS)]
            o = _attention(qkvs, biases)
            x = _attn_tail(o, x, ada[i], _LayerOf(p["attn_w_o"], j), gain2, w_in, w_out)
        else:
            u = _conv_head(x, ada[i], gain1, _LayerOf(p["conv_w_pw1"], j),
                           row(p["conv_b_pw1"][j]))
            x = _conv_tail(u, x, ada[i], p["conv_w_dw"][j], row(p["conv_b_dw"][j]),
                           row(p["conv_ln_g"][j]), row(p["conv_ln_b"][j]),
                           _LayerOf(p["conv_w_pw2"], j), row(p["conv_b_pw2"][j]), gain2,
                           w_in, w_out)
    return x


def kernel(x_prompt, x_sample, c_prompt, c_sample, rel_bias_table, norm1_g, norm2_g, ada_w, ada_b, attn_w_qkv, attn_q_gain, attn_k_gain, attn_w_o, conv_w_pw1, conv_b_pw1, conv_w_dw, conv_b_dw, conv_ln_g, conv_ln_b, conv_w_pw2, conv_b_pw2, ffn_w_in, ffn_w_out):
    p = dict(norm1_g=norm1_g, norm2_g=norm2_g, attn_q_gain=attn_q_gain, attn_k_gain=attn_k_gain,
             conv_b_pw1=conv_b_pw1, conv_w_dw=conv_w_dw, conv_b_dw=conv_b_dw, conv_ln_g=conv_ln_g,
             conv_ln_b=conv_ln_b, conv_b_pw2=conv_b_pw2,
             attn_w_qkv=attn_w_qkv.astype(BF16), attn_w_o=attn_w_o.astype(BF16),
             conv_w_pw1=conv_w_pw1.astype(BF16), conv_w_pw2=conv_w_pw2.astype(BF16),
             ffn_w_in=ffn_w_in.astype(BF16), ffn_w_out=ffn_w_out.astype(BF16))
    nb_p = c_prompt.shape[0]
    d = x_prompt.shape[-1]
    c_all = jnp.concatenate([c_prompt, c_sample], axis=0)
    ada = _ada_all(c_all, ada_w, ada_b).reshape(DEPTH, c_all.shape[0], 6, d)

    bias_cache = {}

    def biases_for(seq_len):
        out = []
        for g, (_, dil) in enumerate(DILATED_GROUPS):
            key = (g,) + _attn_geometry(seq_len, dil)
            if key not in bias_cache:
                bias_cache[key] = _expand_bias(rel_bias_table, _bias_bucket_index(seq_len, dil), g)
            out.append(bias_cache[key])
        return out

    y_prompt = _trunk(x_prompt, ada[:, :nb_p], biases_for(x_prompt.shape[1]), p)
    y_sample = _trunk(x_sample, ada[:, nb_p:], biases_for(x_sample.shape[1]), p)
    return (y_prompt, y_sample)
```
